```python
import jax, jax.numpy as jnp
from jax import lax
import numpy as np

D_MODEL = 2048
BATCH = 2
SEQ = 8192
DEPTH = 2

EPS = 1e-6
Q_BLOCK = 128
MLA_HEADS = 8
Q_LORA = 512
KV_LORA = 512
NOPE_DIM = 128
ROPE_DIM = 64
MLA_V_DIM = 128
ROPE_THETA = 10000.0
MLA_OUT = MLA_HEADS * MLA_V_DIM
DIL_WINDOWS = (128, 512, 2048)
DIL_RATES = (1, 4, 16)
DIL_GROUPS = 3
DIL_HEADS = 4
DIL_HEAD_DIM = 128
DIL_KEYS = DIL_WINDOWS[0] // DIL_RATES[0] + 1
DIL_WIDTH = DIL_GROUPS * DIL_HEADS * DIL_HEAD_DIM
DIL_OUT = DIL_HEADS * DIL_HEAD_DIM
ALIBI_MAX_BIAS = 8.0
SB_HEADS = 8
SB_HEAD_DIM = 128
SB_WIDTH = SB_HEADS * SB_HEAD_DIM
N_BRANCHES = 3
IN_SIZES = (Q_LORA, KV_LORA, ROPE_DIM, 3 * DIL_WIDTH, 3 * SB_WIDTH, N_BRANCHES * D_MODEL)
IN_COLS = Q_LORA + KV_LORA + ROPE_DIM + 3 * DIL_WIDTH + 3 * SB_WIDTH + N_BRANCHES * D_MODEL
N_EXPERTS = 16
N_GROUPS = 4
EXPERTS_PER_GROUP = N_EXPERTS // N_GROUPS
TOP_K = 2
D_EXPERT = 512

kernel_name = 'hybrid_gated_mla_dilated_stickbreaking_grouped_moe'


def rmsnorm(x, g):
    xf = x.astype(jnp.float32)
    y = xf * lax.rsqrt(jnp.mean(xf * xf, axis=-1, keepdims=True) + EPS)
    return (y * g.astype(jnp.float32)).astype(x.dtype)


def rope(x, positions):
    half = ROPE_DIM // 2
    inv_freq = ROPE_THETA ** (-jnp.arange(half, dtype=jnp.float32) / half)
    ang = positions.astype(jnp.float32)[..., None] * inv_freq
    ang = ang.reshape(ang.shape[:2] + (1,) * (x.ndim - 3) + (half,))
    cos, sin = jnp.cos(ang), jnp.sin(ang)
    x1 = x[..., :half].astype(jnp.float32)
    x2 = x[..., half:].astype(jnp.float32)
    return jnp.concatenate([x1 * cos - x2 * sin, x2 * cos + x1 * sin], axis=-1).astype(x.dtype)


def to_blocks(t):
    b, s = t.shape[:2]
    return jnp.moveaxis(t.reshape((b, s // Q_BLOCK, Q_BLOCK) + t.shape[2:]), 1, 0)


def from_blocks(t):
    nb, b = t.shape[:2]
    return jnp.moveaxis(t, 0, 1).reshape((b, nb * Q_BLOCK) + t.shape[3:])


def mla_attention(c_q, c_kv, k_rope, positions, g_q, w_uq, g_kv, w_ukv):
    b, s, _ = c_q.shape
    q = (rmsnorm(c_q, g_q) @ w_uq).reshape(b, s, MLA_HEADS, NOPE_DIM + ROPE_DIM)
    q_nope, q_rope = q[..., :NOPE_DIM], rope(q[..., NOPE_DIM:], positions)
    kv = (rmsnorm(c_kv, g_kv) @ w_ukv).reshape(b, s, MLA_HEADS, NOPE_DIM + MLA_V_DIM)
    k_nope, v = kv[..., :NOPE_DIM], kv[..., NOPE_DIM:]
    k_rope = rope(k_rope, positions)
    scale = (NOPE_DIM + ROPE_DIM) ** -0.5
    key_idx = jnp.arange(s)

    def block(args):
        qn, qr, i = args
        t = i * Q_BLOCK + jnp.arange(Q_BLOCK)
        z = (jnp.einsum('bqhd,bkhd->bhqk', qn, k_nope)
             + jnp.einsum('bqhr,bkr->bhqk', qr, k_rope)).astype(jnp.float32) * scale
        z = jnp.where(key_idx[None, :] <= t[:, None], z, -jnp.inf)
        p = jax.nn.softmax(z, axis=-1).astype(v.dtype)
        return jnp.einsum('bhqk,bkhd->bqhd', p, v)

    o = lax.map(block, (to_blocks(q_nope), to_blocks(q_rope), jnp.arange(s // Q_BLOCK)))
    return from_blocks(o).reshape(b, s, MLA_OUT)


def dilated_attention(q, k, v, positions):
    b, s = q.shape[:2]
    rates = jnp.array(DIL_RATES, dtype=jnp.int32)
    n_h = DIL_GROUPS * DIL_HEADS
    slopes = (2.0 ** (-ALIBI_MAX_BIAS * jnp.arange(1, n_h + 1, dtype=jnp.float32) / n_h)).reshape(DIL_GROUPS, DIL_HEADS)
    scale = DIL_HEAD_DIM ** -0.5
    offsets = jnp.arange(DIL_KEYS)
    pos_f = positions.astype(jnp.float32)

    def group(qg, kg, vg, rate, slope, t):
        idx = t[:, None] - offsets[None, :] * rate
        valid = idx >= 0
        idx = jnp.maximum(idx, 0)
        kk = kg[:, idx]
        vv = vg[:, idx]
        z = jnp.einsum('bqhd,bqjhd->bhqj', qg, kk).astype(jnp.float32) * scale
        dist = jnp.abs(pos_f[:, t][:, :, None] - pos_f[:, idx])
        z = z - slope[None, :, None, None] * dist[:, None]
        z = jnp.where(valid[None, None], z, -jnp.inf)
        m = jnp.max(z, axis=-1, keepdims=True)
        e = jnp.exp(z - m)
        den = jnp.sum(e, axis=-1, keepdims=True)
        o = jnp.einsum('bhqj,bqjhd->bqhd', (e / den).astype(vg.dtype), vv)
        lse = (m + jnp.log(den))[..., 0]
        return o, jnp.moveaxis(lse, 1, 2)

    def block(args):
        qb, i = args
        t = i * Q_BLOCK + jnp.arange(Q_BLOCK)
        o, lse = jax.vmap(group, in_axes=(2, 2, 2, 0, 0, None), out_axes=2)(qb, k, v, rates, slopes, t)
        w = jax.nn.softmax(lse, axis=2)
        return jnp.einsum('bqgh,bqghd->bqhd', w.astype(o.dtype), o)

    o = lax.map(block, (to_blocks(q), jnp.arange(s // Q_BLOCK)))
    return from_blocks(o).reshape(b, s, DIL_OUT)


def stick_breaking_attention(q, k, v):
    b, s = q.shape[:2]
    scale = SB_HEAD_DIM ** -0.5
    key_idx = jnp.arange(s)

    def block(args):
        qb, i = args
        t = i * Q_BLOCK + jnp.arange(Q_BLOCK)
        z = jnp.einsum('bqhd,bkhd->bhqk', qb, k).astype(jnp.float32) * scale
        before = key_idx[None, :] < t[:, None]
        log_keep = jnp.where(before, jax.nn.log_sigmoid(-z), 0.0)
        later = lax.cumsum(log_keep, axis=3, reverse=True) - log_keep
        log_a = jnp.where(before, jax.nn.log_sigmoid(z) + later, -jnp.inf)
        a = jnp.exp(log_a).astype(v.dtype)
        return jnp.einsum('bhqk,bkhd->bqhd', a, v)

    o = lax.map(block, (to_blocks(q), jnp.arange(s // Q_BLOCK)))
    return from_blocks(o).reshape(b, s, SB_WIDTH)


def grouped_moe(a, w_router, b_router, w_gate, w_up, w_down):
    b, s, d = a.shape
    xt = a.reshape(b * s, d)
    scores = jax.nn.sigmoid((xt @ w_router).astype(jnp.float32))
    grouped = (scores + b_router.astype(jnp.float32)).reshape(-1, N_GROUPS, EXPERTS_PER_GROUP)
    group_score = jnp.sum(lax.top_k(grouped, TOP_K)[0], axis=-1)
    g_sel = jnp.argmax(group_score, axis=-1)
    in_group = jnp.take_along_axis(grouped, g_sel[:, None, None], axis=1)[:, 0]
    _, local = lax.top_k(in_group, TOP_K)
    expert_idx = g_sel[:, None] * EXPERTS_PER_GROUP + local
    sel = jnp.take_along_axis(scores, expert_idx, axis=1)
    weights = sel / jnp.sum(sel, axis=-1, keepdims=True)
    gates = jnp.sum(jax.nn.one_hot(expert_idx, N_EXPERTS, dtype=jnp.float32) * weights[..., None], axis=1)
    y = jnp.zeros((b * s, d), jnp.float32)
    for e in range(N_EXPERTS):
        h = jax.nn.silu(xt @ w_gate[e]) * (xt @ w_up[e])
        y = y + gates[:, e:e + 1] * (h @ w_down[e]).astype(jnp.float32)
    return y.astype(a.dtype).reshape(b, s, d)


def setup_inputs(seed: int = 0) -> dict:
    key = jax.random.key(seed)
    ks = jax.random.split(key, 24)
    f32 = jnp.float32

    def nrm(k, shape, fan_in, mult=1.0):
        return jax.random.normal(k, shape, f32) * (mult * fan_in ** -0.5)

    def gain(k, shape):
        return 1.0 + 0.01 * jax.random.normal(k, shape, f32)

    positions = (jnp.arange(SEQ, dtype=jnp.int32)[None, :]
                 + jax.random.randint(ks[2], (BATCH, 1), 0, 1024, dtype=jnp.int32))
    return {
        'x': jax.random.normal(ks[0], (BATCH, SEQ, D_MODEL), f32),
        'c': jax.random.normal(ks[1], (BATCH, D_MODEL), f32),
        'positions': positions,
        'w_ada': nrm(ks[3], (DEPTH, D_MODEL, 6 * D_MODEL), D_MODEL, 0.5),
        'b_ada': 0.01 * jax.random.normal(ks[4], (DEPTH, 6 * D_MODEL), f32),
        'g_mix': gain(ks[5], (DEPTH, D_MODEL)),
        'g_moe': gain(ks[6], (DEPTH, D_MODEL)),
        'w_in': nrm(ks[7], (DEPTH, D_MODEL, IN_COLS), D_MODEL),
        'g_q': gain(ks[8], (DEPTH, Q_LORA)),
        'w_uq': nrm(ks[9], (DEPTH, Q_LORA, MLA_HEADS * (NOPE_DIM + ROPE_DIM)), Q_LORA),
        'g_kv': gain(ks[10], (DEPTH, KV_LORA)),
        'w_ukv': nrm(ks[11], (DEPTH, KV_LORA, MLA_HEADS * (NOPE_DIM + MLA_V_DIM)), KV_LORA),
        'w_o_mla': nrm(ks[12], (DEPTH, MLA_OUT, D_MODEL), MLA_OUT),
        'w_o_dil': nrm(ks[13], (DEPTH, DIL_OUT, D_MODEL), DIL_OUT),
        'w_o_sb': nrm(ks[14], (DEPTH, SB_WIDTH, D_MODEL), SB_WIDTH),
        'w_out': nrm(ks[15], (DEPTH, D_MODEL, D_MODEL), D_MODEL),
        'w_router': nrm(ks[16], (D_MODEL, N_EXPERTS), D_MODEL),
        'b_router': 0.01 * jax.random.normal(ks[17], (N_EXPERTS,), f32),
        'w_gate': nrm(ks[18], (DEPTH, N_EXPERTS, D_MODEL, D_EXPERT), D_MODEL),
        'w_up': nrm(ks[19], (DEPTH, N_EXPERTS, D_MODEL, D_EXPERT), D_MODEL),
        'w_down': nrm(ks[20], (DEPTH, N_EXPERTS, D_EXPERT, D_MODEL), D_EXPERT),
        'g_final': gain(ks[21], (D_MODEL,)),
    }


def reference(x, c, positions, w_ada, b_ada, g_mix, g_moe, w_in, g_q, w_uq, g_kv, w_ukv,
              w_o_mla, w_o_dil, w_o_sb, w_out, w_router, b_router, w_gate, w_up, w_down, g_final):
    b, s, d = x.shape
    split_at = [int(v) for v in np.cumsum(IN_SIZES)[:-1]]
    c_act = jax.nn.silu(c)
    for l in range(DEPTH):
        mod = (c_act @ w_ada[l] + b_ada[l])[:, None, :]
        shift_a, scale_a, gate_a, shift_m, scale_m, gate_m = jnp.split(mod, 6, axis=-1)

        a = rmsnorm(x, g_mix[l]) * (1 + scale_a) + shift_a
        proj = a @ w_in[l]
        c_q, c_kv, k_rope, qkv_dil, qkv_sb, gate_logits = jnp.split(proj, split_at, axis=-1)

        y_mla = mla_attention(c_q, c_kv, k_rope, positions, g_q[l], w_uq[l], g_kv[l], w_ukv[l])
        qkv_dil = qkv_dil.reshape(b, s, 3, DIL_GROUPS, DIL_HEADS, DIL_HEAD_DIM)
        y_dil = dilated_attention(qkv_dil[:, :, 0], qkv_dil[:, :, 1], qkv_dil[:, :, 2], positions)
        qkv_sb = qkv_sb.reshape(b, s, 3, SB_HEADS, SB_HEAD_DIM)
        y_sb = stick_breaking_attention(qkv_sb[:, :, 0], qkv_sb[:, :, 1], qkv_sb[:, :, 2])

        g = jax.nn.sigmoid(gate_logits.reshape(b, s, N_BRANCHES, d))
        merged = (g[:, :, 0] * (y_mla @ w_o_mla[l])
                  + g[:, :, 1] * (y_dil @ w_o_dil[l])
                  + g[:, :, 2] * (y_sb @ w_o_sb[l]))
        x = x + gate_a * (merged @ w_out[l])

        a2 = rmsnorm(x, g_moe[l]) * (1 + scale_m) + shift_m
        x = x + gate_m * grouped_moe(a2, w_router, b_router, w_gate[l], w_up[l], w_down[l])
    return rmsnorm(x, g_final)
```

```python
import functools
import math

import jax
import jax.numpy as jnp
from jax import lax
from jax.experimental import pallas as pl
from jax.experimental.pallas import tpu as pltpu

D_MODEL = 2048
EPS = 1e-6
MLA_HEADS = 8
Q_LORA = 512
KV_LORA = 512
NOPE_DIM = 128
ROPE_DIM = 64
MLA_V_DIM = 128
ROPE_THETA = 10000.0
MLA_QK = NOPE_DIM + ROPE_DIM
DIL_WINDOWS = (128, 512, 2048)
DIL_RATES = (1, 4, 16)
DIL_GROUPS = 3
DIL_HEADS = 4
DIL_HEAD_DIM = 128
DIL_SPAN = DIL_WINDOWS[0] // DIL_RATES[0]
DIL_GW = DIL_HEADS * DIL_HEAD_DIM
DIL_WIDTH = DIL_GROUPS * DIL_GW
ALIBI_MAX_BIAS = 8.0
SB_HEADS = 8
SB_HEAD_DIM = 128
SB_WIDTH = SB_HEADS * SB_HEAD_DIM
N_EXPERTS = 16
N_GROUPS = 4
EXPERTS_PER_GROUP = N_EXPERTS // N_GROUPS
D_EXPERT = 512

LOG2E = math.log2(math.e)
LN2 = math.log(2.0)
NEG_BIG = -1e30

V7X_LANES = 128
V7X_VMEM_BYTES = 64 * 1024 * 1024
VMEM_LIMIT = 56 * 1024 * 1024

F32 = jnp.float32
BF16 = jnp.bfloat16


def _cp(sem, vmem=VMEM_LIMIT):
    return pltpu.CompilerParams(dimension_semantics=sem, vmem_limit_bytes=vmem)


def _nt_dot(a, b):
    return lax.dot_general(a, b, (((1,), (1,)), ((), ())), preferred_element_type=F32)


def _ada_kernel(c_ref, w_ref, b_ref, o_ref):
    c = c_ref[...]
    ca = c * (1.0 / (1.0 + jnp.exp(-c)))
    o_ref[0] = jnp.dot(ca, w_ref[0], preferred_element_type=F32,
                       precision=lax.Precision.HIGHEST) + b_ref[0]


def ada_mod(c, w_ada, b_ada, tn=1024):
    depth, d, n = w_ada.shape
    b = c.shape[0]
    return pl.pallas_call(
        _ada_kernel,
        grid=(depth, n // tn),
        in_specs=[pl.BlockSpec((b, d), lambda l, j: (0, 0)),
                  pl.BlockSpec((1, d, tn), lambda l, j: (l, 0, j)),
                  pl.BlockSpec((1, 1, tn), lambda l, j: (l, 0, j))],
        out_specs=pl.BlockSpec((1, b, tn), lambda l, j: (l, 0, j)),
        out_shape=jax.ShapeDtypeStruct((depth, b, n), F32),
        compiler_params=_cp(("arbitrary", "arbitrary")),
        name="ada_mod",
    )(c, w_ada, b_ada.reshape(depth, 1, n))


def _rope_table_kernel(pos_ref, inv_ref, cos_ref, sin_ref):
    ang = pos_ref[0] * inv_ref[...]
    cos_ref[0] = jnp.cos(ang)
    sin_ref[0] = jnp.sin(ang)


def rope_tables(pos_col, tm=512):
    b, s, _ = pos_col.shape
    tm = min(tm, s)
    half = ROPE_DIM // 2
    inv = ROPE_THETA ** (-jnp.arange(half, dtype=F32) / half)
    inv2 = jnp.concatenate([inv, inv]).reshape(1, ROPE_DIM)
    shp = jax.ShapeDtypeStruct((b, s, ROPE_DIM), F32)
    return pl.pallas_call(
        _rope_table_kernel,
        grid=(b, s // tm),
        in_specs=[pl.BlockSpec((1, tm, 1), lambda i, j: (i, j, 0)),
                  pl.BlockSpec((1, ROPE_DIM), lambda i, j: (0, 0))],
        out_specs=[pl.BlockSpec((1, tm, ROPE_DIM), lambda i, j: (i, j, 0))] * 2,
        out_shape=[shp, shp],
        compiler_params=_cp(("arbitrary", "arbitrary")),
        name="rope_tables",
    )(pos_col, inv2)


def _norm_mod(x, g, scale, shift):
    y = x * lax.rsqrt(jnp.mean(x * x, axis=-1, keepdims=True) + EPS)
    return (y * g) * (1.0 + scale) + shift


def _norm_kernel(x_ref, g_ref, sc_ref, sh_ref, o_ref):
    o_ref[0] = _norm_mod(x_ref[0], g_ref[...], sc_ref[0], sh_ref[0]).astype(o_ref.dtype)


def _norm_router_kernel(x_ref, g_ref, sc_ref, sh_ref, wr_ref, o_ref, lg_ref):
    a = _norm_mod(x_ref[0], g_ref[...], sc_ref[0], sh_ref[0])
    o_ref[0] = a.astype(o_ref.dtype)
    lg_ref[0] = lax.dot_general(wr_ref[...], a, (((1,), (1,)), ((), ())),
                                preferred_element_type=F32, precision=lax.Precision.HIGHEST)


def norm_mod(x, g, mod3, sc_idx, sh_idx, w_router_t=None, tm=512):
    b, s, d = x.shape
    tm = min(tm, s)
    in_specs = [pl.BlockSpec((1, tm, d), lambda i, j: (i, j, 0)),
                pl.BlockSpec((1, d), lambda i, j: (0, 0)),
                pl.BlockSpec((1, 1, d), lambda i, j: (i, 0, sc_idx)),
                pl.BlockSpec((1, 1, d), lambda i, j: (i, 0, sh_idx))]
    o_spec = pl.BlockSpec((1, tm, d), lambda i, j: (i, j, 0))
    o_shape = jax.ShapeDtypeStruct((b, s, d), BF16)
    if w_router_t is None:
        return pl.pallas_call(
            _norm_kernel, grid=(b, s // tm), in_specs=in_specs, out_specs=o_spec, out_shape=o_shape,
            compiler_params=_cp(("arbitrary", "arbitrary")), name="norm_mod",
        )(x, g.reshape(1, d), mod3, mod3)
    e = w_router_t.shape[0]
    return pl.pallas_call(
        _norm_router_kernel, grid=(b, s // tm),
        in_specs=in_specs + [pl.BlockSpec((e, d), lambda i, j: (0, 0))],
        out_specs=[o_spec, pl.BlockSpec((1, e, tm), lambda i, j: (i, 0, j))],
        out_shape=[o_shape, jax.ShapeDtypeStruct((b, e, s), F32)],
        compiler_params=_cp(("arbitrary", "arbitrary")), name="norm_router",
    )(x, g.reshape(1, d), mod3, mod3, w_router_t)


def _matmul_kernel(a_ref, w_ref, o_ref):
    o_ref[0] = jnp.dot(a_ref[0], w_ref[...], preferred_element_type=F32).astype(o_ref.dtype)


def project(a, w, tm=1024, tn=512, out_dtype=BF16):
    b, s, k = a.shape
    n = w.shape[1]
    tm = min(tm, s)
    tn = min(tn, n)
    assert n % tn == 0 and s % tm == 0
    return pl.pallas_call(
        _matmul_kernel,
        grid=(b, s // tm, n // tn),
        in_specs=[pl.BlockSpec((1, tm, k), lambda i, j, c: (i, j, 0)),
                  pl.BlockSpec((k, tn), lambda i, j, c: (0, c))],
        out_specs=pl.BlockSpec((1, tm, tn), lambda i, j, c: (i, j, c)),
        out_shape=jax.ShapeDtypeStruct((b, s, n), out_dtype),
        compiler_params=_cp(("arbitrary", "arbitrary", "arbitrary")),
        name="project",
    )(a, w)


def _latent_norm(c, g):
    c = c.astype(F32)
    return (c * lax.rsqrt(jnp.mean(c * c, axis=-1, keepdims=True) + EPS) * g).astype(BF16)


def _mla_q_kernel(cq_ref, g_ref, w_ref, cos_ref, sin_ref, o_ref, rn_ref):
    @pl.when(pl.program_id(2) == 0)
    def _():
        rn_ref[...] = _latent_norm(cq_ref[0], g_ref[...])

    r = jnp.dot(rn_ref[...], w_ref[0], preferred_element_type=F32)
    o_ref[0, 0, :, :NOPE_DIM] = r[:, :NOPE_DIM].astype(o_ref.dtype)
    roped = (r[:, NOPE_DIM:NOPE_DIM + ROPE_DIM] * cos_ref[0]
             + r[:, NOPE_DIM + ROPE_DIM:] * sin_ref[0])
    o_ref[0, 0, :, NOPE_DIM:] = roped.astype(o_ref.dtype)


def _mla_kv_kernel(ckv_ref, kr_ref, g_ref, w_ref, cos_ref, sin_ref, k_ref, v_ref, rn_ref, kr_scr):
    @pl.when(pl.program_id(2) == 0)
    def _():
        rn_ref[...] = _latent_norm(ckv_ref[0], g_ref[...])
        kr = kr_ref[0].astype(F32)
        kr_scr[...] = (kr[:, :ROPE_DIM] * cos_ref[0] + kr[:, ROPE_DIM:] * sin_ref[0]).astype(kr_scr.dtype)

    r = jnp.dot(rn_ref[...], w_ref[0], preferred_element_type=F32)
    k_ref[0, 0, :, :NOPE_DIM] = r[:, :NOPE_DIM].astype(k_ref.dtype)
    k_ref[0, 0, :, NOPE_DIM:] = kr_scr[...]
    v_ref[0, 0] = r[:, NOPE_DIM:].astype(v_ref.dtype)


def mla_project(lat, g_q, wq_h, g_kv, wkv_h, cos, sin, tm=512):
    b, s, _ = lat.shape
    tm = min(tm, s)
    h = MLA_HEADS
    grid = (b, s // tm, h)
    sem = _cp(("arbitrary", "arbitrary", "arbitrary"))
    tab = pl.BlockSpec((1, tm, ROPE_DIM), lambda i, j, k: (i, j, 0))
    q = pl.pallas_call(
        _mla_q_kernel, grid=grid,
        in_specs=[pl.BlockSpec((1, tm, Q_LORA), lambda i, j, k: (i, j, 0)),
                  pl.BlockSpec((1, Q_LORA), lambda i, j, k: (0, 0)),
                  pl.BlockSpec((1, Q_LORA, 256), lambda i, j, k: (k, 0, 0)),
                  tab, tab],
        out_specs=pl.BlockSpec((1, 1, tm, MLA_QK), lambda i, j, k: (i, k, j, 0)),
        out_shape=jax.ShapeDtypeStruct((b, h, s, MLA_QK), BF16),
        scratch_shapes=[pltpu.VMEM((tm, Q_LORA), BF16)],
        compiler_params=sem, name="mla_q",
    )(lat, g_q.reshape(1, Q_LORA), wq_h, cos, sin)
    kr_blk = (Q_LORA + KV_LORA) // V7X_LANES
    k, v = pl.pallas_call(
        _mla_kv_kernel, grid=grid,
        in_specs=[pl.BlockSpec((1, tm, KV_LORA), lambda i, j, k: (i, j, 1)),
                  pl.BlockSpec((1, tm, 2 * ROPE_DIM), lambda i, j, k: (i, j, kr_blk)),
                  pl.BlockSpec((1, KV_LORA), lambda i, j, k: (0, 0)),
                  pl.BlockSpec((1, KV_LORA, 256), lambda i, j, k: (k, 0, 0)),
                  tab, tab],
        out_specs=[pl.BlockSpec((1, 1, tm, MLA_QK), lambda i, j, k: (i, k, j, 0)),
                   pl.BlockSpec((1, 1, tm, MLA_V_DIM), lambda i, j, k: (i, k, j, 0))],
        out_shape=[jax.ShapeDtypeStruct((b, h, s, MLA_QK), BF16),
                   jax.ShapeDtypeStruct((b, h, s, MLA_V_DIM), BF16)],
        scratch_shapes=[pltpu.VMEM((tm, KV_LORA), BF16), pltpu.VMEM((tm, ROPE_DIM), BF16)],
        compiler_params=sem, name="mla_kv",
    )(lat, lat, g_kv.reshape(1, KV_LORA), wkv_h, cos, sin)
    return q, k, v


def _mla_attn_kernel(q_ref, k_ref, v_ref, o_ref, m_scr, l_scr, acc_scr, *, tq):
    qi = pl.program_id(2)
    q = q_ref[0, 0]
    m_scr[...] = jnp.full(m_scr.shape, NEG_BIG, F32)
    l_scr[...] = jnp.zeros(l_scr.shape, F32)
    acc_scr[...] = jnp.zeros(acc_scr.shape, F32)

    def step(j, masked):
        start = pl.multiple_of(j * tq, tq)
        k = k_ref[0, 0, pl.ds(start, tq), :]
        v = v_ref[0, 0, pl.ds(start, tq), :]
        s = _nt_dot(q, k)
        if masked:
            row = lax.broadcasted_iota(jnp.int32, (tq, tq), 0)
            col = lax.broadcasted_iota(jnp.int32, (tq, tq), 1)
            s = jnp.where(col <= row, s, NEG_BIG)
        m_prev = m_scr[...]
        m_new = jnp.maximum(m_prev, jnp.max(s, axis=-1, keepdims=True))
        p = jnp.exp2(s - m_new)
        alpha = jnp.exp2(m_prev - m_new)
        l_scr[...] = alpha * l_scr[...] + jnp.sum(p, axis=-1, keepdims=True)
        acc_scr[...] = alpha * acc_scr[...] + jnp.dot(p.astype(BF16), v, preferred_element_type=F32)
        m_scr[...] = m_new

    def body(j, carry):
        step(j, False)
        return carry

    lax.fori_loop(0, qi, body, 0)
    step(qi, True)
    o_ref[0] = (acc_scr[...] / l_scr[...]).astype(o_ref.dtype)


def mla_attention(q, k, v, tq=512):
    b, h, s, _ = q.shape
    tq = min(tq, s)
    return pl.pallas_call(
        functools.partial(_mla_attn_kernel, tq=tq),
        grid=(b, h, s // tq),
        in_specs=[pl.BlockSpec((1, 1, tq, MLA_QK), lambda i, j, t: (i, j, t, 0)),
                  pl.BlockSpec((1, 1, s, MLA_QK), lambda i, j, t: (i, j, 0, 0)),
                  pl.BlockSpec((1, 1, s, MLA_V_DIM), lambda i, j, t: (i, j, 0, 0))],
        out_specs=pl.BlockSpec((1, tq, MLA_V_DIM), lambda i, j, t: (i, t, j)),
        out_shape=jax.ShapeDtypeStruct((b, s, h * MLA_V_DIM), BF16),
        scratch_shapes=[pltpu.VMEM((tq, 1), F32), pltpu.VMEM((tq, 1), F32),
                        pltpu.VMEM((tq, MLA_V_DIM), F32)],
        compiler_params=_cp(("arbitrary", "arbitrary", "arbitrary")),
        name="mla_attention",
    )(q, k, v)


def _sb_attn_kernel(q_ref, k_ref, v_ref, tri_ref, o_ref, c_scr, acc_scr, *, tq, tk):
    qi = pl.program_id(2)
    q = q_ref[0]
    tri = tri_ref[...]
    c_scr[...] = jnp.zeros(c_scr.shape, F32)
    acc_scr[...] = jnp.zeros(acc_scr.shape, F32)
    n_diag = tq // tk

    def step(j, masked):
        start = pl.multiple_of(j * tk, tk)
        k = k_ref[0, pl.ds(start, tk), :]
        v = v_ref[0, pl.ds(start, tk), :]
        z = _nt_dot(q, k)
        lk = -(jnp.maximum(z, 0.0) + jnp.log(1.0 + jnp.exp(-jnp.abs(z))))
        if masked:
            row = qi * tq + lax.broadcasted_iota(jnp.int32, (tq, tk), 0)
            col = j * tk + lax.broadcasted_iota(jnp.int32, (tq, tk), 1)
            before = col < row
            lk = jnp.where(before, lk, 0.0)
        hi = lk.astype(BF16)
        lo = (lk - hi.astype(F32)).astype(BF16)
        suffix = (jnp.dot(hi, tri, preferred_element_type=F32)
                  + jnp.dot(lo, tri, preferred_element_type=F32))
        log_a = z + suffix + c_scr[...]
        a = jnp.exp(log_a)
        if masked:
            a = jnp.where(before, a, 0.0)
        acc_scr[...] += jnp.dot(a.astype(BF16), v, preferred_element_type=F32)
        c_scr[...] += jnp.sum(lk, axis=-1, keepdims=True)

    for d in range(n_diag):
        step(qi * n_diag + (n_diag - 1 - d), True)

    def body(t, carry):
        step(qi * n_diag - 1 - t, False)
        return carry

    lax.fori_loop(0, qi * n_diag, body, 0)
    o_ref[0] = acc_scr[...].astype(o_ref.dtype)


def sb_attention(qkv, tq=256, tk=256):
    b, s, _ = qkv.shape
    tq = min(tq, s)
    tk = min(tk, tq)
    h = SB_HEADS
    tri = (lax.broadcasted_iota(jnp.int32, (tk, tk), 0)
           >= lax.broadcasted_iota(jnp.int32, (tk, tk), 1)).astype(BF16)
    return pl.pallas_call(
        functools.partial(_sb_attn_kernel, tq=tq, tk=tk),
        grid=(b, h, s // tq),
        in_specs=[pl.BlockSpec((1, tq, SB_HEAD_DIM), lambda i, j, t: (i, t, j)),
                  pl.BlockSpec((1, s, SB_HEAD_DIM), lambda i, j, t: (i, 0, h + j)),
                  pl.BlockSpec((1, s, SB_HEAD_DIM), lambda i, j, t: (i, 0, 2 * h + j)),
                  pl.BlockSpec((tk, tk), lambda i, j, t: (0, 0))],
        out_specs=pl.BlockSpec((1, tq, SB_HEAD_DIM), lambda i, j, t: (i, t, j)),
        out_shape=jax.ShapeDtypeStruct((b, s, SB_WIDTH), BF16),
        scratch_shapes=[pltpu.VMEM((tq, 1), F32), pltpu.VMEM((tq, SB_HEAD_DIM), F32)],
        compiler_params=_cp(("arbitrary", "arbitrary", "arbitrary")),
        name="sb_attention",
    )(qkv, qkv, qkv, tri)


def _dil_kernel(q_ref, kc_ref, vc_ref, kp_ref, vp_ref, pq_ref, pkc_ref, pkp_ref, sl_ref,
                o_ref, lse_ref, *, tq):
    ti = pl.program_id(2)
    w = DIL_SPAN
    a_idx = lax.broadcasted_iota(jnp.int32, (w, w), 0)
    c_idx = lax.broadcasted_iota(jnp.int32, (w, w), 1)
    own_ok = c_idx <= a_idx
    prev_tri = c_idx >= a_idx
    for sb in range(tq // w):
        rows = slice(sb * w, (sb + 1) * w)
        pq = pq_ref[0, 0, rows, :]
        pk_own = pkc_ref[0, 0, :, rows]
        if sb == 0:
            pk_prev = pkp_ref[0, 0]
            prev_ok = jnp.logical_and(prev_tri, ti > 0)
        else:
            pk_prev = pkc_ref[0, 0, :, (sb - 1) * w: sb * w]
            prev_ok = prev_tri
        dist_own = jnp.abs(pq - pk_own)
        dist_prev = jnp.abs(pq - pk_prev)
        for h in range(DIL_HEADS):
            cols = slice(h * DIL_HEAD_DIM, (h + 1) * DIL_HEAD_DIM)
            q = q_ref[0, rows, cols]
            k_own = kc_ref[0, rows, cols]
            v_own = vc_ref[0, rows, cols]
            if sb == 0:
                k_prev = kp_ref[0, :, cols]
                v_prev = vp_ref[0, :, cols]
            else:
                k_prev = kc_ref[0, (sb - 1) * w: sb * w, cols]
                v_prev = vc_ref[0, (sb - 1) * w: sb * w, cols]
            slope = sl_ref[h]
            s_own = jnp.where(own_ok, _nt_dot(q, k_own) - slope * dist_own, NEG_BIG)
            s_prev = jnp.where(prev_ok, _nt_dot(q, k_prev) - slope * dist_prev, NEG_BIG)
            m = jnp.maximum(jnp.max(s_own, axis=-1, keepdims=True),
                            jnp.max(s_prev, axis=-1, keepdims=True))
            p_own = jnp.exp2(s_own - m)
            p_prev = jnp.exp2(s_prev - m)
            den = jnp.sum(p_own, axis=-1, keepdims=True) + jnp.sum(p_prev, axis=-1, keepdims=True)
            o = (jnp.dot(p_own.astype(BF16), v_own, preferred_element_type=F32)
                 + jnp.dot(p_prev.astype(BF16), v_prev, preferred_element_type=F32)) / den
            o_ref[0, rows, cols] = o
            lse = (m + jnp.log2(den)) * LN2
            lse_ref[0, rows, cols] = jnp.broadcast_to(lse, (w, DIL_HEAD_DIM))


def dil_group_attention(dil, pos_f, g, tq=512):
    b, s, c = dil.shape
    r = DIL_RATES[g]
    sr = s // r
    tq = min(tq, sr)
    w = DIL_SPAN
    assert sr % tq == 0 and tq % w == 0
    nsub = tq // w
    view = dil.reshape(b, sr, r * c)
    ncb = c // DIL_GW
    pos_r = pos_f.reshape(b, sr, r).transpose(0, 2, 1)
    pos_q = pos_r[..., None]
    pos_k = pos_r[:, :, None, :]
    n_h = DIL_GROUPS * DIL_HEADS
    slopes = 2.0 ** (-ALIBI_MAX_BIAS * jnp.arange(1, n_h + 1, dtype=F32) / n_h)
    slopes2 = (slopes * LOG2E)[g * DIL_HEADS:(g + 1) * DIL_HEADS]

    def cur(off):
        return pl.BlockSpec((1, tq, DIL_GW), lambda i, p, t: (i, t, p * ncb + off + g))

    def prev(off):
        return pl.BlockSpec((1, w, DIL_GW), lambda i, p, t: (i, jnp.maximum(t * nsub - 1, 0), p * ncb + off + g))

    out_spec = pl.BlockSpec((1, tq, DIL_GW), lambda i, p, t: (i, t, p))
    out_shape = jax.ShapeDtypeStruct((b, sr, r * DIL_GW), F32)
    o, lse = pl.pallas_call(
        functools.partial(_dil_kernel, tq=tq),
        grid=(b, r, sr // tq),
        in_specs=[cur(0), cur(DIL_GROUPS), cur(2 * DIL_GROUPS), prev(DIL_GROUPS), prev(2 * DIL_GROUPS),
                  pl.BlockSpec((1, 1, tq, 1), lambda i, p, t: (i, p, t, 0)),
                  pl.BlockSpec((1, 1, 1, tq), lambda i, p, t: (i, p, 0, t)),
                  pl.BlockSpec((1, 1, 1, w), lambda i, p, t: (i, p, 0, jnp.maximum(t * nsub - 1, 0))),
                  pl.BlockSpec(memory_space=pltpu.SMEM)],
        out_specs=[out_spec, out_spec],
        out_shape=[out_shape, out_shape],
        compiler_params=_cp(("arbitrary", "arbitrary", "arbitrary")),
        name=f"dil_attention_g{g}",
    )(view, view, view, view, view, pos_q, pos_k, pos_k, slopes2)
    return o.reshape(b, s, DIL_GW), lse.reshape(b, s, DIL_GW)


def _merge_kernel(ym_ref, o0_ref, o1_ref, o2_ref, l0_ref, l1_ref, l2_ref, ys_ref,
                  g0_ref, g1_ref, g2_ref, wm_ref, wd_ref, ws_ref, out_ref):
    l0, l1, l2 = l0_ref[0], l1_ref[0], l2_ref[0]
    mx = jnp.maximum(jnp.maximum(l0, l1), l2)
    e0, e1, e2 = jnp.exp(l0 - mx), jnp.exp(l1 - mx), jnp.exp(l2 - mx)
    y_dil = (e0 * o0_ref[0] + e1 * o1_ref[0] + e2 * o2_ref[0]) / (e0 + e1 + e2)

    def sig(ref):
        return 1.0 / (1.0 + jnp.exp(-ref[0].astype(F32)))

    merged = (sig(g0_ref) * jnp.dot(ym_ref[0], wm_ref[...], preferred_element_type=F32)
              + sig(g1_ref) * jnp.dot(y_dil.astype(BF16), wd_ref[...], preferred_element_type=F32)
              + sig(g2_ref) * jnp.dot(ys_ref[0], ws_ref[...], preferred_element_type=F32))
    out_ref[0] = merged.astype(out_ref.dtype)


def merge_branches(y_mla, dil_o, dil_lse, y_sb, gates, w_o_mla, w_o_dil, w_o_sb, tm=256):
    b, s, _ = y_mla.shape
    d = D_MODEL
    tm = min(tm, s)

    def row(width, cb=0):
        return pl.BlockSpec((1, tm, width), lambda i, j: (i, j, cb))

    def full(w):
        return pl.BlockSpec(w.shape, lambda i, j: (0, 0))

    return pl.pallas_call(
        _merge_kernel, grid=(b, s // tm),
        in_specs=[row(y_mla.shape[2])] + [row(DIL_GW)] * 6 + [row(SB_WIDTH),
                  row(d, 0), row(d, 1), row(d, 2), full(w_o_mla), full(w_o_dil), full(w_o_sb)],
        out_specs=row(d),
        out_shape=jax.ShapeDtypeStruct((b, s, d), BF16),
        compiler_params=_cp(("arbitrary", "arbitrary")),
        name="merge_branches",
    )(y_mla, *dil_o, *dil_lse, y_sb, gates, gates, gates, w_o_mla, w_o_dil, w_o_sb)


def _resid_proj_kernel(a_ref, w_ref, x_ref, g_ref, o_ref):
    y = jnp.dot(a_ref[0], w_ref[...], preferred_element_type=F32)
    o_ref[0] = x_ref[0] + g_ref[0] * y


def resid_project(a, w, x, mod3, gate_idx, tm=512, tn=1024):
    b, s, k = a.shape
    n = w.shape[1]
    tm = min(tm, s)
    return pl.pallas_call(
        _resid_proj_kernel, grid=(b, s // tm, n // tn),
        in_specs=[pl.BlockSpec((1, tm, k), lambda i, j, c: (i, j, 0)),
                  pl.BlockSpec((k, tn), lambda i, j, c: (0, c)),
                  pl.BlockSpec((1, tm, tn), lambda i, j, c: (i, j, c)),
                  pl.BlockSpec((1, 1, tn), lambda i, j, c: (i, 0, gate_idx * (n // tn) + c))],
        out_specs=pl.BlockSpec((1, tm, tn), lambda i, j, c: (i, j, c)),
        out_shape=jax.ShapeDtypeStruct((b, s, n), F32),
        compiler_params=_cp(("arbitrary", "arbitrary", "arbitrary")),
        name="resid_project",
    )(a, w, x, mod3)


def _router_kernel(lg_ref, b_ref, o_ref):
    lg = lg_ref[0]
    tm = lg.shape[1]
    sc = [1.0 / (1.0 + jnp.exp(-lg[e:e + 1, :])) for e in range(N_EXPERTS)]
    bi = [sc[e] + b_ref[e] for e in range(N_EXPERTS)]
    n = EXPERTS_PER_GROUP
    gscore = []
    for g in range(N_GROUPS):
        v = bi[g * n:(g + 1) * n]
        pair_max = None
        for a in range(n):
            for c in range(a + 1, n):
                pm = v[a] + v[c]
                pair_max = pm if pair_max is None else jnp.maximum(pair_max, pm)
        gscore.append(pair_max)
    best = gscore[0]
    gsel = jnp.zeros((1, tm), jnp.int32)
    for g in range(1, N_GROUPS):
        better = gscore[g] > best
        best = jnp.where(better, gscore[g], best)
        gsel = jnp.where(better, g, gsel)
    gb = [sum(jnp.where(gsel == g, bi[g * n + i], 0.0) for g in range(N_GROUPS)) for i in range(n)]
    gs = [sum(jnp.where(gsel == g, sc[g * n + i], 0.0) for g in range(N_GROUPS)) for i in range(n)]
    sel = []
    for i in range(n):
        beaten = jnp.zeros((1, tm), jnp.int32)
        for j in range(n):
            if j == i:
                continue
            wins = (gb[j] > gb[i]) if j > i else (gb[j] >= gb[i])
            beaten = beaten + wins.astype(jnp.int32)
        sel.append(beaten < 2)
    den = sum(jnp.where(sel[i], gs[i], 0.0) for i in range(n))
    for e in range(N_EXPERTS):
        g, i = divmod(e, n)
        o_ref[0, e:e + 1, :] = jnp.where(jnp.logical_and(gsel == g, sel[i]), gs[i] / den, 0.0)


def router_gates(logits_t, b_router, tm=1024):
    b, e, s = logits_t.shape
    tm = min(tm, s)
    return pl.pallas_call(
        _router_kernel, grid=(b, s // tm),
        in_specs=[pl.BlockSpec((1, e, tm), lambda i, j: (i, 0, j)),
                  pl.BlockSpec(memory_space=pltpu.SMEM)],
        out_specs=pl.BlockSpec((1, e, tm), lambda i, j: (i, 0, j)),
        out_shape=jax.ShapeDtypeStruct((b, e, s), F32),
        compiler_params=_cp(("arbitrary", "arbitrary")),
        name="router_gates",
    )(logits_t, b_router)


def _moe_kernel(a_ref, gt_ref, wg_ref, wu_ref, wd_ref, x_ref, gm_ref, o_ref, acc_ref):
    e = pl.program_id(2)

    @pl.when(e == 0)
    def _():
        acc_ref[...] = jnp.zeros(acc_ref.shape, F32)

    a = a_ref[0]
    hg = jnp.dot(a, wg_ref[0], preferred_element_type=F32)
    hu = jnp.dot(a, wu_ref[0], preferred_element_type=F32)
    onehot = lax.broadcasted_iota(jnp.int32, (1, N_EXPERTS), 1) == e
    gate = jnp.sum(jnp.where(onehot, gt_ref[0], 0.0), axis=-1, keepdims=True)
    h = (hg / (1.0 + jnp.exp(-hg))) * hu * gate
    acc_ref[...] += jnp.dot(h.astype(BF16), wd_ref[0], preferred_element_type=F32)

    @pl.when(e == N_EXPERTS - 1)
    def _():
        o_ref[0] = x_ref[0] + gm_ref[0] * acc_ref[...]


def moe_dense(a, gates, w_gate, w_up, w_down, x, mod3, gate_idx, tm=512):
    b, s, d = a.shape
    tm = min(tm, s)
    ne = w_gate.shape[0]
    return pl.pallas_call(
        _moe_kernel, grid=(b, s // tm, ne),
        in_specs=[pl.BlockSpec((1, tm, d), lambda i, j, e: (i, j, 0)),
                  pl.BlockSpec((1, tm, ne), lambda i, j, e: (i, j, 0)),
                  pl.BlockSpec((1, d, D_EXPERT), lambda i, j, e: (e, 0, 0)),
                  pl.BlockSpec((1, d, D_EXPERT), lambda i, j, e: (e, 0, 0)),
                  pl.BlockSpec((1, D_EXPERT, d), lambda i, j, e: (e, 0, 0)),
                  pl.BlockSpec((1, tm, d), lambda i, j, e: (i, j, 0)),
                  pl.BlockSpec((1, 1, d), lambda i, j, e: (i, 0, gate_idx))],
        out_specs=pl.BlockSpec((1, tm, d), lambda i, j, e: (i, j, 0)),
        out_shape=jax.ShapeDtypeStruct((b, s, d), F32),
        scratch_shapes=[pltpu.VMEM((tm, d), F32)],
        compiler_params=_cp(("arbitrary", "arbitrary", "arbitrary")),
        name="moe_dense",
    )(a, gates, w_gate, w_up, w_down, x, mod3)


def _final_norm_kernel(x_ref, g_ref, o_ref):
    x = x_ref[0]
    o_ref[0] = x * lax.rsqrt(jnp.mean(x * x, axis=-1, keepdims=True) + EPS) * g_ref[...]


def final_norm(x, g, tm=512):
    b, s, d = x.shape
    tm = min(tm, s)
    return pl.pallas_call(
        _final_norm_kernel, grid=(b, s // tm),
        in_specs=[pl.BlockSpec((1, tm, d), lambda i, j: (i, j, 0)),
                  pl.BlockSpec((1, d), lambda i, j: (0, 0))],
        out_specs=pl.BlockSpec((1, tm, d), lambda i, j: (i, j, 0)),
        out_shape=jax.ShapeDtypeStruct((b, s, d), F32),
        compiler_params=_cp(("arbitrary", "arbitrary")),
        name="final_norm",
    )(x, g.reshape(1, d))


def _rot_cols(w):
    half = w.shape[-1] // 2
    return jnp.concatenate([-w[..., half:], w[..., :half]], axis=-1)


def _prep_layer(w_in, w_uq, w_ukv, w_o_mla, w_o_dil, w_o_sb, w_out, w_gate, w_up, w_down):
    o1 = Q_LORA
    o2 = o1 + KV_LORA
    o3 = o2 + ROPE_DIM
    o4 = o3 + 3 * DIL_WIDTH
    o5 = o4 + 3 * SB_WIDTH
    w_kr = w_in[:, o2:o3]
    w_lat = jnp.concatenate([w_in[:, :o2], w_kr, _rot_cols(w_kr)], axis=1).astype(BF16)
    dil_scale = DIL_HEAD_DIM ** -0.5 * LOG2E
    w_dil = jnp.concatenate([w_in[:, o3:o3 + DIL_WIDTH] * dil_scale, w_in[:, o3 + DIL_WIDTH:o4]],
                            axis=1).astype(BF16)
    sb_scale = SB_HEAD_DIM ** -0.5
    w_sb = jnp.concatenate([w_in[:, o4:o4 + SB_WIDTH] * sb_scale, w_in[:, o4 + SB_WIDTH:o5]],
                           axis=1).astype(BF16)
    w_gl = w_in[:, o5:].astype(BF16)
    q_scale = MLA_QK ** -0.5 * LOG2E
    wq = (w_uq * q_scale).reshape(Q_LORA, MLA_HEADS, MLA_QK).transpose(1, 0, 2)
    wq_h = jnp.concatenate([wq, _rot_cols(wq[..., NOPE_DIM:])], axis=-1).astype(BF16)
    wkv_h = w_ukv.reshape(KV_LORA, MLA_HEADS, NOPE_DIM + MLA_V_DIM).transpose(1, 0, 2).astype(BF16)
    return dict(w_lat=w_lat, w_dil=w_dil, w_sb=w_sb, w_gl=w_gl, wq_h=wq_h, wkv_h=wkv_h,
                w_o_mla=w_o_mla.astype(BF16), w_o_dil=w_o_dil.astype(BF16), w_o_sb=w_o_sb.astype(BF16),
                w_out=w_out.astype(BF16), w_gate=w_gate.astype(BF16), w_up=w_up.astype(BF16),
                w_down=w_down.astype(BF16))


def kernel(x, c, positions, w_ada, b_ada, g_mix, g_moe, w_in, g_q, w_uq, g_kv, w_ukv, w_o_mla, w_o_dil,
           w_o_sb, w_out, w_router, b_router, w_gate, w_up, w_down, g_final):
    b, s, d = x.shape
    depth = w_ada.shape[0]
    pos_f = positions.astype(F32)
    cos, sin = rope_tables(pos_f.reshape(b, s, 1))
    mod = ada_mod(c, w_ada, b_ada)
    w_router_t = w_router.T
    for l in range(depth):
        p = _prep_layer(w_in[l], w_uq[l], w_ukv[l], w_o_mla[l], w_o_dil[l], w_o_sb[l], w_out[l],
                        w_gate[l], w_up[l], w_down[l])
        mod3 = mod[l].reshape(b, 1, 6 * d)

        a = norm_mod(x, g_mix[l], mod3, sc_idx=1, sh_idx=0)
        lat = project(a, p["w_lat"], tn=p["w_lat"].shape[1])
        dil = project(a, p["w_dil"])
        sbp = project(a, p["w_sb"])
        gl = project(a, p["w_gl"])

        q, k, v = mla_project(lat, g_q[l], p["wq_h"], g_kv[l], p["wkv_h"], cos, sin)
        y_mla = mla_attention(q, k, v)
        dil_out = [dil_group_attention(dil, pos_f, g) for g in range(DIL_GROUPS)]
        y_sb = sb_attention(sbp)
        merged = merge_branches(y_mla, [o for o, _ in dil_out], [e for _, e in dil_out], y_sb, gl,
                                p["w_o_mla"], p["w_o_dil"], p["w_o_sb"])
        x = resid_project(merged, p["w_out"], x, mod3, gate_idx=2)

        a2, logits_t = norm_mod(x, g_moe[l], mod3, sc_idx=4, sh_idx=3, w_router_t=w_router_t)
        gates_t = router_gates(logits_t, b_router)
        gates = gates_t.transpose(0, 2, 1)
        x = moe_dense(a2, gates, p["w_gate"], p["w_up"], p["w_down"], x, mod3, gate_idx=5)
    return final_norm(x, g_final)
```

```python
import functools
import math

import jax
import jax.numpy as jnp
from jax import lax
from jax.experimental import pallas as pl
from jax.experimental.pallas import tpu as pltpu

D_MODEL = 2048
EPS = 1e-6
MLA_HEADS = 8
Q_LORA = 512
KV_LORA = 512
NOPE_DIM = 128
ROPE_DIM = 64
MLA_V_DIM = 128
ROPE_THETA = 10000.0
MLA_QK = NOPE_DIM + ROPE_DIM
DIL_WINDOWS = (128, 512, 2048)
DIL_RATES = (1, 4, 16)
DIL_GROUPS = 3
DIL_HEADS = 4
DIL_HEAD_DIM = 128
DIL_SPAN = DIL_WINDOWS[0] // DIL_RATES[0]
DIL_GW = DIL_HEADS * DIL_HEAD_DIM
DIL_WIDTH = DIL_GROUPS * DIL_GW
ALIBI_MAX_BIAS = 8.0
SB_HEADS = 8
SB_HEAD_DIM = 128
SB_WIDTH = SB_HEADS * SB_HEAD_DIM
N_EXPERTS = 16
N_GROUPS = 4
EXPERTS_PER_GROUP = N_EXPERTS // N_GROUPS
D_EXPERT = 512

LOG2E = math.log2(math.e)
LN2 = math.log(2.0)
NEG_BIG = -1e30

V7X_LANES = 128
V7X_VMEM_BYTES = 64 * 1024 * 1024
VMEM_LIMIT = 56 * 1024 * 1024

F32 = jnp.float32
BF16 = jnp.bfloat16


def _cp(sem, vmem=VMEM_LIMIT):
    return pltpu.CompilerParams(dimension_semantics=sem, vmem_limit_bytes=vmem)


def _nt_dot(a, b):
    return lax.dot_general(a, b, (((1,), (1,)), ((), ())), preferred_element_type=F32)


def _ada_kernel(c_ref, w_ref, b_ref, o_ref):
    c = c_ref[...]
    ca = c * (1.0 / (1.0 + jnp.exp(-c)))
    o_ref[0] = jnp.dot(ca, w_ref[0], preferred_element_type=F32,
                       precision=lax.Precision.HIGHEST) + b_ref[0]


def ada_mod(c, w_ada, b_ada, tn=1024):
    depth, d, n = w_ada.shape
    b = c.shape[0]
    return pl.pallas_call(
        _ada_kernel,
        grid=(depth, n // tn),
        in_specs=[pl.BlockSpec((b, d), lambda l, j: (0, 0)),
                  pl.BlockSpec((1, d, tn), lambda l, j: (l, 0, j)),
                  pl.BlockSpec((1, 1, tn), lambda l, j: (l, 0, j))],
        out_specs=pl.BlockSpec((1, b, tn), lambda l, j: (l, 0, j)),
        out_shape=jax.ShapeDtypeStruct((depth, b, n), F32),
        compiler_params=_cp(("arbitrary", "arbitrary")),
        name="ada_mod",
    )(c, w_ada, b_ada.reshape(depth, 1, n))


def _rope_table_kernel(pos_ref, inv_ref, cos_ref, sin_ref):
    ang = pos_ref[0] * inv_ref[...]
    cos_ref[0] = jnp.cos(ang)
    sin_ref[0] = jnp.sin(ang)


def rope_tables(pos_col, tm=512):
    b, s, _ = pos_col.shape
    tm = min(tm, s)
    half = ROPE_DIM // 2
    inv = ROPE_THETA ** (-jnp.arange(half, dtype=F32) / half)
    inv2 = jnp.concatenate([inv, inv]).reshape(1, ROPE_DIM)
    shp = jax.ShapeDtypeStruct((b, s, ROPE_DIM), F32)
    return pl.pallas_call(
        _rope_table_kernel,
        grid=(b, s // tm),
        in_specs=[pl.BlockSpec((1, tm, 1), lambda i, j: (i, j, 0)),
                  pl.BlockSpec((1, ROPE_DIM), lambda i, j: (0, 0))],
        out_specs=[pl.BlockSpec((1, tm, ROPE_DIM), lambda i, j: (i, j, 0))] * 2,
        out_shape=[shp, shp],
        compiler_params=_cp(("arbitrary", "arbitrary")),
        name="rope_tables",
    )(pos_col, inv2)


def _norm_mod(x, g, scale, shift):
    y = x * lax.rsqrt(jnp.mean(x * x, axis=-1, keepdims=True) + EPS)
    return (y * g) * (1.0 + scale) + shift


def _norm_kernel(x_ref, g_ref, sc_ref, sh_ref, o_ref):
    o_ref[0] = _norm_mod(x_ref[0], g_ref[...], sc_ref[0], sh_ref[0]).astype(o_ref.dtype)


def _norm_router_kernel(x_ref, g_ref, sc_ref, sh_ref, wr_ref, o_ref, lg_ref):
    a = _norm_mod(x_ref[0], g_ref[...], sc_ref[0], sh_ref[0])
    o_ref[0] = a.astype(o_ref.dtype)
    lg_ref[0] = lax.dot_general(wr_ref[...], a, (((1,), (1,)), ((), ())),
                                preferred_element_type=F32, precision=lax.Precision.HIGHEST)


def norm_mod(x, g, mod3, sc_idx, sh_idx, w_router_t=None, tm=512):
    b, s, d = x.shape
    tm = min(tm, s)
    in_specs = [pl.BlockSpec((1, tm, d), lambda i, j: (i, j, 0)),
                pl.BlockSpec((1, d), lambda i, j: (0, 0)),
                pl.BlockSpec((1, 1, d), lambda i, j: (i, 0, sc_idx)),
                pl.BlockSpec((1, 1, d), lambda i, j: (i, 0, sh_idx))]
    o_spec = pl.BlockSpec((1, tm, d), lambda i, j: (i, j, 0))
    o_shape = jax.ShapeDtypeStruct((b, s, d), BF16)
    if w_router_t is None:
        return pl.pallas_call(
            _norm_kernel, grid=(b, s // tm), in_specs=in_specs, out_specs=o_spec, out_shape=o_shape,
            compiler_params=_cp(("arbitrary", "arbitrary")), name="norm_mod",
        )(x, g.reshape(1, d), mod3, mod3)
    e = w_router_t.shape[0]
    return pl.pallas_call(
        _norm_router_kernel, grid=(b, s // tm),
        in_specs=in_specs + [pl.BlockSpec((e, d), lambda i, j: (0, 0))],
        out_specs=[o_spec, pl.BlockSpec((1, e, tm), lambda i, j: (i, 0, j))],
        out_shape=[o_shape, jax.ShapeDtypeStruct((b, e, s), F32)],
        compiler_params=_cp(("arbitrary", "arbitrary")), name="norm_router",
    )(x, g.reshape(1, d), mod3, mod3, w_router_t)


def _matmul_kernel(a_ref, w_ref, o_ref):
    o_ref[0] = jnp.dot(a_ref[0], w_ref[...], preferred_element_type=F32).astype(o_ref.dtype)


def project(a, w, tm=1024, tn=512, out_dtype=BF16):
    b, s, k = a.shape
    n = w.shape[1]
    tm = min(tm, s)
    tn = min(tn, n)
    assert n % tn == 0 and s % tm == 0
    return pl.pallas_call(
        _matmul_kernel,
        grid=(b, s // tm, n // tn),
        in_specs=[pl.BlockSpec((1, tm, k), lambda i, j, c: (i, j, 0)),
                  pl.BlockSpec((k, tn), lambda i, j, c: (0, c))],
        out_specs=pl.BlockSpec((1, tm, tn), lambda i, j, c: (i, j, c)),
        out_shape=jax.ShapeDtypeStruct((b, s, n), out_dtype),
        compiler_params=_cp(("arbitrary", "arbitrary", "arbitrary")),
        name="project",
    )(a, w)


def _latent_norm(c, g):
    c = c.astype(F32)
    return (c * lax.rsqrt(jnp.mean(c * c, axis=-1, keepdims=True) + EPS) * g).astype(BF16)


def _mla_q_kernel(cq_ref, g_ref, w_ref, cos_ref, sin_ref, o_ref, rn_ref):
    @pl.when(pl.program_id(2) == 0)
    def _():
        rn_ref[...] = _latent_norm(cq_ref[0], g_ref[...])

    r = jnp.dot(rn_ref[...], w_ref[0], preferred_element_type=F32)
    o_ref[0, 0, :, :NOPE_DIM] = r[:, :NOPE_DIM].astype(o_ref.dtype)
    roped = (r[:, NOPE_DIM:NOPE_DIM + ROPE_DIM] * cos_ref[0]
             + r[:, NOPE_DIM + ROPE_DIM:] * sin_ref[0])
    o_ref[0, 0, :, NOPE_DIM:] = roped.astype(o_ref.dtype)


def _mla_kv_kernel(ckv_ref, kr_ref, g_ref, w_ref, cos_ref, sin_ref, k_ref, v_ref, rn_ref, kr_scr):
    @pl.when(pl.program_id(2) == 0)
    def _():
        rn_ref[...] = _latent_norm(ckv_ref[0], g_ref[...])
        kr = kr_ref[0].astype(F32)
        kr_scr[...] = (kr[:, :ROPE_DIM] * cos_ref[0] + kr[:, ROPE_DIM:] * sin_ref[0]).astype(kr_scr.dtype)

    r = jnp.dot(rn_ref[...], w_ref[0], preferred_element_type=F32)
    k_ref[0, 0, :, :NOPE_DIM] = r[:, :NOPE_DIM].astype(k_ref.dtype)
    k_ref[0, 0, :, NOPE_DIM:] = kr_scr[...]
    v_ref[0, 0, :, :MLA_V_DIM] = r[:, NOPE_DIM:].astype(v_ref.dtype)
    v_ref[0, 0, :, MLA_V_DIM:] = jnp.ones((r.shape[0], MLA_V_DIM), v_ref.dtype)


def mla_project(lat, g_q, wq_h, g_kv, wkv_h, cos, sin, tm=512):
    b, s, _ = lat.shape
    tm = min(tm, s)
    h = MLA_HEADS
    grid = (b, s // tm, h)
    sem = _cp(("arbitrary", "arbitrary", "arbitrary"))
    tab = pl.BlockSpec((1, tm, ROPE_DIM), lambda i, j, k: (i, j, 0))
    q = pl.pallas_call(
        _mla_q_kernel, grid=grid,
        in_specs=[pl.BlockSpec((1, tm, Q_LORA), lambda i, j, k: (i, j, 0)),
                  pl.BlockSpec((1, Q_LORA), lambda i, j, k: (0, 0)),
                  pl.BlockSpec((1, Q_LORA, 256), lambda i, j, k: (k, 0, 0)),
                  tab, tab],
        out_specs=pl.BlockSpec((1, 1, tm, MLA_QK), lambda i, j, k: (i, k, j, 0)),
        out_shape=jax.ShapeDtypeStruct((b, h, s, MLA_QK), BF16),
        scratch_shapes=[pltpu.VMEM((tm, Q_LORA), BF16)],
        compiler_params=sem, name="mla_q",
    )(lat, g_q.reshape(1, Q_LORA), wq_h, cos, sin)
    kr_blk = (Q_LORA + KV_LORA) // V7X_LANES
    k, v = pl.pallas_call(
        _mla_kv_kernel, grid=grid,
        in_specs=[pl.BlockSpec((1, tm, KV_LORA), lambda i, j, k: (i, j, 1)),
                  pl.BlockSpec((1, tm, 2 * ROPE_DIM), lambda i, j, k: (i, j, kr_blk)),
                  pl.BlockSpec((1, KV_LORA), lambda i, j, k: (0, 0)),
                  pl.BlockSpec((1, KV_LORA, 256), lambda i, j, k: (k, 0, 0)),
                  tab, tab],
        out_specs=[pl.BlockSpec((1, 1, tm, MLA_QK), lambda i, j, k: (i, k, j, 0)),
                   pl.BlockSpec((1, 1, tm, 2 * MLA_V_DIM), lambda i, j, k: (i, k, j, 0))],
        out_shape=[jax.ShapeDtypeStruct((b, h, s, MLA_QK), BF16),
                   jax.ShapeDtypeStruct((b, h, s, 2 * MLA_V_DIM), BF16)],
        scratch_shapes=[pltpu.VMEM((tm, KV_LORA), BF16), pltpu.VMEM((tm, ROPE_DIM), BF16)],
        compiler_params=sem, name="mla_kv",
    )(lat, lat, g_kv.reshape(1, KV_LORA), wkv_h, cos, sin)
    return q, k, v


def _mla_attn_kernel(q_ref, k_ref, v_ref, o_ref, s0_scr, s1_scr, m_scr, acc_scr, *, tq, tk):
    qi = pl.program_id(2)
    m_scr[...] = jnp.full(m_scr.shape, NEG_BIG, F32)
    acc_scr[...] = jnp.zeros(acc_scr.shape, F32)

    def scores(c, dst, row0=0):
        start = pl.multiple_of(c * tk, tk)
        dst[row0:, :] = _nt_dot(q_ref[0, 0, row0:, :], k_ref[0, 0, pl.ds(start, tk), :])

    def absorb(src, c, col_off=None, row0=0):
        start = pl.multiple_of(c * tk, tk)
        s = src[row0:, :]
        n = tq - row0
        if col_off is not None:
            row = row0 + lax.broadcasted_iota(jnp.int32, (n, tk), 0)
            col = col_off + lax.broadcasted_iota(jnp.int32, (n, tk), 1)
            s = jnp.where(col <= row, s, NEG_BIG)
        m_prev = m_scr[row0:, :]
        m_new = jnp.maximum(m_prev, jnp.max(s, axis=-1, keepdims=True))
        p = jnp.exp2(s - pltpu.repeat(m_new, tk // V7X_LANES, axis=1))
        alpha = jnp.exp2(m_prev - m_new)
        pv = jnp.dot(p.astype(BF16), v_ref[0, 0, pl.ds(start, tk), :], preferred_element_type=F32)
        acc_scr[row0:, :] = pltpu.repeat(alpha, 2, axis=1) * acc_scr[row0:, :] + pv
        m_scr[row0:, :] = m_new

    scores(0, s0_scr)

    def body(i, carry):
        scores(2 * i + 1, s1_scr)
        absorb(s0_scr, 2 * i)
        scores(2 * i + 2, s0_scr)
        absorb(s1_scr, 2 * i + 1)
        return carry

    lax.fori_loop(0, qi, body, 0)
    scores(2 * qi + 1, s1_scr, row0=tk)
    absorb(s0_scr, 2 * qi, col_off=0)
    absorb(s1_scr, 2 * qi + 1, col_off=tk, row0=tk)
    acc = acc_scr[...]
    o_ref[0] = (acc[:, :MLA_V_DIM] / acc[:, MLA_V_DIM:]).astype(o_ref.dtype)


def mla_attention(q, k, v, tq=1024):
    b, h, s, _ = q.shape
    tq = min(tq, s)
    tk = tq // 2
    return pl.pallas_call(
        functools.partial(_mla_attn_kernel, tq=tq, tk=tk),
        grid=(b, h, s // tq),
        in_specs=[pl.BlockSpec((1, 1, tq, MLA_QK), lambda i, j, t: (i, j, t, 0)),
                  pl.BlockSpec((1, 1, s, MLA_QK), lambda i, j, t: (i, j, 0, 0)),
                  pl.BlockSpec((1, 1, s, 2 * MLA_V_DIM), lambda i, j, t: (i, j, 0, 0))],
        out_specs=pl.BlockSpec((1, tq, MLA_V_DIM), lambda i, j, t: (i, t, j)),
        out_shape=jax.ShapeDtypeStruct((b, s, h * MLA_V_DIM), BF16),
        scratch_shapes=[pltpu.VMEM((tq, tk), F32), pltpu.VMEM((tq, tk), F32),
                        pltpu.VMEM((tq, V7X_LANES), F32), pltpu.VMEM((tq, 2 * MLA_V_DIM), F32)],
        compiler_params=_cp(("arbitrary", "arbitrary", "arbitrary")),
        name="mla_attention",
    )(q, k, v)


SB_SKIP_LOG2 = 150.0


def _sb_attn_kernel(q_ref, k_ref, v_ref, tri_ref, o_ref, c_scr, acc_scr, *, tq, hp):
    qi = pl.program_id(2)
    c_scr[...] = jnp.zeros(c_scr.shape, F32)
    acc_scr[...] = jnp.zeros(acc_scr.shape, F32)
    reps = tq // V7X_LANES

    def chunk(j, diagonal):
        start = pl.multiple_of(j * tq, tq)
        heads = range(hp)
        cols = [slice(hh * SB_HEAD_DIM, (hh + 1) * SB_HEAD_DIM) for hh in heads]
        if diagonal:
            before = (lax.broadcasted_iota(jnp.int32, (tq, tq), 1)
                      < lax.broadcasted_iota(jnp.int32, (tq, tq), 0))
        zn = [_nt_dot(q_ref[0, :, cols[hh]], k_ref[0, pl.ds(start, tq), cols[hh]]) for hh in heads]
        lk = [jnp.minimum(z, 0.0) - jnp.log2(1.0 + jnp.exp2(-jnp.abs(z))) for z in zn]
        if diagonal:
            lk = [jnp.where(before, x, 0.0) for x in lk]
        hi = [x.astype(BF16) for x in lk]
        lo = [(x - h.astype(F32)).astype(BF16) for x, h in zip(lk, hi)]
        suffix = [jnp.dot(jnp.concatenate([h, l], axis=1), tri_ref[...], preferred_element_type=F32)
                  for h, l in zip(hi, lo)]
        c = [c_scr[hh] for hh in heads]
        a = [jnp.exp2(sf - z + pltpu.repeat(cc, reps, axis=1)) for sf, z, cc in zip(suffix, zn, c)]
        if diagonal:
            a = [jnp.where(before, x, 0.0) for x in a]
        for hh in heads:
            acc_scr[:, cols[hh]] += jnp.dot(a[hh].astype(BF16), v_ref[0, pl.ds(start, tq), cols[hh]],
                                            preferred_element_type=F32)
            c_scr[hh] = c[hh] + jnp.sum(lk[hh], axis=-1, keepdims=True)

    chunk(qi, True)

    def cond(state):
        j, go = state
        return jnp.logical_and(j >= 0, go > 0)

    def body(state):
        j, _ = state
        chunk(j, False)
        go = (jnp.max(c_scr[...]) > -SB_SKIP_LOG2).astype(jnp.int32)
        return j - 1, go

    lax.while_loop(cond, body, (qi - 1, jnp.int32(1)))
    o_ref[0] = acc_scr[...].astype(o_ref.dtype)


def sb_attention(qkv, tq=256, hp=4):
    b, s, _ = qkv.shape
    tq = min(tq, s)
    hw = hp * SB_HEAD_DIM
    nhb = SB_HEADS // hp
    tri = (lax.broadcasted_iota(jnp.int32, (tq, tq), 0)
           >= lax.broadcasted_iota(jnp.int32, (tq, tq), 1)).astype(BF16)
    tri2 = jnp.concatenate([tri, tri], axis=0)
    return pl.pallas_call(
        functools.partial(_sb_attn_kernel, tq=tq, hp=hp),
        grid=(b, nhb, s // tq),
        in_specs=[pl.BlockSpec((1, tq, hw), lambda i, j, t: (i, t, j)),
                  pl.BlockSpec((1, s, hw), lambda i, j, t: (i, 0, nhb + j)),
                  pl.BlockSpec((1, s, hw), lambda i, j, t: (i, 0, 2 * nhb + j)),
                  pl.BlockSpec((2 * tq, tq), lambda i, j, t: (0, 0))],
        out_specs=pl.BlockSpec((1, tq, hw), lambda i, j, t: (i, t, j)),
        out_shape=jax.ShapeDtypeStruct((b, s, SB_WIDTH), BF16),
        scratch_shapes=[pltpu.VMEM((hp, tq, V7X_LANES), F32), pltpu.VMEM((tq, hw), F32)],
        compiler_params=_cp(("arbitrary", "arbitrary", "arbitrary")),
        name="sb_attention",
    )(qkv, qkv, qkv, tri2)


def _dil_kernel(q_ref, kc_ref, vc_ref, kp_ref, vp_ref, pq_ref, pkc_ref, pkp_ref, sl_ref,
                o_ref, lse_ref, *, tq):
    ti = pl.program_id(2)
    w = DIL_SPAN
    a_idx = lax.broadcasted_iota(jnp.int32, (w, w), 0)
    c_idx = lax.broadcasted_iota(jnp.int32, (w, w), 1)
    own_ok = c_idx <= a_idx
    prev_tri = c_idx >= a_idx
    for sb in range(tq // w):
        rows = slice(sb * w, (sb + 1) * w)
        pq = pq_ref[0, 0, rows, :]
        pk_own = pkc_ref[0, 0, :, rows]
        if sb == 0:
            pk_prev = pkp_ref[0, 0]
            prev_ok = jnp.logical_and(prev_tri, ti > 0)
        else:
            pk_prev = pkc_ref[0, 0, :, (sb - 1) * w: sb * w]
            prev_ok = prev_tri
        dist_own = jnp.abs(pq - pk_own)
        dist_prev = jnp.abs(pq - pk_prev)
        for h in range(DIL_HEADS):
            cols = slice(h * DIL_HEAD_DIM, (h + 1) * DIL_HEAD_DIM)
            q = q_ref[0, rows, cols]
            k_own = kc_ref[0, rows, cols]
            v_own = vc_ref[0, rows, cols]
            if sb == 0:
                k_prev = kp_ref[0, :, cols]
                v_prev = vp_ref[0, :, cols]
            else:
                k_prev = kc_ref[0, (sb - 1) * w: sb * w, cols]
                v_prev = vc_ref[0, (sb - 1) * w: sb * w, cols]
            slope = sl_ref[h]
            s_own = jnp.where(own_ok, _nt_dot(q, k_own) - slope * dist_own, NEG_BIG)
            s_prev = jnp.where(prev_ok, _nt_dot(q, k_prev) - slope * dist_prev, NEG_BIG)
            m = jnp.maximum(jnp.max(s_own, axis=-1, keepdims=True),
                            jnp.max(s_prev, axis=-1, keepdims=True))
            p_own = jnp.exp2(s_own - m)
            p_prev = jnp.exp2(s_prev - m)
            den = jnp.sum(p_own, axis=-1, keepdims=True) + jnp.sum(p_prev, axis=-1, keepdims=True)
            o = (jnp.dot(p_own.astype(BF16), v_own, preferred_element_type=F32)
                 + jnp.dot(p_prev.astype(BF16), v_prev, preferred_element_type=F32)) / den
            o_ref[0, rows, cols] = o
            lse = (m + jnp.log2(den)) * LN2
            lse_ref[0, rows, cols] = jnp.broadcast_to(lse, (w, DIL_HEAD_DIM))


def dil_group_attention(dil, pos_f, g, tq=512):
    b, s, c = dil.shape
    r = DIL_RATES[g]
    sr = s // r
    tq = min(tq, sr)
    w = DIL_SPAN
    assert sr % tq == 0 and tq % w == 0
    nsub = tq // w
    view = dil.reshape(b, sr, r * c)
    ncb = c // DIL_GW
    pos_r = pos_f.reshape(b, sr, r).transpose(0, 2, 1)
    pos_q = pos_r[..., None]
    pos_k = pos_r[:, :, None, :]
    n_h = DIL_GROUPS * DIL_HEADS
    slopes = 2.0 ** (-ALIBI_MAX_BIAS * jnp.arange(1, n_h + 1, dtype=F32) / n_h)
    slopes2 = (slopes * LOG2E)[g * DIL_HEADS:(g + 1) * DIL_HEADS]

    def cur(off):
        return pl.BlockSpec((1, tq, DIL_GW), lambda i, p, t: (i, t, p * ncb + off + g))

    def prev(off):
        return pl.BlockSpec((1, w, DIL_GW), lambda i, p, t: (i, jnp.maximum(t * nsub - 1, 0), p * ncb + off + g))

    out_spec = pl.BlockSpec((1, tq, DIL_GW), lambda i, p, t: (i, t, p))
    out_shape = jax.ShapeDtypeStruct((b, sr, r * DIL_GW), F32)
    o, lse = pl.pallas_call(
        functools.partial(_dil_kernel, tq=tq),
        grid=(b, r, sr // tq),
        in_specs=[cur(0), cur(DIL_GROUPS), cur(2 * DIL_GROUPS), prev(DIL_GROUPS), prev(2 * DIL_GROUPS),
                  pl.BlockSpec((1, 1, tq, 1), lambda i, p, t: (i, p, t, 0)),
                  pl.BlockSpec((1, 1, 1, tq), lambda i, p, t: (i, p, 0, t)),
                  pl.BlockSpec((1, 1, 1, w), lambda i, p, t: (i, p, 0, jnp.maximum(t * nsub - 1, 0))),
                  pl.BlockSpec(memory_space=pltpu.SMEM)],
        out_specs=[out_spec, out_spec],
        out_shape=[out_shape, out_shape],
        compiler_params=_cp(("arbitrary", "arbitrary", "arbitrary")),
        name=f"dil_attention_g{g}",
    )(view, view, view, view, view, pos_q, pos_k, pos_k, slopes2)
    return o.reshape(b, s, DIL_GW), lse.reshape(b, s, DIL_GW)


def _merge_kernel(ym_ref, o0_ref, o1_ref, o2_ref, l0_ref, l1_ref, l2_ref, ys_ref,
                  g0_ref, g1_ref, g2_ref, wm_ref, wd_ref, ws_ref, out_ref):
    l0, l1, l2 = l0_ref[0], l1_ref[0], l2_ref[0]
    mx = jnp.maximum(jnp.maximum(l0, l1), l2)
    e0, e1, e2 = jnp.exp(l0 - mx), jnp.exp(l1 - mx), jnp.exp(l2 - mx)
    y_dil = (e0 * o0_ref[0] + e1 * o1_ref[0] + e2 * o2_ref[0]) / (e0 + e1 + e2)

    def sig(ref):
        return 1.0 / (1.0 + jnp.exp(-ref[0].astype(F32)))

    merged = (sig(g0_ref) * jnp.dot(ym_ref[0], wm_ref[...], preferred_element_type=F32)
              + sig(g1_ref) * jnp.dot(y_dil.astype(BF16), wd_ref[...], preferred_element_type=F32)
              + sig(g2_ref) * jnp.dot(ys_ref[0], ws_ref[...], preferred_element_type=F32))
    out_ref[0] = merged.astype(out_ref.dtype)


def merge_branches(y_mla, dil_o, dil_lse, y_sb, gates, w_o_mla, w_o_dil, w_o_sb, tm=256):
    b, s, _ = y_mla.shape
    d = D_MODEL
    tm = min(tm, s)

    def row(width, cb=0):
        return pl.BlockSpec((1, tm, width), lambda i, j: (i, j, cb))

    def full(w):
        return pl.BlockSpec(w.shape, lambda i, j: (0, 0))

    return pl.pallas_call(
        _merge_kernel, grid=(b, s // tm),
        in_specs=[row(y_mla.shape[2])] + [row(DIL_GW)] * 6 + [row(SB_WIDTH),
                  row(d, 0), row(d, 1), row(d, 2), full(w_o_mla), full(w_o_dil), full(w_o_sb)],
        out_specs=row(d),
        out_shape=jax.ShapeDtypeStruct((b, s, d), BF16),
        compiler_params=_cp(("arbitrary", "arbitrary")),
        name="merge_branches",
    )(y_mla, *dil_o, *dil_lse, y_sb, gates, gates, gates, w_o_mla, w_o_dil, w_o_sb)


def _resid_proj_kernel(a_ref, w_ref, x_ref, g_ref, o_ref):
    y = jnp.dot(a_ref[0], w_ref[...], preferred_element_type=F32)
    o_ref[0] = x_ref[0] + g_ref[0] * y


def resid_project(a, w, x, mod3, gate_idx, tm=512, tn=1024):
    b, s, k = a.shape
    n = w.shape[1]
    tm = min(tm, s)
    return pl.pallas_call(
        _resid_proj_kernel, grid=(b, s // tm, n // tn),
        in_specs=[pl.BlockSpec((1, tm, k), lambda i, j, c: (i, j, 0)),
                  pl.BlockSpec((k, tn), lambda i, j, c: (0, c)),
                  pl.BlockSpec((1, tm, tn), lambda i, j, c: (i, j, c)),
                  pl.BlockSpec((1, 1, tn), lambda i, j, c: (i, 0, gate_idx * (n // tn) + c))],
        out_specs=pl.BlockSpec((1, tm, tn), lambda i, j, c: (i, j, c)),
        out_shape=jax.ShapeDtypeStruct((b, s, n), F32),
        compiler_params=_cp(("arbitrary", "arbitrary", "arbitrary")),
        name="resid_project",
    )(a, w, x, mod3)


def _router_kernel(lg_ref, b_ref, o_ref):
    lg = lg_ref[0]
    tm = lg.shape[1]
    sc = [1.0 / (1.0 + jnp.exp(-lg[e:e + 1, :])) for e in range(N_EXPERTS)]
    bi = [sc[e] + b_ref[e] for e in range(N_EXPERTS)]
    n = EXPERTS_PER_GROUP
    gscore = []
    for g in range(N_GROUPS):
        v = bi[g * n:(g + 1) * n]
        pair_max = None
        for a in range(n):
            for c in range(a + 1, n):
                pm = v[a] + v[c]
                pair_max = pm if pair_max is None else jnp.maximum(pair_max, pm)
        gscore.append(pair_max)
    best = gscore[0]
    gsel = jnp.zeros((1, tm), jnp.int32)
    for g in range(1, N_GROUPS):
        better = gscore[g] > best
        best = jnp.where(better, gscore[g], best)
        gsel = jnp.where(better, g, gsel)
    gb = [sum(jnp.where(gsel == g, bi[g * n + i], 0.0) for g in range(N_GROUPS)) for i in range(n)]
    gs = [sum(jnp.where(gsel == g, sc[g * n + i], 0.0) for g in range(N_GROUPS)) for i in range(n)]
    sel = []
    for i in range(n):
        beaten = jnp.zeros((1, tm), jnp.int32)
        for j in range(n):
            if j == i:
                continue
            wins = (gb[j] > gb[i]) if j > i else (gb[j] >= gb[i])
            beaten = beaten + wins.astype(jnp.int32)
        sel.append(beaten < 2)
    den = sum(jnp.where(sel[i], gs[i], 0.0) for i in range(n))
    for e in range(N_EXPERTS):
        g, i = divmod(e, n)
        o_ref[0, e:e + 1, :] = jnp.where(jnp.logical_and(gsel == g, sel[i]), gs[i] / den, 0.0)


def router_gates(logits_t, b_router, tm=1024):
    b, e, s = logits_t.shape
    tm = min(tm, s)
    return pl.pallas_call(
        _router_kernel, grid=(b, s // tm),
        in_specs=[pl.BlockSpec((1, e, tm), lambda i, j: (i, 0, j)),
                  pl.BlockSpec(memory_space=pltpu.SMEM)],
        out_specs=pl.BlockSpec((1, e, tm), lambda i, j: (i, 0, j)),
        out_shape=jax.ShapeDtypeStruct((b, e, s), F32),
        compiler_params=_cp(("arbitrary", "arbitrary")),
        name="router_gates",
    )(logits_t, b_router)


def _moe_kernel(a_ref, gt_ref, wg_ref, wu_ref, wd_ref, x_ref, gm_ref, o_ref, acc_ref):
    e = pl.program_id(2)

    @pl.when(e == 0)
    def _():
        acc_ref[...] = jnp.zeros(acc_ref.shape, F32)

    a = a_ref[0]
    hg = jnp.dot(a, wg_ref[0], preferred_element_type=F32)
    hu = jnp.dot(a, wu_ref[0], preferred_element_type=F32)
    onehot = lax.broadcasted_iota(jnp.int32, (1, N_EXPERTS), 1) == e
    gate = jnp.sum(jnp.where(onehot, gt_ref[0], 0.0), axis=-1, keepdims=True)
    h = (hg / (1.0 + jnp.exp(-hg))) * hu * gate
    acc_ref[...] += jnp.dot(h.astype(BF16), wd_ref[0], preferred_element_type=F32)

    @pl.when(e == N_EXPERTS - 1)
    def _():
        o_ref[0] = x_ref[0] + gm_ref[0] * acc_ref[...]


def moe_dense(a, gates, w_gate, w_up, w_down, x, mod3, gate_idx, tm=512):
    b, s, d = a.shape
    tm = min(tm, s)
    ne = w_gate.shape[0]
    return pl.pallas_call(
        _moe_kernel, grid=(b, s // tm, ne),
        in_specs=[pl.BlockSpec((1, tm, d), lambda i, j, e: (i, j, 0)),
                  pl.BlockSpec((1, tm, ne), lambda i, j, e: (i, j, 0)),
                  pl.BlockSpec((1, d, D_EXPERT), lambda i, j, e: (e, 0, 0)),
                  pl.BlockSpec((1, d, D_EXPERT), lambda i, j, e: (e, 0, 0)),
                  pl.BlockSpec((1, D_EXPERT, d), lambda i, j, e: (e, 0, 0)),
                  pl.BlockSpec((1, tm, d), lambda i, j, e: (i, j, 0)),
                  pl.BlockSpec((1, 1, d), lambda i, j, e: (i, 0, gate_idx))],
        out_specs=pl.BlockSpec((1, tm, d), lambda i, j, e: (i, j, 0)),
        out_shape=jax.ShapeDtypeStruct((b, s, d), F32),
        scratch_shapes=[pltpu.VMEM((tm, d), F32)],
        compiler_params=_cp(("arbitrary", "arbitrary", "arbitrary")),
        name="moe_dense",
    )(a, gates, w_gate, w_up, w_down, x, mod3)


def _final_norm_kernel(x_ref, g_ref, o_ref):
    x = x_ref[0]
    o_ref[0] = x * lax.rsqrt(jnp.mean(x * x, axis=-1, keepdims=True) + EPS) * g_ref[...]


def final_norm(x, g, tm=512):
    b, s, d = x.shape
    tm = min(tm, s)
    return pl.pallas_call(
        _final_norm_kernel, grid=(b, s // tm),
        in_specs=[pl.BlockSpec((1, tm, d), lambda i, j: (i, j, 0)),
                  pl.BlockSpec((1, d), lambda i, j: (0, 0))],
        out_specs=pl.BlockSpec((1, tm, d), lambda i, j: (i, j, 0)),
        out_shape=jax.ShapeDtypeStruct((b, s, d), F32),
        compiler_params=_cp(("arbitrary", "arbitrary")),
        name="final_norm",
    )(x, g.reshape(1, d))


def _rot_cols(w):
    half = w.shape[-1] // 2
    return jnp.concatenate([-w[..., half:], w[..., :half]], axis=-1)


def _prep_layer(w_in, w_uq, w_ukv, w_o_mla, w_o_dil, w_o_sb, w_out, w_gate, w_up, w_down):
    o1 = Q_LORA
    o2 = o1 + KV_LORA
    o3 = o2 + ROPE_DIM
    o4 = o3 + 3 * DIL_WIDTH
    o5 = o4 + 3 * SB_WIDTH
    w_kr = w_in[:, o2:o3]
    w_lat = jnp.concatenate([w_in[:, :o2], w_kr, _rot_cols(w_kr)], axis=1).astype(BF16)
    dil_scale = DIL_HEAD_DIM ** -0.5 * LOG2E
    w_dil = jnp.concatenate([w_in[:, o3:o3 + DIL_WIDTH] * dil_scale, w_in[:, o3 + DIL_WIDTH:o4]],
                            axis=1).astype(BF16)
    sb_scale = -(SB_HEAD_DIM ** -0.5) * LOG2E
    w_sb = jnp.concatenate([w_in[:, o4:o4 + SB_WIDTH] * sb_scale, w_in[:, o4 + SB_WIDTH:o5]],
                           axis=1).astype(BF16)
    w_gl = w_in[:, o5:].astype(BF16)
    q_scale = MLA_QK ** -0.5 * LOG2E
    wq = (w_uq * q_scale).reshape(Q_LORA, MLA_HEADS, MLA_QK).transpose(1, 0, 2)
    wq_h = jnp.concatenate([wq, _rot_cols(wq[..., NOPE_DIM:])], axis=-1).astype(BF16)
    wkv_h = w_ukv.reshape(KV_LORA, MLA_HEADS, NOPE_DIM + MLA_V_DIM).transpose(1, 0, 2).astype(BF16)
    return dict(w_lat=w_lat, w_dil=w_dil, w_sb=w_sb, w_gl=w_gl, wq_h=wq_h, wkv_h=wkv_h,
                w_o_mla=w_o_mla.astype(BF16), w_o_dil=w_o_dil.astype(BF16), w_o_sb=w_o_sb.astype(BF16),
                w_out=w_out.astype(BF16), w_gate=w_gate.astype(BF16), w_up=w_up.astype(BF16),
                w_down=w_down.astype(BF16))


def kernel(x, c, positions, w_ada, b_ada, g_mix, g_moe, w_in, g_q, w_uq, g_kv, w_ukv, w_o_mla, w_o_dil,
           w_o_sb, w_out, w_router, b_router, w_gate, w_up, w_down, g_final):
    b, s, d = x.shape
    depth = w_ada.shape[0]
    pos_f = positions.astype(F32)
    cos, sin = rope_tables(pos_f.reshape(b, s, 1))
    mod = ada_mod(c, w_ada, b_ada)
    w_router_t = w_router.T
    for l in range(depth):
        p = _prep_layer(w_in[l], w_uq[l], w_ukv[l], w_o_mla[l], w_o_dil[l], w_o_sb[l], w_out[l],
                        w_gate[l], w_up[l], w_down[l])
        mod3 = mod[l].reshape(b, 1, 6 * d)

        a = norm_mod(x, g_mix[l], mod3, sc_idx=1, sh_idx=0)
        lat = project(a, p["w_lat"], tn=p["w_lat"].shape[1])
        dil = project(a, p["w_dil"])
        sbp = project(a, p["w_sb"])
        gl = project(a, p["w_gl"])

        q, k, v = mla_project(lat, g_q[l], p["wq_h"], g_kv[l], p["wkv_h"], cos, sin)
        y_mla = mla_attention(q, k, v)
        dil_out = [dil_group_attention(dil, pos_f, g) for g in range(DIL_GROUPS)]
        y_sb = sb_attention(sbp)
        merged = merge_branches(y_mla, [o for o, _ in dil_out], [e for _, e in dil_out], y_sb, gl,
                                p["w_o_mla"], p["w_o_dil"], p["w_o_sb"])
        x = resid_project(merged, p["w_out"], x, mod3, gate_idx=2)

        a2, logits_t = norm_mod(x, g_moe[l], mod3, sc_idx=4, sh_idx=3, w_router_t=w_router_t)
        gates_t = router_gates(logits_t, b_router)
        gates = gates_t.transpose(0, 2, 1)
        x = moe_dense(a2, gates, p["w_gate"], p["w_up"], p["w_down"], x, mod3, gate_idx=5)
    return final_norm(x, g_final)
```

```python
import functools
import math

import jax
import jax.numpy as jnp
from jax import lax
from jax.experimental import pallas as pl
from jax.experimental.pallas import tpu as pltpu

D_MODEL = 2048
EPS = 1e-6
MLA_HEADS = 8
Q_LORA = 512
KV_LORA = 512
NOPE_DIM = 128
ROPE_DIM = 64
MLA_V_DIM = 128
ROPE_THETA = 10000.0
MLA_QK = NOPE_DIM + ROPE_DIM
DIL_WINDOWS = (128, 512, 2048)
DIL_RATES = (1, 4, 16)
DIL_GROUPS = 3
DIL_HEADS = 4
DIL_HEAD_DIM = 128
DIL_SPAN = DIL_WINDOWS[0] // DIL_RATES[0]
DIL_GW = DIL_HEADS * DIL_HEAD_DIM
DIL_WIDTH = DIL_GROUPS * DIL_GW
ALIBI_MAX_BIAS = 8.0
SB_HEADS = 8
SB_HEAD_DIM = 128
SB_WIDTH = SB_HEADS * SB_HEAD_DIM
N_EXPERTS = 16
N_GROUPS = 4
EXPERTS_PER_GROUP = N_EXPERTS // N_GROUPS
D_EXPERT = 512

LOG2E = math.log2(math.e)
LN2 = math.log(2.0)
NEG_BIG = -1e30

V7X_LANES = 128
V7X_VMEM_BYTES = 64 * 1024 * 1024
VMEM_LIMIT = 56 * 1024 * 1024

F32 = jnp.float32
BF16 = jnp.bfloat16


def _cp(sem, vmem=VMEM_LIMIT):
    return pltpu.CompilerParams(dimension_semantics=sem, vmem_limit_bytes=vmem)


def _lane_tile(x, n):
    return x if n == 1 else jnp.concatenate([x] * n, axis=1)


def _nt_dot(a, b):
    return lax.dot_general(a, b, (((1,), (1,)), ((), ())), preferred_element_type=F32)


def _ada_kernel(c_ref, w_ref, b_ref, o_ref):
    c = c_ref[...]
    ca = c * (1.0 / (1.0 + jnp.exp(-c)))
    o_ref[0] = jnp.dot(ca, w_ref[0], preferred_element_type=F32,
                       precision=lax.Precision.HIGHEST) + b_ref[0]


def ada_mod(c, w_ada, b_ada, tn=1024):
    depth, d, n = w_ada.shape
    b = c.shape[0]
    return pl.pallas_call(
        _ada_kernel,
        grid=(depth, n // tn),
        in_specs=[pl.BlockSpec((b, d), lambda l, j: (0, 0)),
                  pl.BlockSpec((1, d, tn), lambda l, j: (l, 0, j)),
                  pl.BlockSpec((1, 1, tn), lambda l, j: (l, 0, j))],
        out_specs=pl.BlockSpec((1, b, tn), lambda l, j: (l, 0, j)),
        out_shape=jax.ShapeDtypeStruct((depth, b, n), F32),
        compiler_params=_cp(("arbitrary", "arbitrary")),
        name="ada_mod",
    )(c, w_ada, b_ada.reshape(depth, 1, n))


def _rope_table_kernel(pos_ref, inv_ref, cos_ref, sin_ref):
    ang = pos_ref[0] * inv_ref[...]
    cos_ref[0] = jnp.cos(ang)
    sin_ref[0] = jnp.sin(ang)


def rope_tables(pos_col, tm=512):
    b, s, _ = pos_col.shape
    tm = min(tm, s)
    half = ROPE_DIM // 2
    inv = ROPE_THETA ** (-jnp.arange(half, dtype=F32) / half)
    inv2 = jnp.concatenate([inv, inv]).reshape(1, ROPE_DIM)
    shp = jax.ShapeDtypeStruct((b, s, ROPE_DIM), F32)
    return pl.pallas_call(
        _rope_table_kernel,
        grid=(b, s // tm),
        in_specs=[pl.BlockSpec((1, tm, 1), lambda i, j: (i, j, 0)),
                  pl.BlockSpec((1, ROPE_DIM), lambda i, j: (0, 0))],
        out_specs=[pl.BlockSpec((1, tm, ROPE_DIM), lambda i, j: (i, j, 0))] * 2,
        out_shape=[shp, shp],
        compiler_params=_cp(("arbitrary", "arbitrary")),
        name="rope_tables",
    )(pos_col, inv2)


def _norm_mod(x, g, scale, shift):
    y = x * lax.rsqrt(jnp.mean(x * x, axis=-1, keepdims=True) + EPS)
    return (y * g) * (1.0 + scale) + shift


def _norm_kernel(x_ref, g_ref, sc_ref, sh_ref, o_ref):
    o_ref[0] = _norm_mod(x_ref[0], g_ref[...], sc_ref[0], sh_ref[0]).astype(o_ref.dtype)


def _norm_router_kernel(x_ref, g_ref, sc_ref, sh_ref, wr_ref, o_ref, lg_ref):
    a = _norm_mod(x_ref[0], g_ref[...], sc_ref[0], sh_ref[0])
    o_ref[0] = a.astype(o_ref.dtype)
    lg_ref[0] = lax.dot_general(wr_ref[...], a, (((1,), (1,)), ((), ())),
                                preferred_element_type=F32, precision=lax.Precision.HIGHEST)


def norm_mod(x, g, mod3, sc_idx, sh_idx, w_router_t=None, tm=512):
    b, s, d = x.shape
    tm = min(tm, s)
    in_specs = [pl.BlockSpec((1, tm, d), lambda i, j: (i, j, 0)),
                pl.BlockSpec((1, d), lambda i, j: (0, 0)),
                pl.BlockSpec((1, 1, d), lambda i, j: (i, 0, sc_idx)),
                pl.BlockSpec((1, 1, d), lambda i, j: (i, 0, sh_idx))]
    o_spec = pl.BlockSpec((1, tm, d), lambda i, j: (i, j, 0))
    o_shape = jax.ShapeDtypeStruct((b, s, d), BF16)
    if w_router_t is None:
        return pl.pallas_call(
            _norm_kernel, grid=(b, s // tm), in_specs=in_specs, out_specs=o_spec, out_shape=o_shape,
            compiler_params=_cp(("arbitrary", "arbitrary")), name="norm_mod",
        )(x, g.reshape(1, d), mod3, mod3)
    e = w_router_t.shape[0]
    return pl.pallas_call(
        _norm_router_kernel, grid=(b, s // tm),
        in_specs=in_specs + [pl.BlockSpec((e, d), lambda i, j: (0, 0))],
        out_specs=[o_spec, pl.BlockSpec((1, e, tm), lambda i, j: (i, 0, j))],
        out_shape=[o_shape, jax.ShapeDtypeStruct((b, e, s), F32)],
        compiler_params=_cp(("arbitrary", "arbitrary")), name="norm_router",
    )(x, g.reshape(1, d), mod3, mod3, w_router_t)


def _matmul_kernel(a_ref, w_ref, o_ref):
    o_ref[0] = jnp.dot(a_ref[0], w_ref[...], preferred_element_type=F32).astype(o_ref.dtype)


def project(a, w, tm=1024, tn=512, out_dtype=BF16):
    b, s, k = a.shape
    n = w.shape[1]
    tm = min(tm, s)
    tn = min(tn, n)
    assert n % tn == 0 and s % tm == 0
    return pl.pallas_call(
        _matmul_kernel,
        grid=(b, s // tm, n // tn),
        in_specs=[pl.BlockSpec((1, tm, k), lambda i, j, c: (i, j, 0)),
                  pl.BlockSpec((k, tn), lambda i, j, c: (0, c))],
        out_specs=pl.BlockSpec((1, tm, tn), lambda i, j, c: (i, j, c)),
        out_shape=jax.ShapeDtypeStruct((b, s, n), out_dtype),
        compiler_params=_cp(("arbitrary", "arbitrary", "arbitrary")),
        name="project",
    )(a, w)


def _matmul_residue_kernel(a_ref, w_ref, o_ref, res_scr, *, rate):
    res = jnp.dot(a_ref[0], w_ref[...], preferred_element_type=F32)
    slabs, rows, lanes = res_scr.shape
    for c in range(slabs):
        res_scr[c] = res[:, c * lanes:(c + 1) * lanes]
    n = rows // rate
    for rho in range(rate):
        for c in range(slabs):
            o_ref[0, rho, :, c * lanes:(c + 1) * lanes] = (
                res_scr[c, pl.ds(rho, n, stride=rate), :].astype(o_ref.dtype))


def project_by_residue(a, w, rate, tm=1024, tn=512):
    b, s, k = a.shape
    n = w.shape[1]
    if rate == 1:
        return project(a, w, tm=tm, tn=tn).reshape(b, 1, s, n)
    tm = min(tm, s)
    assert n % tn == 0 and s % tm == 0 and tm % (rate * 16) == 0
    return pl.pallas_call(
        functools.partial(_matmul_residue_kernel, rate=rate),
        grid=(b, s // tm, n // tn),
        in_specs=[pl.BlockSpec((1, tm, k), lambda i, j, c: (i, j, 0)),
                  pl.BlockSpec((k, tn), lambda i, j, c: (0, c))],
        out_specs=pl.BlockSpec((1, rate, tm // rate, tn), lambda i, j, c: (i, 0, j, c)),
        out_shape=jax.ShapeDtypeStruct((b, rate, s // rate, n), BF16),
        scratch_shapes=[pltpu.VMEM((tn // V7X_LANES, tm, V7X_LANES), F32)],
        compiler_params=_cp(("arbitrary", "arbitrary", "arbitrary")),
        name=f"project_by_residue_{rate}",
    )(a, w)


def _latent_norm(c, g):
    c = c.astype(F32)
    return (c * lax.rsqrt(jnp.mean(c * c, axis=-1, keepdims=True) + EPS) * g).astype(BF16)


def _mla_proj_kernel(cq_ref, ckv_ref, kr_ref, gq_ref, gkv_ref, wq_ref, wkv_ref, cos_ref, sin_ref,
                     q_ref, k_ref, v_ref):
    cos, sin = cos_ref[0], sin_ref[0]
    rq = _latent_norm(cq_ref[0], gq_ref[...])
    rkv = _latent_norm(ckv_ref[0], gkv_ref[...])
    kr = kr_ref[0].astype(F32)
    k_rope = (kr[:, :ROPE_DIM] * cos + kr[:, ROPE_DIM:] * sin).astype(k_ref.dtype)
    ones = jnp.ones((rq.shape[0], MLA_V_DIM), v_ref.dtype)
    for h in range(MLA_HEADS):
        r = jnp.dot(rq, wq_ref[h], preferred_element_type=F32)
        q_ref[0, h, :, :NOPE_DIM] = r[:, :NOPE_DIM].astype(q_ref.dtype)
        roped = r[:, NOPE_DIM:NOPE_DIM + ROPE_DIM] * cos + r[:, NOPE_DIM + ROPE_DIM:] * sin
        q_ref[0, h, :, NOPE_DIM:] = roped.astype(q_ref.dtype)
        r = jnp.dot(rkv, wkv_ref[h], preferred_element_type=F32)
        k_ref[0, h, :, :NOPE_DIM] = r[:, :NOPE_DIM].astype(k_ref.dtype)
        k_ref[0, h, :, NOPE_DIM:] = k_rope
        v_ref[0, h, :, :MLA_V_DIM] = r[:, NOPE_DIM:].astype(v_ref.dtype)
        v_ref[0, h, :, MLA_V_DIM:] = ones


def mla_project(lat, g_q, wq_h, g_kv, wkv_h, cos, sin, tm=512):
    b, s, _ = lat.shape
    tm = min(tm, s)
    h = MLA_HEADS
    kr_blk = (Q_LORA + KV_LORA) // (2 * ROPE_DIM)
    tab = pl.BlockSpec((1, tm, ROPE_DIM), lambda i, j: (i, j, 0))

    def const(shape):
        return pl.BlockSpec(shape, lambda i, j: (0,) * len(shape))

    def head_major(width):
        return pl.BlockSpec((1, h, tm, width), lambda i, j: (i, 0, j, 0))

    return pl.pallas_call(
        _mla_proj_kernel, grid=(b, s // tm),
        in_specs=[pl.BlockSpec((1, tm, Q_LORA), lambda i, j: (i, j, 0)),
                  pl.BlockSpec((1, tm, KV_LORA), lambda i, j: (i, j, 1)),
                  pl.BlockSpec((1, tm, 2 * ROPE_DIM), lambda i, j: (i, j, kr_blk)),
                  const((1, Q_LORA)), const((1, KV_LORA)),
                  const((h, Q_LORA, 256)), const((h, KV_LORA, 256)), tab, tab],
        out_specs=[head_major(MLA_QK), head_major(MLA_QK), head_major(2 * MLA_V_DIM)],
        out_shape=[jax.ShapeDtypeStruct((b, h, s, MLA_QK), BF16),
                   jax.ShapeDtypeStruct((b, h, s, MLA_QK), BF16),
                   jax.ShapeDtypeStruct((b, h, s, 2 * MLA_V_DIM), BF16)],
        compiler_params=_cp(("arbitrary", "arbitrary")), name="mla_project",
    )(lat, lat, lat, g_q.reshape(1, Q_LORA), g_kv.reshape(1, KV_LORA), wq_h, wkv_h, cos, sin)


def _mla_attn_kernel(q_ref, k_ref, v_ref, o_ref, s0_scr, s1_scr, m_scr, acc_scr, *, tq, tk):
    qi = pl.program_id(2)
    m_scr[...] = jnp.full(m_scr.shape, NEG_BIG, F32)
    acc_scr[...] = jnp.zeros(acc_scr.shape, F32)

    def scores(c, dst, row0=0):
        start = pl.multiple_of(c * tk, tk)
        dst[row0:, :] = _nt_dot(q_ref[0, 0, row0:, :], k_ref[0, 0, pl.ds(start, tk), :])

    def absorb(src, c, col_off=None, row0=0):
        start = pl.multiple_of(c * tk, tk)
        s = src[row0:, :]
        n = tq - row0
        if col_off is not None:
            row = row0 + lax.broadcasted_iota(jnp.int32, (n, tk), 0)
            col = col_off + lax.broadcasted_iota(jnp.int32, (n, tk), 1)
            s = jnp.where(col <= row, s, NEG_BIG)
        m_prev = m_scr[row0:, :]
        m_new = jnp.maximum(m_prev, jnp.max(s, axis=-1, keepdims=True))
        p = jnp.exp2(s - _lane_tile(m_new, tk // V7X_LANES))
        alpha = jnp.exp2(m_prev - m_new)
        pv = jnp.dot(p.astype(BF16), v_ref[0, 0, pl.ds(start, tk), :], preferred_element_type=F32)
        acc_scr[row0:, :] = _lane_tile(alpha, 2) * acc_scr[row0:, :] + pv
        m_scr[row0:, :] = m_new

    scores(0, s0_scr)

    def body(i, carry):
        scores(2 * i + 1, s1_scr)
        absorb(s0_scr, 2 * i)
        scores(2 * i + 2, s0_scr)
        absorb(s1_scr, 2 * i + 1)
        return carry

    lax.fori_loop(0, qi, body, 0)
    scores(2 * qi + 1, s1_scr, row0=tk)
    absorb(s0_scr, 2 * qi, col_off=0)
    absorb(s1_scr, 2 * qi + 1, col_off=tk, row0=tk)
    acc = acc_scr[...]
    o_ref[0] = (acc[:, :MLA_V_DIM] / acc[:, MLA_V_DIM:]).astype(o_ref.dtype)


def mla_attention(q, k, v, tq=1024):
    b, h, s, _ = q.shape
    tq = min(tq, s)
    tk = tq // 2
    return pl.pallas_call(
        functools.partial(_mla_attn_kernel, tq=tq, tk=tk),
        grid=(b, h, s // tq),
        in_specs=[pl.BlockSpec((1, 1, tq, MLA_QK), lambda i, j, t: (i, j, t, 0)),
                  pl.BlockSpec((1, 1, s, MLA_QK), lambda i, j, t: (i, j, 0, 0)),
                  pl.BlockSpec((1, 1, s, 2 * MLA_V_DIM), lambda i, j, t: (i, j, 0, 0))],
        out_specs=pl.BlockSpec((1, tq, MLA_V_DIM), lambda i, j, t: (i, t, j)),
        out_shape=jax.ShapeDtypeStruct((b, s, h * MLA_V_DIM), BF16),
        scratch_shapes=[pltpu.VMEM((tq, tk), F32), pltpu.VMEM((tq, tk), F32),
                        pltpu.VMEM((tq, V7X_LANES), F32), pltpu.VMEM((tq, 2 * MLA_V_DIM), F32)],
        compiler_params=_cp(("arbitrary", "arbitrary", "arbitrary")),
        name="mla_attention",
    )(q, k, v)


SB_SKIP_LOG2 = 150.0


def _sb_attn_kernel(q_ref, k_ref, v_ref, tri_ref, o_ref, c_scr, acc_scr, *, tq, hp):
    qi = pl.program_id(2)
    c_scr[...] = jnp.zeros(c_scr.shape, F32)
    acc_scr[...] = jnp.zeros(acc_scr.shape, F32)
    reps = tq // V7X_LANES

    def chunk(j, diagonal):
        start = pl.multiple_of(j * tq, tq)
        heads = range(hp)
        cols = [slice(hh * SB_HEAD_DIM, (hh + 1) * SB_HEAD_DIM) for hh in heads]
        if diagonal:
            before = (lax.broadcasted_iota(jnp.int32, (tq, tq), 1)
                      < lax.broadcasted_iota(jnp.int32, (tq, tq), 0))
        zn = [_nt_dot(q_ref[0, :, cols[hh]], k_ref[0, pl.ds(start, tq), cols[hh]]) for hh in heads]
        lk = [jnp.minimum(z, 0.0) - jnp.log2(1.0 + jnp.exp2(-jnp.abs(z))) for z in zn]
        if diagonal:
            lk = [jnp.where(before, x, 0.0) for x in lk]
        hi = [x.astype(BF16) for x in lk]
        lo = [(x - h.astype(F32)).astype(BF16) for x, h in zip(lk, hi)]
        suffix = [jnp.dot(jnp.concatenate([h, l], axis=1), tri_ref[...], preferred_element_type=F32)
                  for h, l in zip(hi, lo)]
        c = [c_scr[hh] for hh in heads]
        a = [jnp.exp2(sf - z + _lane_tile(cc, reps)) for sf, z, cc in zip(suffix, zn, c)]
        if diagonal:
            a = [jnp.where(before, x, 0.0) for x in a]
        for hh in heads:
            acc_scr[:, cols[hh]] += jnp.dot(a[hh].astype(BF16), v_ref[0, pl.ds(start, tq), cols[hh]],
                                            preferred_element_type=F32)
            c_scr[hh] = c[hh] + jnp.sum(lk[hh], axis=-1, keepdims=True)

    chunk(qi, True)

    def cond(state):
        j, go = state
        return jnp.logical_and(j >= 0, go > 0)

    def body(state):
        j, _ = state
        chunk(j, False)
        go = (jnp.max(c_scr[...]) > -SB_SKIP_LOG2).astype(jnp.int32)
        return j - 1, go

    lax.while_loop(cond, body, (qi - 1, jnp.int32(1)))
    o_ref[0] = acc_scr[...].astype(o_ref.dtype)


def sb_attention(qkv, tq=256, hp=4):
    b, s, _ = qkv.shape
    tq = min(tq, s)
    hw = hp * SB_HEAD_DIM
    nhb = SB_HEADS // hp
    tri = (lax.broadcasted_iota(jnp.int32, (tq, tq), 0)
           >= lax.broadcasted_iota(jnp.int32, (tq, tq), 1)).astype(BF16)
    tri2 = jnp.concatenate([tri, tri], axis=0)
    return pl.pallas_call(
        functools.partial(_sb_attn_kernel, tq=tq, hp=hp),
        grid=(b, nhb, s // tq),
        in_specs=[pl.BlockSpec((1, tq, hw), lambda i, j, t: (i, t, j)),
                  pl.BlockSpec((1, s, hw), lambda i, j, t: (i, 0, nhb + j)),
                  pl.BlockSpec((1, s, hw), lambda i, j, t: (i, 0, 2 * nhb + j)),
                  pl.BlockSpec((2 * tq, tq), lambda i, j, t: (0, 0))],
        out_specs=pl.BlockSpec((1, tq, hw), lambda i, j, t: (i, t, j)),
        out_shape=jax.ShapeDtypeStruct((b, s, SB_WIDTH), BF16),
        scratch_shapes=[pltpu.VMEM((hp, tq, V7X_LANES), F32), pltpu.VMEM((tq, hw), F32)],
        compiler_params=_cp(("arbitrary", "arbitrary", "arbitrary")),
        name="sb_attention",
    )(qkv, qkv, qkv, tri2)


def _dil_kernel(q_ref, kc_ref, vc_ref, kp_ref, vp_ref, pq_ref, pkc_ref, pkp_ref, sl_ref,
                o_ref, lse_ref, *, tq):
    ti = pl.program_id(2)
    w = DIL_SPAN
    a_idx = lax.broadcasted_iota(jnp.int32, (w, w), 0)
    c_idx = lax.broadcasted_iota(jnp.int32, (w, w), 1)
    own_ok = c_idx <= a_idx
    prev_tri = c_idx >= a_idx
    for sb in range(tq // w):
        rows = slice(sb * w, (sb + 1) * w)
        pq = pq_ref[0, 0, rows, :]
        pk_own = pkc_ref[0, 0, :, rows]
        if sb == 0:
            pk_prev = pkp_ref[0, 0]
            prev_ok = jnp.logical_and(prev_tri, ti > 0)
        else:
            pk_prev = pkc_ref[0, 0, :, (sb - 1) * w: sb * w]
            prev_ok = prev_tri
        dist_own = jnp.abs(pq - pk_own)
        dist_prev = jnp.abs(pq - pk_prev)
        heads = range(DIL_HEADS)
        cols = [slice(h * DIL_HEAD_DIM, (h + 1) * DIL_HEAD_DIM) for h in heads]
        prev_rows = slice((sb - 1) * w, sb * w)

        def k_prev(h):
            return kp_ref[0, :, cols[h]] if sb == 0 else kc_ref[0, prev_rows, cols[h]]

        def v_prev(h):
            return vp_ref[0, :, cols[h]] if sb == 0 else vc_ref[0, prev_rows, cols[h]]

        z_own = [_nt_dot(q_ref[0, rows, cols[h]], kc_ref[0, rows, cols[h]]) for h in heads]
        z_prev = [_nt_dot(q_ref[0, rows, cols[h]], k_prev(h)) for h in heads]
        s_own = [jnp.where(own_ok, z_own[h] - sl_ref[h] * dist_own, NEG_BIG) for h in heads]
        s_prev = [jnp.where(prev_ok, z_prev[h] - sl_ref[h] * dist_prev, NEG_BIG) for h in heads]
        m = [jnp.maximum(jnp.max(s_own[h], axis=-1, keepdims=True), jnp.max(s_prev[h], axis=-1, keepdims=True))
             for h in heads]
        p_own = [jnp.exp2(s_own[h] - m[h]) for h in heads]
        p_prev = [jnp.exp2(s_prev[h] - m[h]) for h in heads]
        den = [jnp.sum(p_own[h], axis=-1, keepdims=True) + jnp.sum(p_prev[h], axis=-1, keepdims=True)
               for h in heads]
        for h in heads:
            o = (jnp.dot(p_own[h].astype(BF16), vc_ref[0, rows, cols[h]], preferred_element_type=F32)
                 + jnp.dot(p_prev[h].astype(BF16), v_prev(h), preferred_element_type=F32)) / den[h]
            o_ref[0, rows, cols[h]] = o
            lse = (m[h] + jnp.log2(den[h])) * LN2
            lse_ref[0, rows, cols[h]] = jnp.broadcast_to(lse, (w, DIL_HEAD_DIM))


def dil_group_attention(dil_g, pos_f, g, tq=512):
    b, r, sr, c = dil_g.shape
    s = sr * r
    assert r == DIL_RATES[g]
    tq = min(tq, sr)
    w = DIL_SPAN
    assert sr % tq == 0 and tq % w == 0
    nsub = tq // w
    view = dil_g.reshape(b * r, sr, c)
    pos_r = pos_f.reshape(b, sr, r).transpose(0, 2, 1)
    pos_q = pos_r[..., None]
    pos_k = pos_r[:, :, None, :]
    n_h = DIL_GROUPS * DIL_HEADS
    slopes = 2.0 ** (-ALIBI_MAX_BIAS * jnp.arange(1, n_h + 1, dtype=F32) / n_h)
    slopes2 = (slopes * LOG2E)[g * DIL_HEADS:(g + 1) * DIL_HEADS]

    def cur(part):
        return pl.BlockSpec((1, tq, DIL_GW), lambda i, p, t: (i * r + p, t, part))

    def prev(part):
        return pl.BlockSpec((1, w, DIL_GW), lambda i, p, t: (i * r + p, jnp.maximum(t * nsub - 1, 0), part))

    out_spec = pl.BlockSpec((1, tq, DIL_GW), lambda i, p, t: (i, t, p))
    out_shape = jax.ShapeDtypeStruct((b, sr, r * DIL_GW), F32)
    o, lse = pl.pallas_call(
        functools.partial(_dil_kernel, tq=tq),
        grid=(b, r, sr // tq),
        in_specs=[cur(0), cur(1), cur(2), prev(1), prev(2),
                  pl.BlockSpec((1, 1, tq, 1), lambda i, p, t: (i, p, t, 0)),
                  pl.BlockSpec((1, 1, 1, tq), lambda i, p, t: (i, p, 0, t)),
                  pl.BlockSpec((1, 1, 1, w), lambda i, p, t: (i, p, 0, jnp.maximum(t * nsub - 1, 0))),
                  pl.BlockSpec(memory_space=pltpu.SMEM)],
        out_specs=[out_spec, out_spec],
        out_shape=[out_shape, out_shape],
        compiler_params=_cp(("arbitrary", "arbitrary", "arbitrary")),
        name=f"dil_attention_g{g}",
    )(view, view, view, view, view, pos_q, pos_k, pos_k, slopes2)
    return o.reshape(b, s, DIL_GW), lse.reshape(b, s, DIL_GW)


def _merge_kernel(ym_ref, o0_ref, o1_ref, o2_ref, l0_ref, l1_ref, l2_ref, ys_ref,
                  g0_ref, g1_ref, g2_ref, wm_ref, wd_ref, ws_ref, out_ref):
    l0, l1, l2 = l0_ref[0], l1_ref[0], l2_ref[0]
    mx = jnp.maximum(jnp.maximum(l0, l1), l2)
    e0, e1, e2 = jnp.exp(l0 - mx), jnp.exp(l1 - mx), jnp.exp(l2 - mx)
    y_dil = (e0 * o0_ref[0] + e1 * o1_ref[0] + e2 * o2_ref[0]) / (e0 + e1 + e2)

    def sig(ref):
        return 1.0 / (1.0 + jnp.exp(-ref[0].astype(F32)))

    merged = (sig(g0_ref) * jnp.dot(ym_ref[0], wm_ref[...], preferred_element_type=F32)
              + sig(g1_ref) * jnp.dot(y_dil.astype(BF16), wd_ref[...], preferred_element_type=F32)
              + sig(g2_ref) * jnp.dot(ys_ref[0], ws_ref[...], preferred_element_type=F32))
    out_ref[0] = merged.astype(out_ref.dtype)


def merge_branches(y_mla, dil_o, dil_lse, y_sb, gates, w_o_mla, w_o_dil, w_o_sb, tm=256):
    b, s, _ = y_mla.shape
    d = D_MODEL
    tm = min(tm, s)

    def row(width, cb=0):
        return pl.BlockSpec((1, tm, width), lambda i, j: (i, j, cb))

    def full(w):
        return pl.BlockSpec(w.shape, lambda i, j: (0, 0))

    return pl.pallas_call(
        _merge_kernel, grid=(b, s // tm),
        in_specs=[row(y_mla.shape[2])] + [row(DIL_GW)] * 6 + [row(SB_WIDTH),
                  row(d, 0), row(d, 1), row(d, 2), full(w_o_mla), full(w_o_dil), full(w_o_sb)],
        out_specs=row(d),
        out_shape=jax.ShapeDtypeStruct((b, s, d), BF16),
        compiler_params=_cp(("arbitrary", "arbitrary")),
        name="merge_branches",
    )(y_mla, *dil_o, *dil_lse, y_sb, gates, gates, gates, w_o_mla, w_o_dil, w_o_sb)


def _resid_proj_kernel(a_ref, w_ref, x_ref, g_ref, o_ref):
    y = jnp.dot(a_ref[0], w_ref[...], preferred_element_type=F32)
    o_ref[0] = x_ref[0] + g_ref[0] * y


def resid_project(a, w, x, mod3, gate_idx, tm=512, tn=1024):
    b, s, k = a.shape
    n = w.shape[1]
    tm = min(tm, s)
    return pl.pallas_call(
        _resid_proj_kernel, grid=(b, s // tm, n // tn),
        in_specs=[pl.BlockSpec((1, tm, k), lambda i, j, c: (i, j, 0)),
                  pl.BlockSpec((k, tn), lambda i, j, c: (0, c)),
                  pl.BlockSpec((1, tm, tn), lambda i, j, c: (i, j, c)),
                  pl.BlockSpec((1, 1, tn), lambda i, j, c: (i, 0, gate_idx * (n // tn) + c))],
        out_specs=pl.BlockSpec((1, tm, tn), lambda i, j, c: (i, j, c)),
        out_shape=jax.ShapeDtypeStruct((b, s, n), F32),
        compiler_params=_cp(("arbitrary", "arbitrary", "arbitrary")),
        name="resid_project",
    )(a, w, x, mod3)


def _router_kernel(lg_ref, b_ref, o_ref):
    lg = lg_ref[0]
    tm = lg.shape[1]
    sc = [1.0 / (1.0 + jnp.exp(-lg[e:e + 1, :])) for e in range(N_EXPERTS)]
    bi = [sc[e] + b_ref[e] for e in range(N_EXPERTS)]
    n = EXPERTS_PER_GROUP
    gscore = []
    for g in range(N_GROUPS):
        v = bi[g * n:(g + 1) * n]
        pair_max = None
        for a in range(n):
            for c in range(a + 1, n):
                pm = v[a] + v[c]
                pair_max = pm if pair_max is None else jnp.maximum(pair_max, pm)
        gscore.append(pair_max)
    best = gscore[0]
    gsel = jnp.zeros((1, tm), jnp.int32)
    for g in range(1, N_GROUPS):
        better = gscore[g] > best
        best = jnp.where(better, gscore[g], best)
        gsel = jnp.where(better, g, gsel)
    gb = [sum(jnp.where(gsel == g, bi[g * n + i], 0.0) for g in range(N_GROUPS)) for i in range(n)]
    gs = [sum(jnp.where(gsel == g, sc[g * n + i], 0.0) for g in range(N_GROUPS)) for i in range(n)]
    sel = []
    for i in range(n):
        beaten = jnp.zeros((1, tm), jnp.int32)
        for j in range(n):
            if j == i:
                continue
            wins = (gb[j] > gb[i]) if j > i else (gb[j] >= gb[i])
            beaten = beaten + wins.astype(jnp.int32)
        sel.append(beaten < 2)
    den = sum(jnp.where(sel[i], gs[i], 0.0) for i in range(n))
    for e in range(N_EXPERTS):
        g, i = divmod(e, n)
        o_ref[0, e:e + 1, :] = jnp.where(jnp.logical_and(gsel == g, sel[i]), gs[i] / den, 0.0)


def router_gates(logits_t, b_router, tm=1024):
    b, e, s = logits_t.shape
    tm = min(tm, s)
    return pl.pallas_call(
        _router_kernel, grid=(b, s // tm),
        in_specs=[pl.BlockSpec((1, e, tm), lambda i, j: (i, 0, j)),
                  pl.BlockSpec(memory_space=pltpu.SMEM)],
        out_specs=pl.BlockSpec((1, e, tm), lambda i, j: (i, 0, j)),
        out_shape=jax.ShapeDtypeStruct((b, e, s), F32),
        compiler_params=_cp(("arbitrary", "arbitrary")),
        name="router_gates",
    )(logits_t, b_router)


def _moe_kernel(a_ref, gt_ref, wg_ref, wu_ref, wd_ref, x_ref, gm_ref, o_ref, acc_ref):
    e = pl.program_id(2)

    @pl.when(e == 0)
    def _():
        acc_ref[...] = jnp.zeros(acc_ref.shape, F32)

    a = a_ref[0]
    hg = jnp.dot(a, wg_ref[0], preferred_element_type=F32)
    hu = jnp.dot(a, wu_ref[0], preferred_element_type=F32)
    onehot = lax.broadcasted_iota(jnp.int32, (1, N_EXPERTS), 1) == e
    gate = jnp.sum(jnp.where(onehot, gt_ref[0], 0.0), axis=-1, keepdims=True)
    h = (hg / (1.0 + jnp.exp(-hg))) * hu * gate
    acc_ref[...] += jnp.dot(h.astype(BF16), wd_ref[0], preferred_element_type=F32)

    @pl.when(e == N_EXPERTS - 1)
    def _():
        o_ref[0] = x_ref[0] + gm_ref[0] * acc_ref[...]


def moe_dense(a, gates, w_gate, w_up, w_down, x, mod3, gate_idx, tm=512):
    b, s, d = a.shape
    tm = min(tm, s)
    ne = w_gate.shape[0]
    return pl.pallas_call(
        _moe_kernel, grid=(b, s // tm, ne),
        in_specs=[pl.BlockSpec((1, tm, d), lambda i, j, e: (i, j, 0)),
                  pl.BlockSpec((1, tm, ne), lambda i, j, e: (i, j, 0)),
                  pl.BlockSpec((1, d, D_EXPERT), lambda i, j, e: (e, 0, 0)),
                  pl.BlockSpec((1, d, D_EXPERT), lambda i, j, e: (e, 0, 0)),
                  pl.BlockSpec((1, D_EXPERT, d), lambda i, j, e: (e, 0, 0)),
                  pl.BlockSpec((1, tm, d), lambda i, j, e: (i, j, 0)),
                  pl.BlockSpec((1, 1, d), lambda i, j, e: (i, 0, gate_idx))],
        out_specs=pl.BlockSpec((1, tm, d), lambda i, j, e: (i, j, 0)),
        out_shape=jax.ShapeDtypeStruct((b, s, d), F32),
        scratch_shapes=[pltpu.VMEM((tm, d), F32)],
        compiler_params=_cp(("arbitrary", "arbitrary", "arbitrary")),
        name="moe_dense",
    )(a, gates, w_gate, w_up, w_down, x, mod3)


def _final_norm_kernel(x_ref, g_ref, o_ref):
    x = x_ref[0]
    o_ref[0] = x * lax.rsqrt(jnp.mean(x * x, axis=-1, keepdims=True) + EPS) * g_ref[...]


def final_norm(x, g, tm=512):
    b, s, d = x.shape
    tm = min(tm, s)
    return pl.pallas_call(
        _final_norm_kernel, grid=(b, s // tm),
        in_specs=[pl.BlockSpec((1, tm, d), lambda i, j: (i, j, 0)),
                  pl.BlockSpec((1, d), lambda i, j: (0, 0))],
        out_specs=pl.BlockSpec((1, tm, d), lambda i, j: (i, j, 0)),
        out_shape=jax.ShapeDtypeStruct((b, s, d), F32),
        compiler_params=_cp(("arbitrary", "arbitrary")),
        name="final_norm",
    )(x, g.reshape(1, d))


def _rot_cols(w):
    half = w.shape[-1] // 2
    return jnp.concatenate([-w[..., half:], w[..., :half]], axis=-1)


def _prep_layer(w_in, w_uq, w_ukv, w_o_mla, w_o_dil, w_o_sb, w_out, w_gate, w_up, w_down):
    o1 = Q_LORA
    o2 = o1 + KV_LORA
    o3 = o2 + ROPE_DIM
    o4 = o3 + 3 * DIL_WIDTH
    o5 = o4 + 3 * SB_WIDTH
    w_kr = w_in[:, o2:o3]
    w_lat = jnp.concatenate([w_in[:, :o2], w_kr, _rot_cols(w_kr)], axis=1).astype(BF16)
    dil_scale = DIL_HEAD_DIM ** -0.5 * LOG2E
    def dil_cols(part, g):
        lo = o3 + part * DIL_WIDTH + g * DIL_GW
        return w_in[:, lo:lo + DIL_GW]

    w_dil = [jnp.concatenate([dil_cols(0, g) * dil_scale, dil_cols(1, g), dil_cols(2, g)], axis=1).astype(BF16)
             for g in range(DIL_GROUPS)]
    sb_scale = -(SB_HEAD_DIM ** -0.5) * LOG2E
    w_sb = jnp.concatenate([w_in[:, o4:o4 + SB_WIDTH] * sb_scale, w_in[:, o4 + SB_WIDTH:o5]],
                           axis=1).astype(BF16)
    w_gl = w_in[:, o5:].astype(BF16)
    q_scale = MLA_QK ** -0.5 * LOG2E
    wq = (w_uq * q_scale).reshape(Q_LORA, MLA_HEADS, MLA_QK).transpose(1, 0, 2)
    wq_h = jnp.concatenate([wq, _rot_cols(wq[..., NOPE_DIM:])], axis=-1).astype(BF16)
    wkv_h = w_ukv.reshape(KV_LORA, MLA_HEADS, NOPE_DIM + MLA_V_DIM).transpose(1, 0, 2).astype(BF16)
    return dict(w_lat=w_lat, w_dil=w_dil, w_sb=w_sb, w_gl=w_gl, wq_h=wq_h, wkv_h=wkv_h,
                w_o_mla=w_o_mla.astype(BF16), w_o_dil=w_o_dil.astype(BF16), w_o_sb=w_o_sb.astype(BF16),
                w_out=w_out.astype(BF16), w_gate=w_gate.astype(BF16), w_up=w_up.astype(BF16),
                w_down=w_down.astype(BF16))


def kernel(x, c, positions, w_ada, b_ada, g_mix, g_moe, w_in, g_q, w_uq, g_kv, w_ukv, w_o_mla, w_o_dil,
           w_o_sb, w_out, w_router, b_router, w_gate, w_up, w_down, g_final):
    b, s, d = x.shape
    depth = w_ada.shape[0]
    pos_f = positions.astype(F32)
    cos, sin = rope_tables(pos_f.reshape(b, s, 1))
    mod = ada_mod(c, w_ada, b_ada)
    w_router_t = w_router.T
    for l in range(depth):
        p = _prep_layer(w_in[l], w_uq[l], w_ukv[l], w_o_mla[l], w_o_dil[l], w_o_sb[l], w_out[l],
                        w_gate[l], w_up[l], w_down[l])
        mod3 = mod[l].reshape(b, 1, 6 * d)

        a = norm_mod(x, g_mix[l], mod3, sc_idx=1, sh_idx=0)
        lat = project(a, p["w_lat"], tn=p["w_lat"].shape[1])
        dil = [project_by_residue(a, p["w_dil"][g], DIL_RATES[g]) for g in range(DIL_GROUPS)]
        sbp = project(a, p["w_sb"])
        gl = project(a, p["w_gl"])

        q, k, v = mla_project(lat, g_q[l], p["wq_h"], g_kv[l], p["wkv_h"], cos, sin)
        y_mla = mla_attention(q, k, v)
        dil_out = [dil_group_attention(dil[g], pos_f, g) for g in range(DIL_GROUPS)]
        y_sb = sb_attention(sbp)
        merged = merge_branches(y_mla, [o for o, _ in dil_out], [e for _, e in dil_out], y_sb, gl,
                                p["w_o_mla"], p["w_o_dil"], p["w_o_sb"])
        x = resid_project(merged, p["w_out"], x, mod3, gate_idx=2)

        a2, logits_t = norm_mod(x, g_moe[l], mod3, sc_idx=4, sh_idx=3, w_router_t=w_router_t)
        gates_t = router_gates(logits_t, b_router)
        gates = gates_t.transpose(0, 2, 1)
        x = moe_dense(a2, gates, p["w_gate"], p["w_up"], p["w_down"], x, mod3, gate_idx=5)
    return final_norm(x, g_final)
```

```python
import functools
import math

import jax
import jax.numpy as jnp
from jax import lax
from jax.experimental import pallas as pl
from jax.experimental.pallas import tpu as pltpu

D_MODEL = 2048
EPS = 1e-6
MLA_HEADS = 8
Q_LORA = 512
KV_LORA = 512
NOPE_DIM = 128
ROPE_DIM = 64
MLA_V_DIM = 128
ROPE_THETA = 10000.0
MLA_QK = NOPE_DIM + ROPE_DIM
DIL_WINDOWS = (128, 512, 2048)
DIL_RATES = (1, 4, 16)
DIL_GROUPS = 3
DIL_HEADS = 4
DIL_HEAD_DIM = 128
DIL_SPAN = DIL_WINDOWS[0] // DIL_RATES[0]
DIL_GW = DIL_HEADS * DIL_HEAD_DIM
DIL_WIDTH = DIL_GROUPS * DIL_GW
ALIBI_MAX_BIAS = 8.0
SB_HEADS = 8
SB_HEAD_DIM = 128
SB_WIDTH = SB_HEADS * SB_HEAD_DIM
N_EXPERTS = 16
N_GROUPS = 4
EXPERTS_PER_GROUP = N_EXPERTS // N_GROUPS
D_EXPERT = 512

LOG2E = math.log2(math.e)
LN2 = math.log(2.0)
NEG_BIG = -1e30

V7X_LANES = 128
V7X_VMEM_BYTES = 64 * 1024 * 1024
VMEM_LIMIT = 56 * 1024 * 1024

F32 = jnp.float32
BF16 = jnp.bfloat16


def _cp(sem, vmem=VMEM_LIMIT):
    return pltpu.CompilerParams(dimension_semantics=sem, vmem_limit_bytes=vmem)


def _lane_tile(x, n):
    return x if n == 1 else jnp.concatenate([x] * n, axis=1)


def _nt_dot(a, b):
    return lax.dot_general(a, b, (((1,), (1,)), ((), ())), preferred_element_type=F32)


def _ada_kernel(c_ref, w_ref, b_ref, o_ref):
    c = c_ref[...]
    ca = c * (1.0 / (1.0 + jnp.exp(-c)))
    o_ref[0] = jnp.dot(ca, w_ref[0], preferred_element_type=F32,
                       precision=lax.Precision.HIGHEST) + b_ref[0]


def ada_mod(c, w_ada, b_ada, tn=1024):
    depth, d, n = w_ada.shape
    b = c.shape[0]
    return pl.pallas_call(
        _ada_kernel,
        grid=(depth, n // tn),
        in_specs=[pl.BlockSpec((b, d), lambda l, j: (0, 0)),
                  pl.BlockSpec((1, d, tn), lambda l, j: (l, 0, j)),
                  pl.BlockSpec((1, 1, tn), lambda l, j: (l, 0, j))],
        out_specs=pl.BlockSpec((1, b, tn), lambda l, j: (l, 0, j)),
        out_shape=jax.ShapeDtypeStruct((depth, b, n), F32),
        compiler_params=_cp(("arbitrary", "arbitrary")),
        name="ada_mod",
    )(c, w_ada, b_ada.reshape(depth, 1, n))


def _rope_table_kernel(pos_ref, inv_ref, cos_ref, sin_ref):
    ang = pos_ref[0] * inv_ref[...]
    cos_ref[0] = jnp.cos(ang)
    sin_ref[0] = jnp.sin(ang)


def rope_tables(pos_col, tm=512):
    b, s, _ = pos_col.shape
    tm = min(tm, s)
    half = ROPE_DIM // 2
    inv = ROPE_THETA ** (-jnp.arange(half, dtype=F32) / half)
    inv2 = jnp.concatenate([inv, inv]).reshape(1, ROPE_DIM)
    shp = jax.ShapeDtypeStruct((b, s, ROPE_DIM), F32)
    return pl.pallas_call(
        _rope_table_kernel,
        grid=(b, s // tm),
        in_specs=[pl.BlockSpec((1, tm, 1), lambda i, j: (i, j, 0)),
                  pl.BlockSpec((1, ROPE_DIM), lambda i, j: (0, 0))],
        out_specs=[pl.BlockSpec((1, tm, ROPE_DIM), lambda i, j: (i, j, 0))] * 2,
        out_shape=[shp, shp],
        compiler_params=_cp(("arbitrary", "arbitrary")),
        name="rope_tables",
    )(pos_col, inv2)


def _norm_mod(x, g, scale, shift):
    y = x * lax.rsqrt(jnp.mean(x * x, axis=-1, keepdims=True) + EPS)
    return (y * g) * (1.0 + scale) + shift


def _norm_kernel(x_ref, g_ref, sc_ref, sh_ref, o_ref):
    o_ref[0] = _norm_mod(x_ref[0], g_ref[...], sc_ref[0], sh_ref[0]).astype(o_ref.dtype)


def _norm_router_kernel(x_ref, g_ref, sc_ref, sh_ref, wr_ref, o_ref, lg_ref):
    a = _norm_mod(x_ref[0], g_ref[...], sc_ref[0], sh_ref[0])
    o_ref[0] = a.astype(o_ref.dtype)
    lg_ref[0] = lax.dot_general(wr_ref[...], a, (((1,), (1,)), ((), ())),
                                preferred_element_type=F32, precision=lax.Precision.HIGHEST)


def norm_mod(x, g, mod3, sc_idx, sh_idx, w_router_t=None, tm=512):
    b, s, d = x.shape
    tm = min(tm, s)
    in_specs = [pl.BlockSpec((1, tm, d), lambda i, j: (i, j, 0)),
                pl.BlockSpec((1, d), lambda i, j: (0, 0)),
                pl.BlockSpec((1, 1, d), lambda i, j: (i, 0, sc_idx)),
                pl.BlockSpec((1, 1, d), lambda i, j: (i, 0, sh_idx))]
    o_spec = pl.BlockSpec((1, tm, d), lambda i, j: (i, j, 0))
    o_shape = jax.ShapeDtypeStruct((b, s, d), BF16 if w_router_t is None else F32)
    if w_router_t is None:
        return pl.pallas_call(
            _norm_kernel, grid=(b, s // tm), in_specs=in_specs, out_specs=o_spec, out_shape=o_shape,
            compiler_params=_cp(("arbitrary", "arbitrary")), name="norm_mod",
        )(x, g.reshape(1, d), mod3, mod3)
    e = w_router_t.shape[0]
    return pl.pallas_call(
        _norm_router_kernel, grid=(b, s // tm),
        in_specs=in_specs + [pl.BlockSpec((e, d), lambda i, j: (0, 0))],
        out_specs=[o_spec, pl.BlockSpec((1, e, tm), lambda i, j: (i, 0, j))],
        out_shape=[o_shape, jax.ShapeDtypeStruct((b, e, s), F32)],
        compiler_params=_cp(("arbitrary", "arbitrary")), name="norm_router",
    )(x, g.reshape(1, d), mod3, mod3, w_router_t)


def _matmul_kernel(a_ref, w_ref, o_ref):
    o_ref[0] = jnp.dot(a_ref[0], w_ref[...], preferred_element_type=F32).astype(o_ref.dtype)


def project(a, w, tm=1024, tn=512, out_dtype=BF16):
    b, s, k = a.shape
    n = w.shape[1]
    tm = min(tm, s)
    tn = min(tn, n)
    assert n % tn == 0 and s % tm == 0
    return pl.pallas_call(
        _matmul_kernel,
        grid=(b, s // tm, n // tn),
        in_specs=[pl.BlockSpec((1, tm, k), lambda i, j, c: (i, j, 0)),
                  pl.BlockSpec((k, tn), lambda i, j, c: (0, c))],
        out_specs=pl.BlockSpec((1, tm, tn), lambda i, j, c: (i, j, c)),
        out_shape=jax.ShapeDtypeStruct((b, s, n), out_dtype),
        compiler_params=_cp(("arbitrary", "arbitrary", "arbitrary")),
        name="project",
    )(a, w)


def _matmul_residue_kernel(a_ref, w_ref, o_ref, res_scr, *, rate):
    res = jnp.dot(a_ref[0], w_ref[...], preferred_element_type=F32)
    slabs, rows, lanes = res_scr.shape
    for c in range(slabs):
        res_scr[c] = res[:, c * lanes:(c + 1) * lanes]
    n = rows // rate
    for rho in range(rate):
        for c in range(slabs):
            o_ref[0, rho, :, c * lanes:(c + 1) * lanes] = (
                res_scr[c, pl.ds(rho, n, stride=rate), :].astype(o_ref.dtype))


def project_by_residue(a, w, rate, tm=1024, tn=512):
    b, s, k = a.shape
    n = w.shape[1]
    if rate == 1:
        return project(a, w, tm=tm, tn=tn).reshape(b, 1, s, n)
    tm = min(tm, s)
    assert n % tn == 0 and s % tm == 0 and tm % (rate * 16) == 0
    return pl.pallas_call(
        functools.partial(_matmul_residue_kernel, rate=rate),
        grid=(b, s // tm, n // tn),
        in_specs=[pl.BlockSpec((1, tm, k), lambda i, j, c: (i, j, 0)),
                  pl.BlockSpec((k, tn), lambda i, j, c: (0, c))],
        out_specs=pl.BlockSpec((1, rate, tm // rate, tn), lambda i, j, c: (i, 0, j, c)),
        out_shape=jax.ShapeDtypeStruct((b, rate, s // rate, n), BF16),
        scratch_shapes=[pltpu.VMEM((tn // V7X_LANES, tm, V7X_LANES), F32)],
        compiler_params=_cp(("arbitrary", "arbitrary", "arbitrary")),
        name=f"project_by_residue_{rate}",
    )(a, w)


def _latent_norm(c, g):
    c = c.astype(F32)
    return (c * lax.rsqrt(jnp.mean(c * c, axis=-1, keepdims=True) + EPS) * g).astype(BF16)


def _mla_proj_kernel(cq_ref, ckv_ref, kr_ref, gq_ref, gkv_ref, wq_ref, wkv_ref, cos_ref, sin_ref,
                     q_ref, k_ref, v_ref):
    cos, sin = cos_ref[0], sin_ref[0]
    rq = _latent_norm(cq_ref[0], gq_ref[...])
    rkv = _latent_norm(ckv_ref[0], gkv_ref[...])
    kr = kr_ref[0].astype(F32)
    k_rope = (kr[:, :ROPE_DIM] * cos + kr[:, ROPE_DIM:] * sin).astype(k_ref.dtype)
    ones = jnp.ones((rq.shape[0], MLA_V_DIM), v_ref.dtype)
    for h in range(MLA_HEADS):
        r = jnp.dot(rq, wq_ref[h], preferred_element_type=F32)
        q_ref[0, h, :, :NOPE_DIM] = r[:, :NOPE_DIM].astype(q_ref.dtype)
        roped = r[:, NOPE_DIM:NOPE_DIM + ROPE_DIM] * cos + r[:, NOPE_DIM + ROPE_DIM:] * sin
        q_ref[0, h, :, NOPE_DIM:] = roped.astype(q_ref.dtype)
        r = jnp.dot(rkv, wkv_ref[h], preferred_element_type=F32)
        k_ref[0, h, :, :NOPE_DIM] = r[:, :NOPE_DIM].astype(k_ref.dtype)
        k_ref[0, h, :, NOPE_DIM:] = k_rope
        v_ref[0, h, :, :MLA_V_DIM] = r[:, NOPE_DIM:].astype(v_ref.dtype)
        v_ref[0, h, :, MLA_V_DIM:] = ones


def mla_project(lat, g_q, wq_h, g_kv, wkv_h, cos, sin, tm=512):
    b, s, _ = lat.shape
    tm = min(tm, s)
    h = MLA_HEADS
    kr_blk = (Q_LORA + KV_LORA) // (2 * ROPE_DIM)
    tab = pl.BlockSpec((1, tm, ROPE_DIM), lambda i, j: (i, j, 0))

    def const(shape):
        return pl.BlockSpec(shape, lambda i, j: (0,) * len(shape))

    def head_major(width):
        return pl.BlockSpec((1, h, tm, width), lambda i, j: (i, 0, j, 0))

    return pl.pallas_call(
        _mla_proj_kernel, grid=(b, s // tm),
        in_specs=[pl.BlockSpec((1, tm, Q_LORA), lambda i, j: (i, j, 0)),
                  pl.BlockSpec((1, tm, KV_LORA), lambda i, j: (i, j, 1)),
                  pl.BlockSpec((1, tm, 2 * ROPE_DIM), lambda i, j: (i, j, kr_blk)),
                  const((1, Q_LORA)), const((1, KV_LORA)),
                  const((h, Q_LORA, 256)), const((h, KV_LORA, 256)), tab, tab],
        out_specs=[head_major(MLA_QK), head_major(MLA_QK), head_major(2 * MLA_V_DIM)],
        out_shape=[jax.ShapeDtypeStruct((b, h, s, MLA_QK), BF16),
                   jax.ShapeDtypeStruct((b, h, s, MLA_QK), BF16),
                   jax.ShapeDtypeStruct((b, h, s, 2 * MLA_V_DIM), BF16)],
        compiler_params=_cp(("arbitrary", "arbitrary")), name="mla_project",
    )(lat, lat, lat, g_q.reshape(1, Q_LORA), g_kv.reshape(1, KV_LORA), wq_h, wkv_h, cos, sin)


def _mla_attn_kernel(q_ref, k_ref, v_ref, o_ref, s0_scr, s1_scr, m_scr, acc_scr, *, tq, tk):
    qi = pl.program_id(2)
    m_scr[...] = jnp.full(m_scr.shape, NEG_BIG, F32)
    acc_scr[...] = jnp.zeros(acc_scr.shape, F32)

    def scores(c, dst, row0=0):
        start = pl.multiple_of(c * tk, tk)
        dst[row0:, :] = _nt_dot(q_ref[0, 0, row0:, :], k_ref[0, 0, pl.ds(start, tk), :])

    def absorb(src, c, col_off=None, row0=0):
        start = pl.multiple_of(c * tk, tk)
        s = src[row0:, :]
        n = tq - row0
        if col_off is not None:
            row = row0 + lax.broadcasted_iota(jnp.int32, (n, tk), 0)
            col = col_off + lax.broadcasted_iota(jnp.int32, (n, tk), 1)
            s = jnp.where(col <= row, s, NEG_BIG)
        m_prev = m_scr[row0:, :]
        m_new = jnp.maximum(m_prev, jnp.max(s, axis=-1, keepdims=True))
        p = jnp.exp2(s - _lane_tile(m_new, tk // V7X_LANES))
        alpha = jnp.exp2(m_prev - m_new)
        pv = jnp.dot(p.astype(BF16), v_ref[0, 0, pl.ds(start, tk), :], preferred_element_type=F32)
        acc_scr[row0:, :] = _lane_tile(alpha, 2) * acc_scr[row0:, :] + pv
        m_scr[row0:, :] = m_new

    scores(0, s0_scr)

    def body(i, carry):
        scores(2 * i + 1, s1_scr)
        absorb(s0_scr, 2 * i)
        scores(2 * i + 2, s0_scr)
        absorb(s1_scr, 2 * i + 1)
        return carry

    lax.fori_loop(0, qi, body, 0)
    scores(2 * qi + 1, s1_scr, row0=tk)
    absorb(s0_scr, 2 * qi, col_off=0)
    absorb(s1_scr, 2 * qi + 1, col_off=tk, row0=tk)
    acc = acc_scr[...]
    o_ref[0] = (acc[:, :MLA_V_DIM] / acc[:, MLA_V_DIM:]).astype(o_ref.dtype)


def mla_attention(q, k, v, tq=1024):
    b, h, s, _ = q.shape
    tq = min(tq, s)
    tk = tq // 2
    return pl.pallas_call(
        functools.partial(_mla_attn_kernel, tq=tq, tk=tk),
        grid=(b, h, s // tq),
        in_specs=[pl.BlockSpec((1, 1, tq, MLA_QK), lambda i, j, t: (i, j, t, 0)),
                  pl.BlockSpec((1, 1, s, MLA_QK), lambda i, j, t: (i, j, 0, 0)),
                  pl.BlockSpec((1, 1, s, 2 * MLA_V_DIM), lambda i, j, t: (i, j, 0, 0))],
        out_specs=pl.BlockSpec((1, tq, MLA_V_DIM), lambda i, j, t: (i, t, j)),
        out_shape=jax.ShapeDtypeStruct((b, s, h * MLA_V_DIM), BF16),
        scratch_shapes=[pltpu.VMEM((tq, tk), F32), pltpu.VMEM((tq, tk), F32),
                        pltpu.VMEM((tq, V7X_LANES), F32), pltpu.VMEM((tq, 2 * MLA_V_DIM), F32)],
        compiler_params=_cp(("arbitrary", "arbitrary", "arbitrary")),
        name="mla_attention",
    )(q, k, v)


SB_SKIP_LOG2 = 150.0


def _sb_attn_kernel(q_ref, k_ref, v_ref, tri_ref, o_ref, c_scr, acc_scr, *, tq, hp):
    qi = pl.program_id(2)
    c_scr[...] = jnp.zeros(c_scr.shape, F32)
    acc_scr[...] = jnp.zeros(acc_scr.shape, F32)
    reps = tq // V7X_LANES

    def chunk(j, diagonal):
        start = pl.multiple_of(j * tq, tq)
        heads = range(hp)
        cols = [slice(hh * SB_HEAD_DIM, (hh + 1) * SB_HEAD_DIM) for hh in heads]
        if diagonal:
            before = (lax.broadcasted_iota(jnp.int32, (tq, tq), 1)
                      < lax.broadcasted_iota(jnp.int32, (tq, tq), 0))
        zn = [_nt_dot(q_ref[0, :, cols[hh]], k_ref[0, pl.ds(start, tq), cols[hh]]) for hh in heads]
        lk = [jnp.minimum(z, 0.0) - jnp.log2(1.0 + jnp.exp2(-jnp.abs(z))) for z in zn]
        if diagonal:
            lk = [jnp.where(before, x, 0.0) for x in lk]
        hi = [x.astype(BF16) for x in lk]
        lo = [(x - h.astype(F32)).astype(BF16) for x, h in zip(lk, hi)]
        suffix = [jnp.dot(jnp.concatenate([h, l], axis=1), tri_ref[...], preferred_element_type=F32)
                  for h, l in zip(hi, lo)]
        c = [c_scr[hh] for hh in heads]
        a = [jnp.exp2(sf - z + _lane_tile(cc, reps)) for sf, z, cc in zip(suffix, zn, c)]
        if diagonal:
            a = [jnp.where(before, x, 0.0) for x in a]
        for hh in heads:
            acc_scr[:, cols[hh]] += jnp.dot(a[hh].astype(BF16), v_ref[0, pl.ds(start, tq), cols[hh]],
                                            preferred_element_type=F32)
            c_scr[hh] = c[hh] + jnp.sum(lk[hh], axis=-1, keepdims=True)

    chunk(qi, True)

    def cond(state):
        j, go = state
        return jnp.logical_and(j >= 0, go > 0)

    def body(state):
        j, _ = state
        chunk(j, False)
        go = (jnp.max(c_scr[...]) > -SB_SKIP_LOG2).astype(jnp.int32)
        return j - 1, go

    lax.while_loop(cond, body, (qi - 1, jnp.int32(1)))
    o_ref[0] = acc_scr[...].astype(o_ref.dtype)


def sb_attention(qkv, tq=256, hp=4):
    b, s, _ = qkv.shape
    tq = min(tq, s)
    hw = hp * SB_HEAD_DIM
    nhb = SB_HEADS // hp
    tri = (lax.broadcasted_iota(jnp.int32, (tq, tq), 0)
           >= lax.broadcasted_iota(jnp.int32, (tq, tq), 1)).astype(BF16)
    tri2 = jnp.concatenate([tri, tri], axis=0)
    return pl.pallas_call(
        functools.partial(_sb_attn_kernel, tq=tq, hp=hp),
        grid=(b, nhb, s // tq),
        in_specs=[pl.BlockSpec((1, tq, hw), lambda i, j, t: (i, t, j)),
                  pl.BlockSpec((1, s, hw), lambda i, j, t: (i, 0, nhb + j)),
                  pl.BlockSpec((1, s, hw), lambda i, j, t: (i, 0, 2 * nhb + j)),
                  pl.BlockSpec((2 * tq, tq), lambda i, j, t: (0, 0))],
        out_specs=pl.BlockSpec((1, tq, hw), lambda i, j, t: (i, t, j)),
        out_shape=jax.ShapeDtypeStruct((b, s, SB_WIDTH), BF16),
        scratch_shapes=[pltpu.VMEM((hp, tq, V7X_LANES), F32), pltpu.VMEM((tq, hw), F32)],
        compiler_params=_cp(("arbitrary", "arbitrary", "arbitrary")),
        name="sb_attention",
    )(qkv, qkv, qkv, tri2)


def _dil_kernel(q_ref, kc_ref, vc_ref, kp_ref, vp_ref, pq_ref, pkc_ref, pkp_ref, sl_ref,
                o_ref, lse_ref, *, tq):
    ti = pl.program_id(2)
    w = DIL_SPAN
    a_idx = lax.broadcasted_iota(jnp.int32, (w, w), 0)
    c_idx = lax.broadcasted_iota(jnp.int32, (w, w), 1)
    own_ok = c_idx <= a_idx
    prev_tri = c_idx >= a_idx
    for sb in range(tq // w):
        rows = slice(sb * w, (sb + 1) * w)
        pq = pq_ref[0, 0, rows, :]
        pk_own = pkc_ref[0, 0, :, rows]
        if sb == 0:
            pk_prev = pkp_ref[0, 0]
            prev_ok = jnp.logical_and(prev_tri, ti > 0)
        else:
            pk_prev = pkc_ref[0, 0, :, (sb - 1) * w: sb * w]
            prev_ok = prev_tri
        dist_own = jnp.abs(pq - pk_own)
        dist_prev = jnp.abs(pq - pk_prev)
        heads = range(DIL_HEADS)
        cols = [slice(h * DIL_HEAD_DIM, (h + 1) * DIL_HEAD_DIM) for h in heads]
        prev_rows = slice((sb - 1) * w, sb * w)

        def k_prev(h):
            return kp_ref[0, :, cols[h]] if sb == 0 else kc_ref[0, prev_rows, cols[h]]

        def v_prev(h):
            return vp_ref[0, :, cols[h]] if sb == 0 else vc_ref[0, prev_rows, cols[h]]

        z_own = [_nt_dot(q_ref[0, rows, cols[h]], kc_ref[0, rows, cols[h]]) for h in heads]
        z_prev = [_nt_dot(q_ref[0, rows, cols[h]], k_prev(h)) for h in heads]
        s_own = [jnp.where(own_ok, z_own[h] - sl_ref[h] * dist_own, NEG_BIG) for h in heads]
        s_prev = [jnp.where(prev_ok, z_prev[h] - sl_ref[h] * dist_prev, NEG_BIG) for h in heads]
        m = [jnp.maximum(jnp.max(s_own[h], axis=-1, keepdims=True), jnp.max(s_prev[h], axis=-1, keepdims=True))
             for h in heads]
        p_own = [jnp.exp2(s_own[h] - m[h]) for h in heads]
        p_prev = [jnp.exp2(s_prev[h] - m[h]) for h in heads]
        den = [jnp.sum(p_own[h], axis=-1, keepdims=True) + jnp.sum(p_prev[h], axis=-1, keepdims=True)
               for h in heads]
        for h in heads:
            o = (jnp.dot(p_own[h].astype(BF16), vc_ref[0, rows, cols[h]], preferred_element_type=F32)
                 + jnp.dot(p_prev[h].astype(BF16), v_prev(h), preferred_element_type=F32)) / den[h]
            o_ref[0, rows, cols[h]] = o
            lse = (m[h] + jnp.log2(den[h])) * LN2
            lse_ref[0, rows, cols[h]] = jnp.broadcast_to(lse, (w, DIL_HEAD_DIM))


def dil_group_attention(dil_g, pos_f, g, tq=512):
    b, r, sr, c = dil_g.shape
    s = sr * r
    assert r == DIL_RATES[g]
    tq = min(tq, sr)
    w = DIL_SPAN
    assert sr % tq == 0 and tq % w == 0
    nsub = tq // w
    view = dil_g.reshape(b * r, sr, c)
    pos_r = pos_f.reshape(b, sr, r).transpose(0, 2, 1)
    pos_q = pos_r[..., None]
    pos_k = pos_r[:, :, None, :]
    n_h = DIL_GROUPS * DIL_HEADS
    slopes = 2.0 ** (-ALIBI_MAX_BIAS * jnp.arange(1, n_h + 1, dtype=F32) / n_h)
    slopes2 = (slopes * LOG2E)[g * DIL_HEADS:(g + 1) * DIL_HEADS]

    def cur(part):
        return pl.BlockSpec((1, tq, DIL_GW), lambda i, p, t: (i * r + p, t, part))

    def prev(part):
        return pl.BlockSpec((1, w, DIL_GW), lambda i, p, t: (i * r + p, jnp.maximum(t * nsub - 1, 0), part))

    out_spec = pl.BlockSpec((1, tq, DIL_GW), lambda i, p, t: (i, t, p))
    out_shape = jax.ShapeDtypeStruct((b, sr, r * DIL_GW), F32)
    o, lse = pl.pallas_call(
        functools.partial(_dil_kernel, tq=tq),
        grid=(b, r, sr // tq),
        in_specs=[cur(0), cur(1), cur(2), prev(1), prev(2),
                  pl.BlockSpec((1, 1, tq, 1), lambda i, p, t: (i, p, t, 0)),
                  pl.BlockSpec((1, 1, 1, tq), lambda i, p, t: (i, p, 0, t)),
                  pl.BlockSpec((1, 1, 1, w), lambda i, p, t: (i, p, 0, jnp.maximum(t * nsub - 1, 0))),
                  pl.BlockSpec(memory_space=pltpu.SMEM)],
        out_specs=[out_spec, out_spec],
        out_shape=[out_shape, out_shape],
        compiler_params=_cp(("arbitrary", "arbitrary", "arbitrary")),
        name=f"dil_attention_g{g}",
    )(view, view, view, view, view, pos_q, pos_k, pos_k, slopes2)
    return o.reshape(b, s, DIL_GW), lse.reshape(b, s, DIL_GW)


def _merge_kernel(ym_ref, o0_ref, o1_ref, o2_ref, l0_ref, l1_ref, l2_ref, ys_ref,
                  g0_ref, g1_ref, g2_ref, wm_ref, wd_ref, ws_ref, out_ref):
    l0, l1, l2 = l0_ref[0], l1_ref[0], l2_ref[0]
    mx = jnp.maximum(jnp.maximum(l0, l1), l2)
    e0, e1, e2 = jnp.exp(l0 - mx), jnp.exp(l1 - mx), jnp.exp(l2 - mx)
    y_dil = (e0 * o0_ref[0] + e1 * o1_ref[0] + e2 * o2_ref[0]) / (e0 + e1 + e2)

    def sig(ref):
        return 1.0 / (1.0 + jnp.exp(-ref[0].astype(F32)))

    merged = (sig(g0_ref) * jnp.dot(ym_ref[0], wm_ref[...], preferred_element_type=F32)
              + sig(g1_ref) * jnp.dot(y_dil.astype(BF16), wd_ref[...], preferred_element_type=F32)
              + sig(g2_ref) * jnp.dot(ys_ref[0], ws_ref[...], preferred_element_type=F32))
    out_ref[0] = merged.astype(out_ref.dtype)


def merge_branches(y_mla, dil_o, dil_lse, y_sb, gates, w_o_mla, w_o_dil, w_o_sb, tm=256):
    b, s, _ = y_mla.shape
    d = D_MODEL
    tm = min(tm, s)

    def row(width, cb=0):
        return pl.BlockSpec((1, tm, width), lambda i, j: (i, j, cb))

    def full(w):
        return pl.BlockSpec(w.shape, lambda i, j: (0, 0))

    return pl.pallas_call(
        _merge_kernel, grid=(b, s // tm),
        in_specs=[row(y_mla.shape[2])] + [row(DIL_GW)] * 6 + [row(SB_WIDTH),
                  row(d, 0), row(d, 1), row(d, 2), full(w_o_mla), full(w_o_dil), full(w_o_sb)],
        out_specs=row(d),
        out_shape=jax.ShapeDtypeStruct((b, s, d), BF16),
        compiler_params=_cp(("arbitrary", "arbitrary")),
        name="merge_branches",
    )(y_mla, *dil_o, *dil_lse, y_sb, gates, gates, gates, w_o_mla, w_o_dil, w_o_sb)


def _resid_proj_kernel(a_ref, w_ref, x_ref, g_ref, o_ref):
    y = jnp.dot(a_ref[0], w_ref[...], preferred_element_type=F32)
    o_ref[0] = x_ref[0] + g_ref[0] * y


def resid_project(a, w, x, mod3, gate_idx, tm=512, tn=1024):
    b, s, k = a.shape
    n = w.shape[1]
    tm = min(tm, s)
    return pl.pallas_call(
        _resid_proj_kernel, grid=(b, s // tm, n // tn),
        in_specs=[pl.BlockSpec((1, tm, k), lambda i, j, c: (i, j, 0)),
                  pl.BlockSpec((k, tn), lambda i, j, c: (0, c)),
                  pl.BlockSpec((1, tm, tn), lambda i, j, c: (i, j, c)),
                  pl.BlockSpec((1, 1, tn), lambda i, j, c: (i, 0, gate_idx * (n // tn) + c))],
        out_specs=pl.BlockSpec((1, tm, tn), lambda i, j, c: (i, j, c)),
        out_shape=jax.ShapeDtypeStruct((b, s, n), F32),
        compiler_params=_cp(("arbitrary", "arbitrary", "arbitrary")),
        name="resid_project",
    )(a, w, x, mod3)


def _router_kernel(lg_ref, b_ref, e0_ref, e1_ref, w0_ref, w1_ref):
    lg = lg_ref[0]
    tm = lg.shape[1]
    sc = [1.0 / (1.0 + jnp.exp(-lg[e:e + 1, :])) for e in range(N_EXPERTS)]
    bi = [sc[e] + b_ref[e] for e in range(N_EXPERTS)]
    n = EXPERTS_PER_GROUP
    gscore = []
    for g in range(N_GROUPS):
        v = bi[g * n:(g + 1) * n]
        pair_max = None
        for a in range(n):
            for c in range(a + 1, n):
                pm = v[a] + v[c]
                pair_max = pm if pair_max is None else jnp.maximum(pair_max, pm)
        gscore.append(pair_max)
    best = gscore[0]
    gsel = jnp.zeros((1, tm), jnp.int32)
    for g in range(1, N_GROUPS):
        better = gscore[g] > best
        best = jnp.where(better, gscore[g], best)
        gsel = jnp.where(better, g, gsel)
    gb = [sum(jnp.where(gsel == g, bi[g * n + i], 0.0) for g in range(N_GROUPS)) for i in range(n)]
    gs = [sum(jnp.where(gsel == g, sc[g * n + i], 0.0) for g in range(N_GROUPS)) for i in range(n)]
    sel = []
    for i in range(n):
        beaten = jnp.zeros((1, tm), jnp.int32)
        for j in range(n):
            if j == i:
                continue
            wins = (gb[j] > gb[i]) if j > i else (gb[j] >= gb[i])
            beaten = beaten + wins.astype(jnp.int32)
        sel.append(beaten < 2)
    den = sum(jnp.where(sel[i], gs[i], 0.0) for i in range(n))
    i0 = jnp.where(sel[0], 0, jnp.where(sel[1], 1, 2))
    i1 = jnp.where(sel[3], 3, jnp.where(sel[2], 2, 1))
    e0_ref[0] = gsel * n + i0
    e1_ref[0] = gsel * n + i1
    w0_ref[0] = sum(jnp.where(i0 == i, gs[i], 0.0) for i in range(n)) / den
    w1_ref[0] = sum(jnp.where(i1 == i, gs[i], 0.0) for i in range(n)) / den


def router_top2(logits_t, b_router, tm=1024):
    b, e, s = logits_t.shape
    tm = min(tm, s)
    row = pl.BlockSpec((1, 1, tm), lambda i, j: (i, 0, j))
    ishape = jax.ShapeDtypeStruct((b, 1, s), jnp.int32)
    fshape = jax.ShapeDtypeStruct((b, 1, s), F32)
    return pl.pallas_call(
        _router_kernel, grid=(b, s // tm),
        in_specs=[pl.BlockSpec((1, e, tm), lambda i, j: (i, 0, j)),
                  pl.BlockSpec(memory_space=pltpu.SMEM)],
        out_specs=[row, row, row, row],
        out_shape=[ishape, ishape, fshape, fshape],
        compiler_params=_cp(("arbitrary", "arbitrary")),
        name="router_top2",
    )(logits_t, b_router)


def _rank_kernel(e0_ref, e1_ref, tri_ref, r0_ref, r1_ref, cnt_ref, carry_scr):
    @pl.when(jnp.logical_and(pl.program_id(0) == 0, pl.program_id(1) == 0))
    def _():
        carry_scr[...] = jnp.zeros(carry_scr.shape, F32)

    e0, e1 = e0_ref[0], e1_ref[0]
    tb = e0.shape[1]
    eid = lax.broadcasted_iota(jnp.int32, (N_EXPERTS, tb), 0)
    hit0 = eid == e0
    hit1 = eid == e1
    used = jnp.where(jnp.logical_or(hit0, hit1), 1.0, 0.0)
    before = jnp.dot(used.astype(BF16), tri_ref[...], preferred_element_type=F32)
    rank = before + _lane_tile(carry_scr[...], tb // V7X_LANES)
    r0_ref[0] = jnp.sum(jnp.where(hit0, rank, 0.0), axis=0, keepdims=True).astype(jnp.int32)
    r1_ref[0] = jnp.sum(jnp.where(hit1, rank, 0.0), axis=0, keepdims=True).astype(jnp.int32)
    carry_scr[...] = carry_scr[...] + jnp.sum(used, axis=1, keepdims=True)
    cnt_ref[...] = carry_scr[...]


def expert_ranks(e0, e1, tb=512):
    b, _, s = e0.shape
    tb = min(tb, s)
    tri = (lax.broadcasted_iota(jnp.int32, (tb, tb), 0)
           < lax.broadcasted_iota(jnp.int32, (tb, tb), 1)).astype(BF16)
    row = pl.BlockSpec((1, 1, tb), lambda i, j: (i, 0, j))
    ishape = jax.ShapeDtypeStruct((b, 1, s), jnp.int32)
    return pl.pallas_call(
        _rank_kernel, grid=(b, s // tb),
        in_specs=[row, row, pl.BlockSpec((tb, tb), lambda i, j: (0, 0))],
        out_specs=[row, row, pl.BlockSpec((N_EXPERTS, V7X_LANES), lambda i, j: (0, 0))],
        out_shape=[ishape, ishape, jax.ShapeDtypeStruct((N_EXPERTS, V7X_LANES), F32)],
        scratch_shapes=[pltpu.VMEM((N_EXPERTS, V7X_LANES), F32)],
        compiler_params=_cp(("arbitrary", "arbitrary")),
        name="expert_ranks",
    )(e0, e1, tri)


MOE_TM = 256


def _row_gather(idx_ref, base, src_hbm, dst, sem, n):
    def issue(r, carry):
        pltpu.make_async_copy(src_hbm.at[pl.ds(idx_ref[base + r], 1), :], dst.at[pl.ds(r, 1), :], sem).start()
        return carry

    lax.fori_loop(0, n, issue, 0, unroll=8)


def _wait_rows(buf, sem):
    pltpu.make_async_copy(buf, buf, sem).wait()


def _moe_group_kernel(te_ref, nu_ref, src_ref, x_hbm, wg_ref, wu_ref, wd_ref, y_ref, xbuf, sem):
    del te_ref
    i = pl.program_id(0)
    n_used = nu_ref[0]
    tm = xbuf.shape[1]

    @pl.when(i == 0)
    def _():
        _row_gather(src_ref, 0, x_hbm, xbuf.at[0], sem.at[0], tm)

    @pl.when(i + 1 < n_used)
    def _():
        nxt = (i + 1) % 2
        _row_gather(src_ref, (i + 1) * tm, x_hbm, xbuf.at[nxt], sem.at[nxt], tm)

    @pl.when(i < n_used)
    def _():
        slot = i % 2
        _wait_rows(xbuf.at[slot], sem.at[slot])
        xb = xbuf[slot].astype(BF16)
        hg = jnp.dot(xb, wg_ref[0], preferred_element_type=F32)
        hu = jnp.dot(xb, wu_ref[0], preferred_element_type=F32)
        h = (hg / (1.0 + jnp.exp(-hg))) * hu
        y_ref[...] = jnp.dot(h.astype(BF16), wd_ref[0], preferred_element_type=F32)

    @pl.when(i >= n_used)
    def _():
        y_ref[...] = jnp.zeros(y_ref.shape, y_ref.dtype)


def moe_grouped_ffn(x_rows, src, tile_expert, n_used, w_gate, w_up, w_down):
    t, d = x_rows.shape
    p = src.shape[0]
    tm = MOE_TM
    grid_spec = pltpu.PrefetchScalarGridSpec(
        num_scalar_prefetch=3,
        grid=(p // tm,),
        in_specs=[pl.BlockSpec(memory_space=pl.ANY),
                  pl.BlockSpec((1, d, D_EXPERT), lambda i, te, nu, sr: (te[i], 0, 0)),
                  pl.BlockSpec((1, d, D_EXPERT), lambda i, te, nu, sr: (te[i], 0, 0)),
                  pl.BlockSpec((1, D_EXPERT, d), lambda i, te, nu, sr: (te[i], 0, 0))],
        out_specs=pl.BlockSpec((tm, d), lambda i, te, nu, sr: (i, 0)),
        scratch_shapes=[pltpu.VMEM((2, tm, d), F32), pltpu.SemaphoreType.DMA((2,))],
    )
    return pl.pallas_call(
        _moe_group_kernel, grid_spec=grid_spec,
        out_shape=jax.ShapeDtypeStruct((p, d), F32),
        compiler_params=_cp(("arbitrary",)),
        name="moe_grouped_ffn",
    )(tile_expert, n_used, src, x_rows, w_gate, w_up, w_down)


def _moe_combine_kernel(p0_ref, p1_ref, y_hbm, w0_ref, w1_ref, x_ref, gm_ref, o_ref, buf0, buf1, sem):
    i = pl.program_id(0)
    tm = buf0.shape[1]

    def gather(tile, slot):
        _row_gather(p0_ref, tile * tm, y_hbm, buf0.at[slot], sem.at[0, slot], tm)
        _row_gather(p1_ref, tile * tm, y_hbm, buf1.at[slot], sem.at[1, slot], tm)

    @pl.when(i == 0)
    def _():
        gather(0, 0)

    @pl.when(i + 1 < pl.num_programs(0))
    def _():
        gather(i + 1, (i + 1) % 2)

    slot = i % 2
    _wait_rows(buf0.at[slot], sem.at[0, slot])
    _wait_rows(buf1.at[slot], sem.at[1, slot])
    moe = w0_ref[...] * buf0[slot] + w1_ref[...] * buf1[slot]
    o_ref[...] = x_ref[...] + gm_ref[0] * moe


def moe_combine(y, pos0, pos1, w0, w1, x, mod3, gate_idx, tm=256):
    b, s, d = x.shape
    t = b * s
    tm = min(tm, s)
    per_batch = s // tm
    grid_spec = pltpu.PrefetchScalarGridSpec(
        num_scalar_prefetch=2,
        grid=(t // tm,),
        in_specs=[pl.BlockSpec(memory_space=pl.ANY),
                  pl.BlockSpec((tm, 1), lambda i, p0, p1: (i, 0)),
                  pl.BlockSpec((tm, 1), lambda i, p0, p1: (i, 0)),
                  pl.BlockSpec((tm, d), lambda i, p0, p1: (i, 0)),
                  pl.BlockSpec((1, 1, d), lambda i, p0, p1: (i // per_batch, 0, gate_idx))],
        out_specs=pl.BlockSpec((tm, d), lambda i, p0, p1: (i, 0)),
        scratch_shapes=[pltpu.VMEM((2, tm, d), F32), pltpu.VMEM((2, tm, d), F32),
                        pltpu.SemaphoreType.DMA((2, 2))],
    )
    out = pl.pallas_call(
        _moe_combine_kernel, grid_spec=grid_spec,
        out_shape=jax.ShapeDtypeStruct((t, d), F32),
        compiler_params=_cp(("arbitrary",)),
        name="moe_combine",
    )(pos0, pos1, y, w0.reshape(t, 1), w1.reshape(t, 1), x.reshape(t, d), mod3)
    return out.reshape(b, s, d)


def moe_sparse(a2, e0, e1, w0, w1, w_gate, w_up, w_down, x, mod3, gate_idx):
    b, s, d = a2.shape
    t = b * s
    tm = MOE_TM
    n_tiles = (2 * t + N_EXPERTS * (tm - 1)) // tm
    r0, r1, cnt = expert_ranks(e0, e1)
    counts = cnt[:, 0].astype(jnp.int32)
    padded = (counts + tm - 1) // tm * tm
    ends = jnp.cumsum(padded)
    offs = ends - padded
    e0f, e1f = e0.reshape(t), e1.reshape(t)
    pos0 = offs[e0f] + r0.reshape(t)
    pos1 = offs[e1f] + r1.reshape(t)
    tok = jnp.arange(t, dtype=jnp.int32)
    src = jnp.zeros((n_tiles * tm,), jnp.int32).at[pos0].set(tok).at[pos1].set(tok)
    tile_start = jnp.arange(n_tiles, dtype=jnp.int32) * tm
    tile_expert = jnp.minimum(jnp.searchsorted(ends, tile_start, side="right"), N_EXPERTS - 1).astype(jnp.int32)
    n_used = (ends[-1:] // tm).astype(jnp.int32)
    y = moe_grouped_ffn(a2.reshape(t, d), src, tile_expert, n_used, w_gate, w_up, w_down)
    return moe_combine(y, pos0, pos1, w0, w1, x, mod3, gate_idx)


def _final_norm_kernel(x_ref, g_ref, o_ref):
    x = x_ref[0]
    o_ref[0] = x * lax.rsqrt(jnp.mean(x * x, axis=-1, keepdims=True) + EPS) * g_ref[...]


def final_norm(x, g, tm=512):
    b, s, d = x.shape
    tm = min(tm, s)
    return pl.pallas_call(
        _final_norm_kernel, grid=(b, s // tm),
        in_specs=[pl.BlockSpec((1, tm, d), lambda i, j: (i, j, 0)),
                  pl.BlockSpec((1, d), lambda i, j: (0, 0))],
        out_specs=pl.BlockSpec((1, tm, d), lambda i, j: (i, j, 0)),
        out_shape=jax.ShapeDtypeStruct((b, s, d), F32),
        compiler_params=_cp(("arbitrary", "arbitrary")),
        name="final_norm",
    )(x, g.reshape(1, d))


def _rot_cols(w):
    half = w.shape[-1] // 2
    return jnp.concatenate([-w[..., half:], w[..., :half]], axis=-1)


def _prep_layer(w_in, w_uq, w_ukv, w_o_mla, w_o_dil, w_o_sb, w_out, w_gate, w_up, w_down):
    o1 = Q_LORA
    o2 = o1 + KV_LORA
    o3 = o2 + ROPE_DIM
    o4 = o3 + 3 * DIL_WIDTH
    o5 = o4 + 3 * SB_WIDTH
    w_kr = w_in[:, o2:o3]
    w_lat = jnp.concatenate([w_in[:, :o2], w_kr, _rot_cols(w_kr)], axis=1).astype(BF16)
    dil_scale = DIL_HEAD_DIM ** -0.5 * LOG2E
    def dil_cols(part, g):
        lo = o3 + part * DIL_WIDTH + g * DIL_GW
        return w_in[:, lo:lo + DIL_GW]

    w_dil = [jnp.concatenate([dil_cols(0, g) * dil_scale, dil_cols(1, g), dil_cols(2, g)], axis=1).astype(BF16)
             for g in range(DIL_GROUPS)]
    sb_scale = -(SB_HEAD_DIM ** -0.5) * LOG2E
    w_sb = jnp.concatenate([w_in[:, o4:o4 + SB_WIDTH] * sb_scale, w_in[:, o4 + SB_WIDTH:o5]],
                           axis=1).astype(BF16)
    w_gl = w_in[:, o5:].astype(BF16)
    q_scale = MLA_QK ** -0.5 * LOG2E
    wq = (w_uq * q_scale).reshape(Q_LORA, MLA_HEADS, MLA_QK).transpose(1, 0, 2)
    wq_h = jnp.concatenate([wq, _rot_cols(wq[..., NOPE_DIM:])], axis=-1).astype(BF16)
    wkv_h = w_ukv.reshape(KV_LORA, MLA_HEADS, NOPE_DIM + MLA_V_DIM).transpose(1, 0, 2).astype(BF16)
    return dict(w_lat=w_lat, w_dil=w_dil, w_sb=w_sb, w_gl=w_gl, wq_h=wq_h, wkv_h=wkv_h,
                w_o_mla=w_o_mla.astype(BF16), w_o_dil=w_o_dil.astype(BF16), w_o_sb=w_o_sb.astype(BF16),
                w_out=w_out.astype(BF16), w_gate=w_gate.astype(BF16), w_up=w_up.astype(BF16),
                w_down=w_down.astype(BF16))


def kernel(x, c, positions, w_ada, b_ada, g_mix, g_moe, w_in, g_q, w_uq, g_kv, w_ukv, w_o_mla, w_o_dil,
           w_o_sb, w_out, w_router, b_router, w_gate, w_up, w_down, g_final):
    b, s, d = x.shape
    depth = w_ada.shape[0]
    pos_f = positions.astype(F32)
    cos, sin = rope_tables(pos_f.reshape(b, s, 1))
    mod = ada_mod(c, w_ada, b_ada)
    w_router_t = w_router.T
    for l in range(depth):
        p = _prep_layer(w_in[l], w_uq[l], w_ukv[l], w_o_mla[l], w_o_dil[l], w_o_sb[l], w_out[l],
                        w_gate[l], w_up[l], w_down[l])
        mod3 = mod[l].reshape(b, 1, 6 * d)

        a = norm_mod(x, g_mix[l], mod3, sc_idx=1, sh_idx=0)
        lat = project(a, p["w_lat"], tn=p["w_lat"].shape[1])
        dil = [project_by_residue(a, p["w_dil"][g], DIL_RATES[g]) for g in range(DIL_GROUPS)]
        sbp = project(a, p["w_sb"])
        gl = project(a, p["w_gl"])

        q, k, v = mla_project(lat, g_q[l], p["wq_h"], g_kv[l], p["wkv_h"], cos, sin)
        y_mla = mla_attention(q, k, v)
        dil_out = [dil_group_attention(dil[g], pos_f, g) for g in range(DIL_GROUPS)]
        y_sb = sb_attention(sbp)
        merged = merge_branches(y_mla, [o for o, _ in dil_out], [e for _, e in dil_out], y_sb, gl,
                                p["w_o_mla"], p["w_o_dil"], p["w_o_sb"])
        x = resid_project(merged, p["w_out"], x, mod3, gate_idx=2)

        a2, logits_t = norm_mod(x, g_moe[l], mod3, sc_idx=4, sh_idx=3, w_router_t=w_router_t)
        e0, e1, w0, w1 = router_top2(logits_t, b_router)
        x = moe_sparse(a2, e0, e1, w0, w1, p["w_gate"], p["w_up"], p["w_down"], x, mod3, gate_idx=5)
    return final_norm(x, g_final)
```

```python
import functools
import math

import jax
import jax.numpy as jnp
from jax import lax
from jax.experimental import pallas as pl
from jax.experimental.pallas import tpu as pltpu

D_MODEL = 2048
EPS = 1e-6
MLA_HEADS = 8
Q_LORA = 512
KV_LORA = 512
NOPE_DIM = 128
ROPE_DIM = 64
MLA_V_DIM = 128
ROPE_THETA = 10000.0
MLA_QK = NOPE_DIM + ROPE_DIM
DIL_WINDOWS = (128, 512, 2048)
DIL_RATES = (1, 4, 16)
DIL_GROUPS = 3
DIL_HEADS = 4
DIL_HEAD_DIM = 128
DIL_SPAN = DIL_WINDOWS[0] // DIL_RATES[0]
DIL_GW = DIL_HEADS * DIL_HEAD_DIM
DIL_WIDTH = DIL_GROUPS * DIL_GW
ALIBI_MAX_BIAS = 8.0
SB_HEADS = 8
SB_HEAD_DIM = 128
SB_WIDTH = SB_HEADS * SB_HEAD_DIM
N_EXPERTS = 16
N_GROUPS = 4
EXPERTS_PER_GROUP = N_EXPERTS // N_GROUPS
D_EXPERT = 512

LOG2E = math.log2(math.e)
LN2 = math.log(2.0)
NEG_BIG = -1e30

V7X_LANES = 128
V7X_VMEM_BYTES = 64 * 1024 * 1024
VMEM_LIMIT = 56 * 1024 * 1024

F32 = jnp.float32
BF16 = jnp.bfloat16


def _cp(sem, vmem=VMEM_LIMIT):
    return pltpu.CompilerParams(dimension_semantics=sem, vmem_limit_bytes=vmem)


def _lane_tile(x, n):
    return x if n == 1 else jnp.concatenate([x] * n, axis=1)


def _nt_dot(a, b):
    return lax.dot_general(a, b, (((1,), (1,)), ((), ())), preferred_element_type=F32)


def _ada_kernel(c_ref, w_ref, b_ref, o_ref):
    c = c_ref[...]
    ca = c * (1.0 / (1.0 + jnp.exp(-c)))
    o_ref[0] = jnp.dot(ca, w_ref[0], preferred_element_type=F32,
                       precision=lax.Precision.HIGHEST) + b_ref[0]


def ada_mod(c, w_ada, b_ada, tn=1024):
    depth, d, n = w_ada.shape
    b = c.shape[0]
    return pl.pallas_call(
        _ada_kernel,
        grid=(depth, n // tn),
        in_specs=[pl.BlockSpec((b, d), lambda l, j: (0, 0)),
                  pl.BlockSpec((1, d, tn), lambda l, j: (l, 0, j)),
                  pl.BlockSpec((1, 1, tn), lambda l, j: (l, 0, j))],
        out_specs=pl.BlockSpec((1, b, tn), lambda l, j: (l, 0, j)),
        out_shape=jax.ShapeDtypeStruct((depth, b, n), F32),
        compiler_params=_cp(("arbitrary", "arbitrary")),
        name="ada_mod",
    )(c, w_ada, b_ada.reshape(depth, 1, n))


def _rope_table_kernel(pos_ref, inv_ref, cos_ref, sin_ref):
    ang = pos_ref[0] * inv_ref[...]
    cos_ref[0] = jnp.cos(ang)
    sin_ref[0] = jnp.sin(ang)


def rope_tables(pos_col, tm=512):
    b, s, _ = pos_col.shape
    tm = min(tm, s)
    half = ROPE_DIM // 2
    inv = ROPE_THETA ** (-jnp.arange(half, dtype=F32) / half)
    inv2 = jnp.concatenate([inv, inv]).reshape(1, ROPE_DIM)
    shp = jax.ShapeDtypeStruct((b, s, ROPE_DIM), F32)
    return pl.pallas_call(
        _rope_table_kernel,
        grid=(b, s // tm),
        in_specs=[pl.BlockSpec((1, tm, 1), lambda i, j: (i, j, 0)),
                  pl.BlockSpec((1, ROPE_DIM), lambda i, j: (0, 0))],
        out_specs=[pl.BlockSpec((1, tm, ROPE_DIM), lambda i, j: (i, j, 0))] * 2,
        out_shape=[shp, shp],
        compiler_params=_cp(("arbitrary", "arbitrary")),
        name="rope_tables",
    )(pos_col, inv2)


def _norm_mod(x, g, scale, shift):
    y = x * lax.rsqrt(jnp.mean(x * x, axis=-1, keepdims=True) + EPS)
    return (y * g) * (1.0 + scale) + shift


def _norm_kernel(x_ref, g_ref, sc_ref, sh_ref, o_ref):
    o_ref[0] = _norm_mod(x_ref[0], g_ref[...], sc_ref[0], sh_ref[0]).astype(o_ref.dtype)


def _norm_router_kernel(x_ref, g_ref, sc_ref, sh_ref, wr_ref, o_ref, lg_ref):
    a = _norm_mod(x_ref[0], g_ref[...], sc_ref[0], sh_ref[0])
    o_ref[0] = a.astype(o_ref.dtype)
    lg_ref[0] = lax.dot_general(wr_ref[...], a, (((1,), (1,)), ((), ())),
                                preferred_element_type=F32, precision=lax.Precision.HIGHEST)


def norm_mod(x, g, mod3, sc_idx, sh_idx, w_router_t=None, tm=512):
    b, s, d = x.shape
    tm = min(tm, s)
    in_specs = [pl.BlockSpec((1, tm, d), lambda i, j: (i, j, 0)),
                pl.BlockSpec((1, d), lambda i, j: (0, 0)),
                pl.BlockSpec((1, 1, d), lambda i, j: (i, 0, sc_idx)),
                pl.BlockSpec((1, 1, d), lambda i, j: (i, 0, sh_idx))]
    o_spec = pl.BlockSpec((1, tm, d), lambda i, j: (i, j, 0))
    o_shape = jax.ShapeDtypeStruct((b, s, d), BF16 if w_router_t is None else F32)
    if w_router_t is None:
        return pl.pallas_call(
            _norm_kernel, grid=(b, s // tm), in_specs=in_specs, out_specs=o_spec, out_shape=o_shape,
            compiler_params=_cp(("arbitrary", "arbitrary")), name="norm_mod",
        )(x, g.reshape(1, d), mod3, mod3)
    e = w_router_t.shape[0]
    return pl.pallas_call(
        _norm_router_kernel, grid=(b, s // tm),
        in_specs=in_specs + [pl.BlockSpec((e, d), lambda i, j: (0, 0))],
        out_specs=[o_spec, pl.BlockSpec((1, e, tm), lambda i, j: (i, 0, j))],
        out_shape=[o_shape, jax.ShapeDtypeStruct((b, e, s), F32)],
        compiler_params=_cp(("arbitrary", "arbitrary")), name="norm_router",
    )(x, g.reshape(1, d), mod3, mod3, w_router_t)


def _matmul_kernel(a_ref, w_ref, o_ref):
    o_ref[0] = jnp.dot(a_ref[0], w_ref[...], preferred_element_type=F32).astype(o_ref.dtype)


def project(a, w, tm=1024, tn=512, out_dtype=BF16):
    b, s, k = a.shape
    n = w.shape[1]
    tm = min(tm, s)
    tn = min(tn, n)
    assert n % tn == 0 and s % tm == 0
    return pl.pallas_call(
        _matmul_kernel,
        grid=(b, s // tm, n // tn),
        in_specs=[pl.BlockSpec((1, tm, k), lambda i, j, c: (i, j, 0)),
                  pl.BlockSpec((k, tn), lambda i, j, c: (0, c))],
        out_specs=pl.BlockSpec((1, tm, tn), lambda i, j, c: (i, j, c)),
        out_shape=jax.ShapeDtypeStruct((b, s, n), out_dtype),
        compiler_params=_cp(("arbitrary", "arbitrary", "arbitrary")),
        name="project",
    )(a, w)


def _matmul_residue_kernel(a_ref, w_ref, o_ref, res_scr, *, rate):
    res = jnp.dot(a_ref[0], w_ref[...], preferred_element_type=F32)
    slabs, rows, lanes = res_scr.shape
    for c in range(slabs):
        res_scr[c] = res[:, c * lanes:(c + 1) * lanes]
    n = rows // rate
    for rho in range(rate):
        for c in range(slabs):
            o_ref[0, rho, :, c * lanes:(c + 1) * lanes] = (
                res_scr[c, pl.ds(rho, n, stride=rate), :].astype(o_ref.dtype))


def project_by_residue(a, w, rate, tm=1024, tn=512):
    b, s, k = a.shape
    n = w.shape[1]
    if rate == 1:
        return project(a, w, tm=tm, tn=tn).reshape(b, 1, s, n)
    tm = min(tm, s)
    assert n % tn == 0 and s % tm == 0 and tm % (rate * 16) == 0
    return pl.pallas_call(
        functools.partial(_matmul_residue_kernel, rate=rate),
        grid=(b, s // tm, n // tn),
        in_specs=[pl.BlockSpec((1, tm, k), lambda i, j, c: (i, j, 0)),
                  pl.BlockSpec((k, tn), lambda i, j, c: (0, c))],
        out_specs=pl.BlockSpec((1, rate, tm // rate, tn), lambda i, j, c: (i, 0, j, c)),
        out_shape=jax.ShapeDtypeStruct((b, rate, s // rate, n), BF16),
        scratch_shapes=[pltpu.VMEM((tn // V7X_LANES, tm, V7X_LANES), F32)],
        compiler_params=_cp(("arbitrary", "arbitrary", "arbitrary")),
        name=f"project_by_residue_{rate}",
    )(a, w)


def _latent_norm(c, g):
    c = c.astype(F32)
    return (c * lax.rsqrt(jnp.mean(c * c, axis=-1, keepdims=True) + EPS) * g).astype(BF16)


def _mla_proj_kernel(cq_ref, ckv_ref, kr_ref, gq_ref, gkv_ref, wq_ref, wkv_ref, cos_ref, sin_ref,
                     q_ref, k_ref, v_ref):
    cos, sin = cos_ref[0], sin_ref[0]
    rq = _latent_norm(cq_ref[0], gq_ref[...])
    rkv = _latent_norm(ckv_ref[0], gkv_ref[...])
    kr = kr_ref[0].astype(F32)
    k_rope = (kr[:, :ROPE_DIM] * cos + kr[:, ROPE_DIM:] * sin).astype(k_ref.dtype)
    ones = jnp.ones((rq.shape[0], MLA_V_DIM), v_ref.dtype)
    for h in range(MLA_HEADS):
        r = jnp.dot(rq, wq_ref[h], preferred_element_type=F32)
        q_ref[0, h, :, :NOPE_DIM] = r[:, :NOPE_DIM].astype(q_ref.dtype)
        roped = r[:, NOPE_DIM:NOPE_DIM + ROPE_DIM] * cos + r[:, NOPE_DIM + ROPE_DIM:] * sin
        q_ref[0, h, :, NOPE_DIM:] = roped.astype(q_ref.dtype)
        r = jnp.dot(rkv, wkv_ref[h], preferred_element_type=F32)
        k_ref[0, h, :, :NOPE_DIM] = r[:, :NOPE_DIM].astype(k_ref.dtype)
        k_ref[0, h, :, NOPE_DIM:] = k_rope
        v_ref[0, h, :, :MLA_V_DIM] = r[:, NOPE_DIM:].astype(v_ref.dtype)
        v_ref[0, h, :, MLA_V_DIM:] = ones


def mla_project(lat, g_q, wq_h, g_kv, wkv_h, cos, sin, tm=512):
    b, s, _ = lat.shape
    tm = min(tm, s)
    h = MLA_HEADS
    kr_blk = (Q_LORA + KV_LORA) // (2 * ROPE_DIM)
    tab = pl.BlockSpec((1, tm, ROPE_DIM), lambda i, j: (i, j, 0))

    def const(shape):
        return pl.BlockSpec(shape, lambda i, j: (0,) * len(shape))

    def head_major(width):
        return pl.BlockSpec((1, h, tm, width), lambda i, j: (i, 0, j, 0))

    return pl.pallas_call(
        _mla_proj_kernel, grid=(b, s // tm),
        in_specs=[pl.BlockSpec((1, tm, Q_LORA), lambda i, j: (i, j, 0)),
                  pl.BlockSpec((1, tm, KV_LORA), lambda i, j: (i, j, 1)),
                  pl.BlockSpec((1, tm, 2 * ROPE_DIM), lambda i, j: (i, j, kr_blk)),
                  const((1, Q_LORA)), const((1, KV_LORA)),
                  const((h, Q_LORA, 256)), const((h, KV_LORA, 256)), tab, tab],
        out_specs=[head_major(MLA_QK), head_major(MLA_QK), head_major(2 * MLA_V_DIM)],
        out_shape=[jax.ShapeDtypeStruct((b, h, s, MLA_QK), BF16),
                   jax.ShapeDtypeStruct((b, h, s, MLA_QK), BF16),
                   jax.ShapeDtypeStruct((b, h, s, 2 * MLA_V_DIM), BF16)],
        compiler_params=_cp(("arbitrary", "arbitrary")), name="mla_project",
    )(lat, lat, lat, g_q.reshape(1, Q_LORA), g_kv.reshape(1, KV_LORA), wq_h, wkv_h, cos, sin)


def _mla_attn_kernel(q_ref, k_ref, v_ref, o_ref, s0_scr, s1_scr, m_scr, acc_scr, *, tq, tk):
    qi = pl.program_id(2)
    m_scr[...] = jnp.full(m_scr.shape, NEG_BIG, F32)
    acc_scr[...] = jnp.zeros(acc_scr.shape, F32)

    def scores(c, dst, row0=0):
        start = pl.multiple_of(c * tk, tk)
        dst[row0:, :] = _nt_dot(q_ref[0, 0, row0:, :], k_ref[0, 0, pl.ds(start, tk), :])

    def absorb(src, c, col_off=None, row0=0):
        start = pl.multiple_of(c * tk, tk)
        s = src[row0:, :]
        n = tq - row0
        if col_off is not None:
            row = row0 + lax.broadcasted_iota(jnp.int32, (n, tk), 0)
            col = col_off + lax.broadcasted_iota(jnp.int32, (n, tk), 1)
            s = jnp.where(col <= row, s, NEG_BIG)
        m_prev = m_scr[row0:, :]
        m_new = jnp.maximum(m_prev, jnp.max(s, axis=-1, keepdims=True))
        p = jnp.exp2(s - _lane_tile(m_new, tk // V7X_LANES))
        alpha = jnp.exp2(m_prev - m_new)
        pv = jnp.dot(p.astype(BF16), v_ref[0, 0, pl.ds(start, tk), :], preferred_element_type=F32)
        acc_scr[row0:, :] = _lane_tile(alpha, 2) * acc_scr[row0:, :] + pv
        m_scr[row0:, :] = m_new

    scores(0, s0_scr)

    def body(i, carry):
        scores(2 * i + 1, s1_scr)
        absorb(s0_scr, 2 * i)
        scores(2 * i + 2, s0_scr)
        absorb(s1_scr, 2 * i + 1)
        return carry

    lax.fori_loop(0, qi, body, 0)
    scores(2 * qi + 1, s1_scr, row0=tk)
    absorb(s0_scr, 2 * qi, col_off=0)
    absorb(s1_scr, 2 * qi + 1, col_off=tk, row0=tk)
    acc = acc_scr[...]
    o_ref[0] = (acc[:, :MLA_V_DIM] / acc[:, MLA_V_DIM:]).astype(o_ref.dtype)


def mla_attention(q, k, v, tq=1024):
    b, h, s, _ = q.shape
    tq = min(tq, s)
    tk = tq // 2
    return pl.pallas_call(
        functools.partial(_mla_attn_kernel, tq=tq, tk=tk),
        grid=(b, h, s // tq),
        in_specs=[pl.BlockSpec((1, 1, tq, MLA_QK), lambda i, j, t: (i, j, t, 0)),
                  pl.BlockSpec((1, 1, s, MLA_QK), lambda i, j, t: (i, j, 0, 0)),
                  pl.BlockSpec((1, 1, s, 2 * MLA_V_DIM), lambda i, j, t: (i, j, 0, 0))],
        out_specs=pl.BlockSpec((1, tq, MLA_V_DIM), lambda i, j, t: (i, t, j)),
        out_shape=jax.ShapeDtypeStruct((b, s, h * MLA_V_DIM), BF16),
        scratch_shapes=[pltpu.VMEM((tq, tk), F32), pltpu.VMEM((tq, tk), F32),
                        pltpu.VMEM((tq, V7X_LANES), F32), pltpu.VMEM((tq, 2 * MLA_V_DIM), F32)],
        compiler_params=_cp(("arbitrary", "arbitrary", "arbitrary")),
        name="mla_attention",
    )(q, k, v)


SB_SKIP_LOG2 = 150.0


def _sb_attn_kernel(q_ref, k_ref, v_ref, tri_ref, o_ref, c_scr, acc_scr, *, tq, hp):
    qi = pl.program_id(2)
    c_scr[...] = jnp.zeros(c_scr.shape, F32)
    acc_scr[...] = jnp.zeros(acc_scr.shape, F32)
    reps = tq // V7X_LANES

    def chunk(j, diagonal):
        start = pl.multiple_of(j * tq, tq)
        heads = range(hp)
        cols = [slice(hh * SB_HEAD_DIM, (hh + 1) * SB_HEAD_DIM) for hh in heads]
        if diagonal:
            before = (lax.broadcasted_iota(jnp.int32, (tq, tq), 1)
                      < lax.broadcasted_iota(jnp.int32, (tq, tq), 0))
        zn = [_nt_dot(q_ref[0, :, cols[hh]], k_ref[0, pl.ds(start, tq), cols[hh]]) for hh in heads]
        lk = [jnp.minimum(z, 0.0) - jnp.log2(1.0 + jnp.exp2(-jnp.abs(z))) for z in zn]
        if diagonal:
            lk = [jnp.where(before, x, 0.0) for x in lk]
        hi = [x.astype(BF16) for x in lk]
        lo = [(x - h.astype(F32)).astype(BF16) for x, h in zip(lk, hi)]
        suffix = [jnp.dot(jnp.concatenate([h, l], axis=1), tri_ref[...], preferred_element_type=F32)
                  for h, l in zip(hi, lo)]
        c = [c_scr[hh] for hh in heads]
        a = [jnp.exp2(sf - z + _lane_tile(cc, reps)) for sf, z, cc in zip(suffix, zn, c)]
        if diagonal:
            a = [jnp.where(before, x, 0.0) for x in a]
        for hh in heads:
            acc_scr[:, cols[hh]] += jnp.dot(a[hh].astype(BF16), v_ref[0, pl.ds(start, tq), cols[hh]],
                                            preferred_element_type=F32)
            c_scr[hh] = c[hh] + jnp.sum(lk[hh], axis=-1, keepdims=True)

    chunk(qi, True)

    def cond(state):
        j, go = state
        return jnp.logical_and(j >= 0, go > 0)

    def body(state):
        j, _ = state
        chunk(j, False)
        go = (jnp.max(c_scr[...]) > -SB_SKIP_LOG2).astype(jnp.int32)
        return j - 1, go

    lax.while_loop(cond, body, (qi - 1, jnp.int32(1)))
    o_ref[0] = acc_scr[...].astype(o_ref.dtype)


def sb_attention(qkv, tq=256, hp=4):
    b, s, _ = qkv.shape
    tq = min(tq, s)
    hw = hp * SB_HEAD_DIM
    nhb = SB_HEADS // hp
    tri = (lax.broadcasted_iota(jnp.int32, (tq, tq), 0)
           >= lax.broadcasted_iota(jnp.int32, (tq, tq), 1)).astype(BF16)
    tri2 = jnp.concatenate([tri, tri], axis=0)
    return pl.pallas_call(
        functools.partial(_sb_attn_kernel, tq=tq, hp=hp),
        grid=(b, nhb, s // tq),
        in_specs=[pl.BlockSpec((1, tq, hw), lambda i, j, t: (i, t, j)),
                  pl.BlockSpec((1, s, hw), lambda i, j, t: (i, 0, nhb + j)),
                  pl.BlockSpec((1, s, hw), lambda i, j, t: (i, 0, 2 * nhb + j)),
                  pl.BlockSpec((2 * tq, tq), lambda i, j, t: (0, 0))],
        out_specs=pl.BlockSpec((1, tq, hw), lambda i, j, t: (i, t, j)),
        out_shape=jax.ShapeDtypeStruct((b, s, SB_WIDTH), BF16),
        scratch_shapes=[pltpu.VMEM((hp, tq, V7X_LANES), F32), pltpu.VMEM((tq, hw), F32)],
        compiler_params=_cp(("arbitrary", "arbitrary", "arbitrary")),
        name="sb_attention",
    )(qkv, qkv, qkv, tri2)


def _dil_kernel(q_ref, kc_ref, vc_ref, kp_ref, vp_ref, pq_ref, pkc_ref, pkp_ref, sl_ref,
                o_ref, lse_ref, *, tq):
    ti = pl.program_id(2)
    w = DIL_SPAN
    a_idx = lax.broadcasted_iota(jnp.int32, (w, w), 0)
    c_idx = lax.broadcasted_iota(jnp.int32, (w, w), 1)
    own_ok = c_idx <= a_idx
    prev_tri = c_idx >= a_idx
    for sb in range(tq // w):
        rows = slice(sb * w, (sb + 1) * w)
        pq = pq_ref[0, 0, rows, :]
        pk_own = pkc_ref[0, 0, :, rows]
        if sb == 0:
            pk_prev = pkp_ref[0, 0]
            prev_ok = jnp.logical_and(prev_tri, ti > 0)
        else:
            pk_prev = pkc_ref[0, 0, :, (sb - 1) * w: sb * w]
            prev_ok = prev_tri
        dist_own = jnp.abs(pq - pk_own)
        dist_prev = jnp.abs(pq - pk_prev)
        heads = range(DIL_HEADS)
        cols = [slice(h * DIL_HEAD_DIM, (h + 1) * DIL_HEAD_DIM) for h in heads]
        prev_rows = slice((sb - 1) * w, sb * w)

        def k_prev(h):
            return kp_ref[0, :, cols[h]] if sb == 0 else kc_ref[0, prev_rows, cols[h]]

        def v_prev(h):
            return vp_ref[0, :, cols[h]] if sb == 0 else vc_ref[0, prev_rows, cols[h]]

        z_own = [_nt_dot(q_ref[0, rows, cols[h]], kc_ref[0, rows, cols[h]]) for h in heads]
        z_prev = [_nt_dot(q_ref[0, rows, cols[h]], k_prev(h)) for h in heads]
        s_own = [jnp.where(own_ok, z_own[h] - sl_ref[h] * dist_own, NEG_BIG) for h in heads]
        s_prev = [jnp.where(prev_ok, z_prev[h] - sl_ref[h] * dist_prev, NEG_BIG) for h in heads]
        m = [jnp.maximum(jnp.max(s_own[h], axis=-1, keepdims=True), jnp.max(s_prev[h], axis=-1, keepdims=True))
             for h in heads]
        p_own = [jnp.exp2(s_own[h] - m[h]) for h in heads]
        p_prev = [jnp.exp2(s_prev[h] - m[h]) for h in heads]
        den = [jnp.sum(p_own[h], axis=-1, keepdims=True) + jnp.sum(p_prev[h], axis=-1, keepdims=True)
               for h in heads]
        for h in heads:
            o = (jnp.dot(p_own[h].astype(BF16), vc_ref[0, rows, cols[h]], preferred_element_type=F32)
                 + jnp.dot(p_prev[h].astype(BF16), v_prev(h), preferred_element_type=F32)) / den[h]
            o_ref[0, rows, cols[h]] = o
            lse = (m[h] + jnp.log2(den[h])) * LN2
            lse_ref[0, rows, cols[h]] = jnp.broadcast_to(lse, (w, DIL_HEAD_DIM))


def dil_group_attention(dil_g, pos_f, g, tq=512):
    b, r, sr, c = dil_g.shape
    s = sr * r
    assert r == DIL_RATES[g]
    tq = min(tq, sr)
    w = DIL_SPAN
    assert sr % tq == 0 and tq % w == 0
    nsub = tq // w
    view = dil_g.reshape(b * r, sr, c)
    pos_r = pos_f.reshape(b, sr, r).transpose(0, 2, 1)
    pos_q = pos_r[..., None]
    pos_k = pos_r[:, :, None, :]
    n_h = DIL_GROUPS * DIL_HEADS
    slopes = 2.0 ** (-ALIBI_MAX_BIAS * jnp.arange(1, n_h + 1, dtype=F32) / n_h)
    slopes2 = (slopes * LOG2E)[g * DIL_HEADS:(g + 1) * DIL_HEADS]

    def cur(part):
        return pl.BlockSpec((1, tq, DIL_GW), lambda i, p, t: (i * r + p, t, part))

    def prev(part):
        return pl.BlockSpec((1, w, DIL_GW), lambda i, p, t: (i * r + p, jnp.maximum(t * nsub - 1, 0), part))

    out_spec = pl.BlockSpec((1, tq, DIL_GW), lambda i, p, t: (i, t, p))
    out_shape = jax.ShapeDtypeStruct((b, sr, r * DIL_GW), F32)
    o, lse = pl.pallas_call(
        functools.partial(_dil_kernel, tq=tq),
        grid=(b, r, sr // tq),
        in_specs=[cur(0), cur(1), cur(2), prev(1), prev(2),
                  pl.BlockSpec((1, 1, tq, 1), lambda i, p, t: (i, p, t, 0)),
                  pl.BlockSpec((1, 1, 1, tq), lambda i, p, t: (i, p, 0, t)),
                  pl.BlockSpec((1, 1, 1, w), lambda i, p, t: (i, p, 0, jnp.maximum(t * nsub - 1, 0))),
                  pl.BlockSpec(memory_space=pltpu.SMEM)],
        out_specs=[out_spec, out_spec],
        out_shape=[out_shape, out_shape],
        compiler_params=_cp(("arbitrary", "arbitrary", "arbitrary")),
        name=f"dil_attention_g{g}",
    )(view, view, view, view, view, pos_q, pos_k, pos_k, slopes2)
    return o.reshape(b, s, DIL_GW), lse.reshape(b, s, DIL_GW)


def _merge_kernel(ym_ref, o0_ref, o1_ref, o2_ref, l0_ref, l1_ref, l2_ref, ys_ref,
                  g0_ref, g1_ref, g2_ref, wm_ref, wd_ref, ws_ref, out_ref):
    l0, l1, l2 = l0_ref[0], l1_ref[0], l2_ref[0]
    mx = jnp.maximum(jnp.maximum(l0, l1), l2)
    e0, e1, e2 = jnp.exp(l0 - mx), jnp.exp(l1 - mx), jnp.exp(l2 - mx)
    y_dil = (e0 * o0_ref[0] + e1 * o1_ref[0] + e2 * o2_ref[0]) / (e0 + e1 + e2)

    def sig(ref):
        return 1.0 / (1.0 + jnp.exp(-ref[0].astype(F32)))

    merged = (sig(g0_ref) * jnp.dot(ym_ref[0], wm_ref[...], preferred_element_type=F32)
              + sig(g1_ref) * jnp.dot(y_dil.astype(BF16), wd_ref[...], preferred_element_type=F32)
              + sig(g2_ref) * jnp.dot(ys_ref[0], ws_ref[...], preferred_element_type=F32))
    out_ref[0] = merged.astype(out_ref.dtype)


def merge_branches(y_mla, dil_o, dil_lse, y_sb, gates, w_o_mla, w_o_dil, w_o_sb, tm=256):
    b, s, _ = y_mla.shape
    d = D_MODEL
    tm = min(tm, s)

    def row(width, cb=0):
        return pl.BlockSpec((1, tm, width), lambda i, j: (i, j, cb))

    def full(w):
        return pl.BlockSpec(w.shape, lambda i, j: (0, 0))

    return pl.pallas_call(
        _merge_kernel, grid=(b, s // tm),
        in_specs=[row(y_mla.shape[2])] + [row(DIL_GW)] * 6 + [row(SB_WIDTH),
                  row(d, 0), row(d, 1), row(d, 2), full(w_o_mla), full(w_o_dil), full(w_o_sb)],
        out_specs=row(d),
        out_shape=jax.ShapeDtypeStruct((b, s, d), BF16),
        compiler_params=_cp(("arbitrary", "arbitrary")),
        name="merge_branches",
    )(y_mla, *dil_o, *dil_lse, y_sb, gates, gates, gates, w_o_mla, w_o_dil, w_o_sb)


def _resid_proj_kernel(a_ref, w_ref, x_ref, g_ref, o_ref):
    y = jnp.dot(a_ref[0], w_ref[...], preferred_element_type=F32)
    o_ref[0] = x_ref[0] + g_ref[0] * y


def resid_project(a, w, x, mod3, gate_idx, tm=512, tn=1024):
    b, s, k = a.shape
    n = w.shape[1]
    tm = min(tm, s)
    return pl.pallas_call(
        _resid_proj_kernel, grid=(b, s // tm, n // tn),
        in_specs=[pl.BlockSpec((1, tm, k), lambda i, j, c: (i, j, 0)),
                  pl.BlockSpec((k, tn), lambda i, j, c: (0, c)),
                  pl.BlockSpec((1, tm, tn), lambda i, j, c: (i, j, c)),
                  pl.BlockSpec((1, 1, tn), lambda i, j, c: (i, 0, gate_idx * (n // tn) + c))],
        out_specs=pl.BlockSpec((1, tm, tn), lambda i, j, c: (i, j, c)),
        out_shape=jax.ShapeDtypeStruct((b, s, n), F32),
        compiler_params=_cp(("arbitrary", "arbitrary", "arbitrary")),
        name="resid_project",
    )(a, w, x, mod3)


def _router_kernel(lg_ref, b_ref, e0_ref, e1_ref, w0_ref, w1_ref):
    lg = lg_ref[0]
    tm = lg.shape[1]
    sc = [1.0 / (1.0 + jnp.exp(-lg[e:e + 1, :])) for e in range(N_EXPERTS)]
    bi = [sc[e] + b_ref[e] for e in range(N_EXPERTS)]
    n = EXPERTS_PER_GROUP
    gscore = []
    for g in range(N_GROUPS):
        v = bi[g * n:(g + 1) * n]
        pair_max = None
        for a in range(n):
            for c in range(a + 1, n):
                pm = v[a] + v[c]
                pair_max = pm if pair_max is None else jnp.maximum(pair_max, pm)
        gscore.append(pair_max)
    best = gscore[0]
    gsel = jnp.zeros((1, tm), jnp.int32)
    for g in range(1, N_GROUPS):
        better = gscore[g] > best
        best = jnp.where(better, gscore[g], best)
        gsel = jnp.where(better, g, gsel)
    gb = [sum(jnp.where(gsel == g, bi[g * n + i], 0.0) for g in range(N_GROUPS)) for i in range(n)]
    gs = [sum(jnp.where(gsel == g, sc[g * n + i], 0.0) for g in range(N_GROUPS)) for i in range(n)]
    sel = []
    for i in range(n):
        beaten = jnp.zeros((1, tm), jnp.int32)
        for j in range(n):
            if j == i:
                continue
            wins = (gb[j] > gb[i]) if j > i else (gb[j] >= gb[i])
            beaten = beaten + wins.astype(jnp.int32)
        sel.append(beaten < 2)
    den = sum(jnp.where(sel[i], gs[i], 0.0) for i in range(n))
    i0 = jnp.where(sel[0], 0, jnp.where(sel[1], 1, 2))
    i1 = jnp.where(sel[3], 3, jnp.where(sel[2], 2, 1))
    e0_ref[0] = gsel * n + i0
    e1_ref[0] = gsel * n + i1
    w0_ref[0] = sum(jnp.where(i0 == i, gs[i], 0.0) for i in range(n)) / den
    w1_ref[0] = sum(jnp.where(i1 == i, gs[i], 0.0) for i in range(n)) / den


def router_top2(logits_t, b_router, tm=1024):
    b, e, s = logits_t.shape
    tm = min(tm, s)
    row = pl.BlockSpec((1, 1, tm), lambda i, j: (i, 0, j))
    ishape = jax.ShapeDtypeStruct((b, 1, s), jnp.int32)
    fshape = jax.ShapeDtypeStruct((b, 1, s), F32)
    return pl.pallas_call(
        _router_kernel, grid=(b, s // tm),
        in_specs=[pl.BlockSpec((1, e, tm), lambda i, j: (i, 0, j)),
                  pl.BlockSpec(memory_space=pltpu.SMEM)],
        out_specs=[row, row, row, row],
        out_shape=[ishape, ishape, fshape, fshape],
        compiler_params=_cp(("arbitrary", "arbitrary")),
        name="router_top2",
    )(logits_t, b_router)


def _rank_kernel(e0_ref, e1_ref, tri_ref, r0_ref, r1_ref, cnt_ref, carry_scr):
    @pl.when(jnp.logical_and(pl.program_id(0) == 0, pl.program_id(1) == 0))
    def _():
        carry_scr[...] = jnp.zeros(carry_scr.shape, F32)

    e0, e1 = e0_ref[0], e1_ref[0]
    tb = e0.shape[1]
    eid = lax.broadcasted_iota(jnp.int32, (N_EXPERTS, tb), 0)
    hit0 = eid == e0
    hit1 = eid == e1
    used = jnp.where(jnp.logical_or(hit0, hit1), 1.0, 0.0)
    before = jnp.dot(used.astype(BF16), tri_ref[...], preferred_element_type=F32)
    rank = before + _lane_tile(carry_scr[...], tb // V7X_LANES)
    r0_ref[0] = jnp.sum(jnp.where(hit0, rank, 0.0), axis=0, keepdims=True).astype(jnp.int32)
    r1_ref[0] = jnp.sum(jnp.where(hit1, rank, 0.0), axis=0, keepdims=True).astype(jnp.int32)
    carry_scr[...] = carry_scr[...] + jnp.sum(used, axis=1, keepdims=True)
    cnt_ref[...] = carry_scr[...]


def expert_ranks(e0, e1, tb=512):
    b, _, s = e0.shape
    tb = min(tb, s)
    tri = (lax.broadcasted_iota(jnp.int32, (tb, tb), 0)
           < lax.broadcasted_iota(jnp.int32, (tb, tb), 1)).astype(BF16)
    row = pl.BlockSpec((1, 1, tb), lambda i, j: (i, 0, j))
    ishape = jax.ShapeDtypeStruct((b, 1, s), jnp.int32)
    return pl.pallas_call(
        _rank_kernel, grid=(b, s // tb),
        in_specs=[row, row, pl.BlockSpec((tb, tb), lambda i, j: (0, 0))],
        out_specs=[row, row, pl.BlockSpec((N_EXPERTS, V7X_LANES), lambda i, j: (0, 0))],
        out_shape=[ishape, ishape, jax.ShapeDtypeStruct((N_EXPERTS, V7X_LANES), F32)],
        scratch_shapes=[pltpu.VMEM((N_EXPERTS, V7X_LANES), F32)],
        compiler_params=_cp(("arbitrary", "arbitrary")),
        name="expert_ranks",
    )(e0, e1, tri)


MOE_TM = 256


def _row_gather(idx_ref, base, src_hbm, dst, sem, n):
    def issue(r, carry):
        pltpu.make_async_copy(src_hbm.at[pl.ds(idx_ref[base + r], 1), :], dst.at[pl.ds(r, 1), :], sem).start()
        return carry

    lax.fori_loop(0, n, issue, 0, unroll=8)


def _wait_rows(buf, sem):
    pltpu.make_async_copy(buf, buf, sem).wait()


def _moe_group_kernel(te_ref, nu_ref, src_ref, x_hbm, wg_ref, wu_ref, wd_ref, y_ref,
                      xbuf, wg_bf, wu_bf, wd_bf, sem):
    i = pl.program_id(0)
    n_used = nu_ref[0]
    tm = xbuf.shape[1]

    @pl.when(i == 0)
    def _():
        _row_gather(src_ref, 0, x_hbm, xbuf.at[0], sem.at[0], tm)

    @pl.when(jnp.logical_or(i == 0, te_ref[i] != te_ref[jnp.maximum(i - 1, 0)]))
    def _():
        wg_bf[...] = wg_ref[0, 0].astype(BF16)
        wu_bf[...] = wu_ref[0, 0].astype(BF16)
        wd_bf[...] = wd_ref[0, 0].astype(BF16)

    @pl.when(i < n_used)
    def _():
        slot = i % 2
        nxt = 1 - slot
        _wait_rows(xbuf.at[slot], sem.at[slot])
        base = jnp.minimum(i + 1, n_used - 1) * tm
        for r in range(tm):
            pltpu.make_async_copy(x_hbm.at[pl.ds(src_ref[base + r], 1), :], xbuf.at[nxt, pl.ds(r, 1), :],
                                  sem.at[nxt]).start()
        xb = xbuf[slot].astype(BF16)
        hg = jnp.dot(xb, wg_bf[...], preferred_element_type=F32)
        hu = jnp.dot(xb, wu_bf[...], preferred_element_type=F32)
        h = (hg / (1.0 + jnp.exp(-hg))) * hu
        y_ref[...] = jnp.dot(h.astype(BF16), wd_bf[...], preferred_element_type=F32)

    @pl.when(i == n_used - 1)
    def _():
        _wait_rows(xbuf.at[(i + 1) % 2], sem.at[(i + 1) % 2])

    @pl.when(i >= n_used)
    def _():
        y_ref[...] = jnp.zeros(y_ref.shape, y_ref.dtype)


def moe_grouped_ffn(x_rows, src, tile_expert, n_used, w_gate, w_up, w_down, layer):
    t, d = x_rows.shape
    p = src.shape[0]
    tm = MOE_TM
    grid_spec = pltpu.PrefetchScalarGridSpec(
        num_scalar_prefetch=3,
        grid=(p // tm,),
        in_specs=[pl.BlockSpec(memory_space=pl.ANY),
                  pl.BlockSpec((1, 1, d, D_EXPERT), lambda i, te, nu, sr: (layer, te[i], 0, 0)),
                  pl.BlockSpec((1, 1, d, D_EXPERT), lambda i, te, nu, sr: (layer, te[i], 0, 0)),
                  pl.BlockSpec((1, 1, D_EXPERT, d), lambda i, te, nu, sr: (layer, te[i], 0, 0))],
        out_specs=pl.BlockSpec((tm, d), lambda i, te, nu, sr: (i, 0)),
        scratch_shapes=[pltpu.VMEM((2, tm, d), F32),
                        pltpu.VMEM((d, D_EXPERT), BF16), pltpu.VMEM((d, D_EXPERT), BF16),
                        pltpu.VMEM((D_EXPERT, d), BF16), pltpu.SemaphoreType.DMA((2,))],
    )
    return pl.pallas_call(
        _moe_group_kernel, grid_spec=grid_spec,
        out_shape=jax.ShapeDtypeStruct((p, d), F32),
        compiler_params=_cp(("arbitrary",)),
        name="moe_grouped_ffn",
    )(tile_expert, n_used, src, x_rows, w_gate, w_up, w_down)


def _moe_combine_kernel(p0_ref, p1_ref, y_hbm, w0_ref, w1_ref, x_ref, gm_ref, o_ref, buf0, buf1, sem):
    i = pl.program_id(0)
    tm = buf0.shape[1]

    def gather(tile, slot):
        _row_gather(p0_ref, tile * tm, y_hbm, buf0.at[slot], sem.at[0, slot], tm)
        _row_gather(p1_ref, tile * tm, y_hbm, buf1.at[slot], sem.at[1, slot], tm)

    @pl.when(i == 0)
    def _():
        gather(0, 0)

    @pl.when(i + 1 < pl.num_programs(0))
    def _():
        gather(i + 1, (i + 1) % 2)

    slot = i % 2
    _wait_rows(buf0.at[slot], sem.at[0, slot])
    _wait_rows(buf1.at[slot], sem.at[1, slot])
    moe = w0_ref[...] * buf0[slot] + w1_ref[...] * buf1[slot]
    o_ref[...] = x_ref[...] + gm_ref[0] * moe


def moe_combine(y, pos0, pos1, w0, w1, x, mod3, gate_idx, tm=256):
    b, s, d = x.shape
    t = b * s
    tm = min(tm, s)
    per_batch = s // tm
    grid_spec = pltpu.PrefetchScalarGridSpec(
        num_scalar_prefetch=2,
        grid=(t // tm,),
        in_specs=[pl.BlockSpec(memory_space=pl.ANY),
                  pl.BlockSpec((tm, 1), lambda i, p0, p1: (i, 0)),
                  pl.BlockSpec((tm, 1), lambda i, p0, p1: (i, 0)),
                  pl.BlockSpec((tm, d), lambda i, p0, p1: (i, 0)),
                  pl.BlockSpec((1, 1, d), lambda i, p0, p1: (i // per_batch, 0, gate_idx))],
        out_specs=pl.BlockSpec((tm, d), lambda i, p0, p1: (i, 0)),
        scratch_shapes=[pltpu.VMEM((2, tm, d), F32), pltpu.VMEM((2, tm, d), F32),
                        pltpu.SemaphoreType.DMA((2, 2))],
    )
    out = pl.pallas_call(
        _moe_combine_kernel, grid_spec=grid_spec,
        out_shape=jax.ShapeDtypeStruct((t, d), F32),
        compiler_params=_cp(("arbitrary",)),
        name="moe_combine",
    )(pos0, pos1, y, w0.reshape(t, 1), w1.reshape(t, 1), x.reshape(t, d), mod3)
    return out.reshape(b, s, d)


def moe_sparse(a2, e0, e1, w0, w1, w_gate, w_up, w_down, layer, x, mod3, gate_idx):
    b, s, d = a2.shape
    t = b * s
    tm = MOE_TM
    n_tiles = (2 * t + N_EXPERTS * (tm - 1)) // tm
    r0, r1, cnt = expert_ranks(e0, e1)
    counts = cnt[:, 0].astype(jnp.int32)
    padded = (counts + tm - 1) // tm * tm
    ends = jnp.cumsum(padded)
    offs = ends - padded
    e0f, e1f = e0.reshape(t), e1.reshape(t)
    pos0 = offs[e0f] + r0.reshape(t)
    pos1 = offs[e1f] + r1.reshape(t)
    tok = jnp.arange(t, dtype=jnp.int32)
    src = jnp.zeros((n_tiles * tm,), jnp.int32).at[jnp.concatenate([pos0, pos1])].set(
        jnp.concatenate([tok, tok]), unique_indices=True)
    tile_start = jnp.arange(n_tiles, dtype=jnp.int32) * tm
    tile_expert = jnp.minimum(jnp.sum((tile_start[:, None] >= ends[None, :]).astype(jnp.int32), axis=1),
                              N_EXPERTS - 1)
    n_used = (ends[-1:] // tm).astype(jnp.int32)
    y = moe_grouped_ffn(a2.reshape(t, d), src, tile_expert, n_used, w_gate, w_up, w_down, layer)
    return moe_combine(y, pos0, pos1, w0, w1, x, mod3, gate_idx)


def _final_norm_kernel(x_ref, g_ref, o_ref):
    x = x_ref[0]
    o_ref[0] = x * lax.rsqrt(jnp.mean(x * x, axis=-1, keepdims=True) + EPS) * g_ref[...]


def final_norm(x, g, tm=512):
    b, s, d = x.shape
    tm = min(tm, s)
    return pl.pallas_call(
        _final_norm_kernel, grid=(b, s // tm),
        in_specs=[pl.BlockSpec((1, tm, d), lambda i, j: (i, j, 0)),
                  pl.BlockSpec((1, d), lambda i, j: (0, 0))],
        out_specs=pl.BlockSpec((1, tm, d), lambda i, j: (i, j, 0)),
        out_shape=jax.ShapeDtypeStruct((b, s, d), F32),
        compiler_params=_cp(("arbitrary", "arbitrary")),
        name="final_norm",
    )(x, g.reshape(1, d))


def _rot_cols(w):
    half = w.shape[-1] // 2
    return jnp.concatenate([-w[..., half:], w[..., :half]], axis=-1)


def _prep_layer(w_in, w_uq, w_ukv, w_o_mla, w_o_dil, w_o_sb, w_out):
    o1 = Q_LORA
    o2 = o1 + KV_LORA
    o3 = o2 + ROPE_DIM
    o4 = o3 + 3 * DIL_WIDTH
    o5 = o4 + 3 * SB_WIDTH
    w_kr = w_in[:, o2:o3]
    w_lat = jnp.concatenate([w_in[:, :o2], w_kr, _rot_cols(w_kr)], axis=1).astype(BF16)
    dil_scale = DIL_HEAD_DIM ** -0.5 * LOG2E
    def dil_cols(part, g):
        lo = o3 + part * DIL_WIDTH + g * DIL_GW
        return w_in[:, lo:lo + DIL_GW]

    w_dil = [jnp.concatenate([dil_cols(0, g) * dil_scale, dil_cols(1, g), dil_cols(2, g)], axis=1).astype(BF16)
             for g in range(DIL_GROUPS)]
    sb_scale = -(SB_HEAD_DIM ** -0.5) * LOG2E
    w_sb = jnp.concatenate([w_in[:, o4:o4 + SB_WIDTH] * sb_scale, w_in[:, o4 + SB_WIDTH:o5]],
                           axis=1).astype(BF16)
    w_gl = w_in[:, o5:].astype(BF16)
    q_scale = MLA_QK ** -0.5 * LOG2E
    wq = (w_uq * q_scale).reshape(Q_LORA, MLA_HEADS, MLA_QK).transpose(1, 0, 2)
    wq_h = jnp.concatenate([wq, _rot_cols(wq[..., NOPE_DIM:])], axis=-1).astype(BF16)
    wkv_h = w_ukv.reshape(KV_LORA, MLA_HEADS, NOPE_DIM + MLA_V_DIM).transpose(1, 0, 2).astype(BF16)
    return dict(w_lat=w_lat, w_dil=w_dil, w_sb=w_sb, w_gl=w_gl, wq_h=wq_h, wkv_h=wkv_h,
                w_o_mla=w_o_mla.astype(BF16), w_o_dil=w_o_dil.astype(BF16), w_o_sb=w_o_sb.astype(BF16),
                w_out=w_out.astype(BF16))


def kernel(x, c, positions, w_ada, b_ada, g_mix, g_moe, w_in, g_q, w_uq, g_kv, w_ukv, w_o_mla, w_o_dil,
           w_o_sb, w_out, w_router, b_router, w_gate, w_up, w_down, g_final):
    b, s, d = x.shape
    depth = w_ada.shape[0]
    pos_f = positions.astype(F32)
    cos, sin = rope_tables(pos_f.reshape(b, s, 1))
    mod = ada_mod(c, w_ada, b_ada)
    w_router_t = w_router.T
    for l in range(depth):
        p = _prep_layer(w_in[l], w_uq[l], w_ukv[l], w_o_mla[l], w_o_dil[l], w_o_sb[l], w_out[l])
        mod3 = mod[l].reshape(b, 1, 6 * d)

        a = norm_mod(x, g_mix[l], mod3, sc_idx=1, sh_idx=0)
        lat = project(a, p["w_lat"], tn=p["w_lat"].shape[1])
        dil = [project_by_residue(a, p["w_dil"][g], DIL_RATES[g]) for g in range(DIL_GROUPS)]
        sbp = project(a, p["w_sb"])
        gl = project(a, p["w_gl"])

        q, k, v = mla_project(lat, g_q[l], p["wq_h"], g_kv[l], p["wkv_h"], cos, sin)
        y_mla = mla_attention(q, k, v)
        dil_out = [dil_group_attention(dil[g], pos_f, g) for g in range(DIL_GROUPS)]
        y_sb = sb_attention(sbp)
        merged = merge_branches(y_mla, [o for o, _ in dil_out], [e for _, e in dil_out], y_sb, gl,
                                p["w_o_mla"], p["w_o_dil"], p["w_o_sb"])
        x = resid_project(merged, p["w_out"], x, mod3, gate_idx=2)

        a2, logits_t = norm_mod(x, g_moe[l], mod3, sc_idx=4, sh_idx=3, w_router_t=w_router_t)
        e0, e1, w0, w1 = router_top2(logits_t, b_router)
        x = moe_sparse(a2, e0, e1, w0, w1, w_gate, w_up, w_down, l, x, mod3, gate_idx=5)
    return final_norm(x, g_final)
```

```python
import functools
import math

import jax
import jax.numpy as jnp
from jax import lax
from jax.experimental import pallas as pl
from jax.experimental.pallas import tpu as pltpu

D_MODEL = 2048
EPS = 1e-6
MLA_HEADS = 8
Q_LORA = 512
KV_LORA = 512
NOPE_DIM = 128
ROPE_DIM = 64
MLA_V_DIM = 128
ROPE_THETA = 10000.0
MLA_QK = NOPE_DIM + ROPE_DIM
DIL_WINDOWS = (128, 512, 2048)
DIL_RATES = (1, 4, 16)
DIL_GROUPS = 3
DIL_HEADS = 4
DIL_HEAD_DIM = 128
DIL_SPAN = DIL_WINDOWS[0] // DIL_RATES[0]
DIL_GW = DIL_HEADS * DIL_HEAD_DIM
DIL_WIDTH = DIL_GROUPS * DIL_GW
ALIBI_MAX_BIAS = 8.0
SB_HEADS = 8
SB_HEAD_DIM = 128
SB_WIDTH = SB_HEADS * SB_HEAD_DIM
N_EXPERTS = 16
N_GROUPS = 4
EXPERTS_PER_GROUP = N_EXPERTS // N_GROUPS
D_EXPERT = 512

LOG2E = math.log2(math.e)
LN2 = math.log(2.0)
NEG_BIG = -1e30

V7X_LANES = 128
V7X_VMEM_BYTES = 64 * 1024 * 1024
VMEM_LIMIT = 56 * 1024 * 1024

F32 = jnp.float32
BF16 = jnp.bfloat16


def _cp(sem, vmem=VMEM_LIMIT):
    return pltpu.CompilerParams(dimension_semantics=sem, vmem_limit_bytes=vmem)


def _lane_tile(x, n):
    return x if n == 1 else jnp.concatenate([x] * n, axis=1)


def _nt_dot(a, b):
    return lax.dot_general(a, b, (((1,), (1,)), ((), ())), preferred_element_type=F32)


def _ada_kernel(c_ref, w_ref, b_ref, o_ref):
    c = c_ref[...]
    ca = c * (1.0 / (1.0 + jnp.exp(-c)))
    o_ref[0] = jnp.dot(ca, w_ref[0], preferred_element_type=F32,
                       precision=lax.Precision.HIGHEST) + b_ref[0]


def ada_mod(c, w_ada, b_ada, tn=1024):
    depth, d, n = w_ada.shape
    b = c.shape[0]
    return pl.pallas_call(
        _ada_kernel,
        grid=(depth, n // tn),
        in_specs=[pl.BlockSpec((b, d), lambda l, j: (0, 0)),
                  pl.BlockSpec((1, d, tn), lambda l, j: (l, 0, j)),
                  pl.BlockSpec((1, 1, tn), lambda l, j: (l, 0, j))],
        out_specs=pl.BlockSpec((1, b, tn), lambda l, j: (l, 0, j)),
        out_shape=jax.ShapeDtypeStruct((depth, b, n), F32),
        compiler_params=_cp(("arbitrary", "arbitrary")),
        name="ada_mod",
    )(c, w_ada, b_ada.reshape(depth, 1, n))


def _rope_table_kernel(pos_ref, inv_ref, cos_ref, sin_ref):
    ang = pos_ref[0] * inv_ref[...]
    cos_ref[0] = jnp.cos(ang)
    sin_ref[0] = jnp.sin(ang)


def rope_tables(pos_col, tm=512):
    b, s, _ = pos_col.shape
    tm = min(tm, s)
    half = ROPE_DIM // 2
    inv = ROPE_THETA ** (-jnp.arange(half, dtype=F32) / half)
    inv2 = jnp.concatenate([inv, inv]).reshape(1, ROPE_DIM)
    shp = jax.ShapeDtypeStruct((b, s, ROPE_DIM), F32)
    return pl.pallas_call(
        _rope_table_kernel,
        grid=(b, s // tm),
        in_specs=[pl.BlockSpec((1, tm, 1), lambda i, j: (i, j, 0)),
                  pl.BlockSpec((1, ROPE_DIM), lambda i, j: (0, 0))],
        out_specs=[pl.BlockSpec((1, tm, ROPE_DIM), lambda i, j: (i, j, 0))] * 2,
        out_shape=[shp, shp],
        compiler_params=_cp(("arbitrary", "arbitrary")),
        name="rope_tables",
    )(pos_col, inv2)


def _norm_mod(x, g, scale, shift):
    y = x * lax.rsqrt(jnp.mean(x * x, axis=-1, keepdims=True) + EPS)
    return (y * g) * (1.0 + scale) + shift


def _norm_kernel(x_ref, g_ref, sc_ref, sh_ref, o_ref):
    o_ref[0] = _norm_mod(x_ref[0], g_ref[...], sc_ref[0], sh_ref[0]).astype(o_ref.dtype)


ROW_SLABS = D_MODEL // V7X_LANES


def _store_row_slabs(ref, val):
    n = val.shape[0]
    for c in range(ROW_SLABS):
        ref[pl.ds(c, n, stride=ROW_SLABS), :] = val[:, c * V7X_LANES:(c + 1) * V7X_LANES]


def _load_row_slabs(ref, n, c):
    return ref[pl.ds(c, n, stride=ROW_SLABS), :]


def _norm_router_kernel(x_ref, g_ref, sc_ref, sh_ref, wr_ref, o_ref, lg_ref):
    a = _norm_mod(x_ref[0], g_ref[...], sc_ref[0], sh_ref[0])
    _store_row_slabs(o_ref.at[0], a)
    lg_ref[0] = lax.dot_general(wr_ref[...], a, (((1,), (1,)), ((), ())),
                                preferred_element_type=F32, precision=lax.Precision.HIGHEST)


def norm_mod(x, g, mod3, sc_idx, sh_idx, w_router_t=None, tm=512):
    b, s, d = x.shape
    tm = min(tm, s)
    in_specs = [pl.BlockSpec((1, tm, d), lambda i, j: (i, j, 0)),
                pl.BlockSpec((1, d), lambda i, j: (0, 0)),
                pl.BlockSpec((1, 1, d), lambda i, j: (i, 0, sc_idx)),
                pl.BlockSpec((1, 1, d), lambda i, j: (i, 0, sh_idx))]
    if w_router_t is None:
        return pl.pallas_call(
            _norm_kernel, grid=(b, s // tm), in_specs=in_specs,
            out_specs=pl.BlockSpec((1, tm, d), lambda i, j: (i, j, 0)),
            out_shape=jax.ShapeDtypeStruct((b, s, d), BF16),
            compiler_params=_cp(("arbitrary", "arbitrary")), name="norm_mod",
        )(x, g.reshape(1, d), mod3, mod3)
    e = w_router_t.shape[0]
    return pl.pallas_call(
        _norm_router_kernel, grid=(b, s // tm),
        in_specs=in_specs + [pl.BlockSpec((e, d), lambda i, j: (0, 0))],
        out_specs=[pl.BlockSpec((1, tm * ROW_SLABS, V7X_LANES), lambda i, j: (i, j, 0)),
                   pl.BlockSpec((1, e, tm), lambda i, j: (i, 0, j))],
        out_shape=[jax.ShapeDtypeStruct((b, s * ROW_SLABS, V7X_LANES), F32),
                   jax.ShapeDtypeStruct((b, e, s), F32)],
        compiler_params=_cp(("arbitrary", "arbitrary")), name="norm_router",
    )(x, g.reshape(1, d), mod3, mod3, w_router_t)


def _matmul_kernel(a_ref, w_ref, o_ref):
    o_ref[0] = jnp.dot(a_ref[0], w_ref[...], preferred_element_type=F32).astype(o_ref.dtype)


def project(a, w, tm=1024, tn=512, out_dtype=BF16):
    b, s, k = a.shape
    n = w.shape[1]
    tm = min(tm, s)
    tn = min(tn, n)
    assert n % tn == 0 and s % tm == 0
    return pl.pallas_call(
        _matmul_kernel,
        grid=(b, s // tm, n // tn),
        in_specs=[pl.BlockSpec((1, tm, k), lambda i, j, c: (i, j, 0)),
                  pl.BlockSpec((k, tn), lambda i, j, c: (0, c))],
        out_specs=pl.BlockSpec((1, tm, tn), lambda i, j, c: (i, j, c)),
        out_shape=jax.ShapeDtypeStruct((b, s, n), out_dtype),
        compiler_params=_cp(("arbitrary", "arbitrary", "arbitrary")),
        name="project",
    )(a, w)


def _matmul_residue_kernel(a_ref, w_ref, o_ref, res_scr, *, rate):
    res = jnp.dot(a_ref[0], w_ref[...], preferred_element_type=F32)
    slabs, rows, lanes = res_scr.shape
    for c in range(slabs):
        res_scr[c] = res[:, c * lanes:(c + 1) * lanes]
    n = rows // rate
    for rho in range(rate):
        for c in range(slabs):
            o_ref[0, rho, :, c * lanes:(c + 1) * lanes] = (
                res_scr[c, pl.ds(rho, n, stride=rate), :].astype(o_ref.dtype))


def project_by_residue(a, w, rate, tm=1024, tn=512):
    b, s, k = a.shape
    n = w.shape[1]
    if rate == 1:
        return project(a, w, tm=tm, tn=tn).reshape(b, 1, s, n)
    tm = min(tm, s)
    assert n % tn == 0 and s % tm == 0 and tm % (rate * 16) == 0
    return pl.pallas_call(
        functools.partial(_matmul_residue_kernel, rate=rate),
        grid=(b, s // tm, n // tn),
        in_specs=[pl.BlockSpec((1, tm, k), lambda i, j, c: (i, j, 0)),
                  pl.BlockSpec((k, tn), lambda i, j, c: (0, c))],
        out_specs=pl.BlockSpec((1, rate, tm // rate, tn), lambda i, j, c: (i, 0, j, c)),
        out_shape=jax.ShapeDtypeStruct((b, rate, s // rate, n), BF16),
        scratch_shapes=[pltpu.VMEM((tn // V7X_LANES, tm, V7X_LANES), F32)],
        compiler_params=_cp(("arbitrary", "arbitrary", "arbitrary")),
        name=f"project_by_residue_{rate}",
    )(a, w)


def _latent_norm(c, g):
    c = c.astype(F32)
    return (c * lax.rsqrt(jnp.mean(c * c, axis=-1, keepdims=True) + EPS) * g).astype(BF16)


def _mla_proj_kernel(cq_ref, ckv_ref, kr_ref, gq_ref, gkv_ref, wq_ref, wkv_ref, cos_ref, sin_ref,
                     q_ref, k_ref, v_ref):
    cos, sin = cos_ref[0], sin_ref[0]
    rq = _latent_norm(cq_ref[0], gq_ref[...])
    rkv = _latent_norm(ckv_ref[0], gkv_ref[...])
    kr = kr_ref[0].astype(F32)
    k_rope = (kr[:, :ROPE_DIM] * cos + kr[:, ROPE_DIM:] * sin).astype(k_ref.dtype)
    ones = jnp.ones((rq.shape[0], MLA_V_DIM), v_ref.dtype)
    for h in range(MLA_HEADS):
        r = jnp.dot(rq, wq_ref[h], preferred_element_type=F32)
        q_ref[0, h, :, :NOPE_DIM] = r[:, :NOPE_DIM].astype(q_ref.dtype)
        roped = r[:, NOPE_DIM:NOPE_DIM + ROPE_DIM] * cos + r[:, NOPE_DIM + ROPE_DIM:] * sin
        q_ref[0, h, :, NOPE_DIM:] = roped.astype(q_ref.dtype)
        r = jnp.dot(rkv, wkv_ref[h], preferred_element_type=F32)
        k_ref[0, h, :, :NOPE_DIM] = r[:, :NOPE_DIM].astype(k_ref.dtype)
        k_ref[0, h, :, NOPE_DIM:] = k_rope
        v_ref[0, h, :, :MLA_V_DIM] = r[:, NOPE_DIM:].astype(v_ref.dtype)
        v_ref[0, h, :, MLA_V_DIM:] = ones


def mla_project(lat, g_q, wq_h, g_kv, wkv_h, cos, sin, tm=512):
    b, s, _ = lat.shape
    tm = min(tm, s)
    h = MLA_HEADS
    kr_blk = (Q_LORA + KV_LORA) // (2 * ROPE_DIM)
    tab = pl.BlockSpec((1, tm, ROPE_DIM), lambda i, j: (i, j, 0))

    def const(shape):
        return pl.BlockSpec(shape, lambda i, j: (0,) * len(shape))

    def head_major(width):
        return pl.BlockSpec((1, h, tm, width), lambda i, j: (i, 0, j, 0))

    return pl.pallas_call(
        _mla_proj_kernel, grid=(b, s // tm),
        in_specs=[pl.BlockSpec((1, tm, Q_LORA), lambda i, j: (i, j, 0)),
                  pl.BlockSpec((1, tm, KV_LORA), lambda i, j: (i, j, 1)),
                  pl.BlockSpec((1, tm, 2 * ROPE_DIM), lambda i, j: (i, j, kr_blk)),
                  const((1, Q_LORA)), const((1, KV_LORA)),
                  const((h, Q_LORA, 256)), const((h, KV_LORA, 256)), tab, tab],
        out_specs=[head_major(MLA_QK), head_major(MLA_QK), head_major(2 * MLA_V_DIM)],
        out_shape=[jax.ShapeDtypeStruct((b, h, s, MLA_QK), BF16),
                   jax.ShapeDtypeStruct((b, h, s, MLA_QK), BF16),
                   jax.ShapeDtypeStruct((b, h, s, 2 * MLA_V_DIM), BF16)],
        compiler_params=_cp(("arbitrary", "arbitrary")), name="mla_project",
    )(lat, lat, lat, g_q.reshape(1, Q_LORA), g_kv.reshape(1, KV_LORA), wq_h, wkv_h, cos, sin)


def _mla_attn_kernel(q_ref, k_ref, v_ref, o_ref, s0_scr, s1_scr, m_scr, acc_scr, *, tq, tk):
    qi = pl.program_id(2)
    m_scr[...] = jnp.full(m_scr.shape, NEG_BIG, F32)
    acc_scr[...] = jnp.zeros(acc_scr.shape, F32)

    def scores(c, dst, row0=0):
        start = pl.multiple_of(c * tk, tk)
        dst[row0:, :] = _nt_dot(q_ref[0, 0, row0:, :], k_ref[0, 0, pl.ds(start, tk), :])

    def absorb(src, c, col_off=None, row0=0):
        start = pl.multiple_of(c * tk, tk)
        s = src[row0:, :]
        n = tq - row0
        if col_off is not None:
            row = row0 + lax.broadcasted_iota(jnp.int32, (n, tk), 0)
            col = col_off + lax.broadcasted_iota(jnp.int32, (n, tk), 1)
            s = jnp.where(col <= row, s, NEG_BIG)
        m_prev = m_scr[row0:, :]
        m_new = jnp.maximum(m_prev, jnp.max(s, axis=-1, keepdims=True))
        p = jnp.exp2(s - _lane_tile(m_new, tk // V7X_LANES))
        alpha = jnp.exp2(m_prev - m_new)
        pv = jnp.dot(p.astype(BF16), v_ref[0, 0, pl.ds(start, tk), :], preferred_element_type=F32)
        acc_scr[row0:, :] = _lane_tile(alpha, 2) * acc_scr[row0:, :] + pv
        m_scr[row0:, :] = m_new

    scores(0, s0_scr)

    def body(i, carry):
        scores(2 * i + 1, s1_scr)
        absorb(s0_scr, 2 * i)
        scores(2 * i + 2, s0_scr)
        absorb(s1_scr, 2 * i + 1)
        return carry

    lax.fori_loop(0, qi, body, 0)
    scores(2 * qi + 1, s1_scr, row0=tk)
    absorb(s0_scr, 2 * qi, col_off=0)
    absorb(s1_scr, 2 * qi + 1, col_off=tk, row0=tk)
    acc = acc_scr[...]
    o_ref[0] = (acc[:, :MLA_V_DIM] / acc[:, MLA_V_DIM:]).astype(o_ref.dtype)


def mla_attention(q, k, v, tq=1024):
    b, h, s, _ = q.shape
    tq = min(tq, s)
    tk = tq // 2
    return pl.pallas_call(
        functools.partial(_mla_attn_kernel, tq=tq, tk=tk),
        grid=(b, h, s // tq),
        in_specs=[pl.BlockSpec((1, 1, tq, MLA_QK), lambda i, j, t: (i, j, t, 0)),
                  pl.BlockSpec((1, 1, s, MLA_QK), lambda i, j, t: (i, j, 0, 0)),
                  pl.BlockSpec((1, 1, s, 2 * MLA_V_DIM), lambda i, j, t: (i, j, 0, 0))],
        out_specs=pl.BlockSpec((1, tq, MLA_V_DIM), lambda i, j, t: (i, t, j)),
        out_shape=jax.ShapeDtypeStruct((b, s, h * MLA_V_DIM), BF16),
        scratch_shapes=[pltpu.VMEM((tq, tk), F32), pltpu.VMEM((tq, tk), F32),
                        pltpu.VMEM((tq, V7X_LANES), F32), pltpu.VMEM((tq, 2 * MLA_V_DIM), F32)],
        compiler_params=_cp(("arbitrary", "arbitrary", "arbitrary")),
        name="mla_attention",
    )(q, k, v)


SB_SKIP_LOG2 = 150.0


def _sb_attn_kernel(q_ref, k_ref, v_ref, tri_ref, o_ref, c_scr, acc_scr, *, tq, hp):
    qi = pl.program_id(2)
    c_scr[...] = jnp.zeros(c_scr.shape, F32)
    acc_scr[...] = jnp.zeros(acc_scr.shape, F32)
    reps = tq // V7X_LANES

    def chunk(j, diagonal):
        start = pl.multiple_of(j * tq, tq)
        heads = range(hp)
        cols = [slice(hh * SB_HEAD_DIM, (hh + 1) * SB_HEAD_DIM) for hh in heads]
        if diagonal:
            before = (lax.broadcasted_iota(jnp.int32, (tq, tq), 1)
                      < lax.broadcasted_iota(jnp.int32, (tq, tq), 0))
        zn = [_nt_dot(q_ref[0, :, cols[hh]], k_ref[0, pl.ds(start, tq), cols[hh]]) for hh in heads]
        lk = [jnp.minimum(z, 0.0) - jnp.log2(1.0 + jnp.exp2(-jnp.abs(z))) for z in zn]
        if diagonal:
            lk = [jnp.where(before, x, 0.0) for x in lk]
        hi = [x.astype(BF16) for x in lk]
        lo = [(x - h.astype(F32)).astype(BF16) for x, h in zip(lk, hi)]
        suffix = [jnp.dot(jnp.concatenate([h, l], axis=1), tri_ref[...], preferred_element_type=F32)
                  for h, l in zip(hi, lo)]
        c = [c_scr[hh] for hh in heads]
        a = [jnp.exp2(sf - z + _lane_tile(cc, reps)) for sf, z, cc in zip(suffix, zn, c)]
        if diagonal:
            a = [jnp.where(before, x, 0.0) for x in a]
        for hh in heads:
            acc_scr[:, cols[hh]] += jnp.dot(a[hh].astype(BF16), v_ref[0, pl.ds(start, tq), cols[hh]],
                                            preferred_element_type=F32)
            c_scr[hh] = c[hh] + jnp.sum(lk[hh], axis=-1, keepdims=True)

    chunk(qi, True)

    def cond(state):
        j, go = state
        return jnp.logical_and(j >= 0, go > 0)

    def body(state):
        j, _ = state
        chunk(j, False)
        go = (jnp.max(c_scr[...]) > -SB_SKIP_LOG2).astype(jnp.int32)
        return j - 1, go

    lax.while_loop(cond, body, (qi - 1, jnp.int32(1)))
    o_ref[0] = acc_scr[...].astype(o_ref.dtype)


def sb_attention(qkv, tq=256, hp=4):
    b, s, _ = qkv.shape
    tq = min(tq, s)
    hw = hp * SB_HEAD_DIM
    nhb = SB_HEADS // hp
    tri = (lax.broadcasted_iota(jnp.int32, (tq, tq), 0)
           >= lax.broadcasted_iota(jnp.int32, (tq, tq), 1)).astype(BF16)
    tri2 = jnp.concatenate([tri, tri], axis=0)
    return pl.pallas_call(
        functools.partial(_sb_attn_kernel, tq=tq, hp=hp),
        grid=(b, nhb, s // tq),
        in_specs=[pl.BlockSpec((1, tq, hw), lambda i, j, t: (i, t, j)),
                  pl.BlockSpec((1, s, hw), lambda i, j, t: (i, 0, nhb + j)),
                  pl.BlockSpec((1, s, hw), lambda i, j, t: (i, 0, 2 * nhb + j)),
                  pl.BlockSpec((2 * tq, tq), lambda i, j, t: (0, 0))],
        out_specs=pl.BlockSpec((1, tq, hw), lambda i, j, t: (i, t, j)),
        out_shape=jax.ShapeDtypeStruct((b, s, SB_WIDTH), BF16),
        scratch_shapes=[pltpu.VMEM((hp, tq, V7X_LANES), F32), pltpu.VMEM((tq, hw), F32)],
        compiler_params=_cp(("arbitrary", "arbitrary", "arbitrary")),
        name="sb_attention",
    )(qkv, qkv, qkv, tri2)


def _dil_kernel(q_ref, kc_ref, vc_ref, kp_ref, vp_ref, pq_ref, pkc_ref, pkp_ref, sl_ref,
                o_ref, lse_ref, *, tq):
    ti = pl.program_id(2)
    w = DIL_SPAN
    a_idx = lax.broadcasted_iota(jnp.int32, (w, w), 0)
    c_idx = lax.broadcasted_iota(jnp.int32, (w, w), 1)
    own_ok = c_idx <= a_idx
    prev_tri = c_idx >= a_idx
    for sb in range(tq // w):
        rows = slice(sb * w, (sb + 1) * w)
        pq = pq_ref[0, 0, rows, :]
        pk_own = pkc_ref[0, 0, :, rows]
        if sb == 0:
            pk_prev = pkp_ref[0, 0]
            prev_ok = jnp.logical_and(prev_tri, ti > 0)
        else:
            pk_prev = pkc_ref[0, 0, :, (sb - 1) * w: sb * w]
            prev_ok = prev_tri
        dist_own = jnp.abs(pq - pk_own)
        dist_prev = jnp.abs(pq - pk_prev)
        heads = range(DIL_HEADS)
        cols = [slice(h * DIL_HEAD_DIM, (h + 1) * DIL_HEAD_DIM) for h in heads]
        prev_rows = slice((sb - 1) * w, sb * w)

        def k_prev(h):
            return kp_ref[0, :, cols[h]] if sb == 0 else kc_ref[0, prev_rows, cols[h]]

        def v_prev(h):
            return vp_ref[0, :, cols[h]] if sb == 0 else vc_ref[0, prev_rows, cols[h]]

        z_own = [_nt_dot(q_ref[0, rows, cols[h]], kc_ref[0, rows, cols[h]]) for h in heads]
        z_prev = [_nt_dot(q_ref[0, rows, cols[h]], k_prev(h)) for h in heads]
        s_own = [jnp.where(own_ok, z_own[h] - sl_ref[h] * dist_own, NEG_BIG) for h in heads]
        s_prev = [jnp.where(prev_ok, z_prev[h] - sl_ref[h] * dist_prev, NEG_BIG) for h in heads]
        m = [jnp.maximum(jnp.max(s_own[h], axis=-1, keepdims=True), jnp.max(s_prev[h], axis=-1, keepdims=True))
             for h in heads]
        p_own = [jnp.exp2(s_own[h] - m[h]) for h in heads]
        p_prev = [jnp.exp2(s_prev[h] - m[h]) for h in heads]
        den = [jnp.sum(p_own[h], axis=-1, keepdims=True) + jnp.sum(p_prev[h], axis=-1, keepdims=True)
               for h in heads]
        for h in heads:
            o = (jnp.dot(p_own[h].astype(BF16), vc_ref[0, rows, cols[h]], preferred_element_type=F32)
                 + jnp.dot(p_prev[h].astype(BF16), v_prev(h), preferred_element_type=F32)) / den[h]
            o_ref[0, rows, cols[h]] = o
            lse = (m[h] + jnp.log2(den[h])) * LN2
            lse_ref[0, rows, cols[h]] = jnp.broadcast_to(lse, (w, DIL_HEAD_DIM))


def dil_group_attention(dil_g, pos_f, g, tq=512):
    b, r, sr, c = dil_g.shape
    assert r == DIL_RATES[g]
    tq = min(tq, sr)
    w = DIL_SPAN
    assert sr % tq == 0 and tq % w == 0
    nsub = tq // w
    view = dil_g.reshape(b * r, sr, c)
    pos_r = pos_f.reshape(b, sr, r).transpose(0, 2, 1)
    pos_q = pos_r[..., None]
    pos_k = pos_r[:, :, None, :]
    n_h = DIL_GROUPS * DIL_HEADS
    slopes = 2.0 ** (-ALIBI_MAX_BIAS * jnp.arange(1, n_h + 1, dtype=F32) / n_h)
    slopes2 = (slopes * LOG2E)[g * DIL_HEADS:(g + 1) * DIL_HEADS]

    def cur(part):
        return pl.BlockSpec((1, tq, DIL_GW), lambda i, p, t: (i * r + p, t, part))

    def prev(part):
        return pl.BlockSpec((1, w, DIL_GW), lambda i, p, t: (i * r + p, jnp.maximum(t * nsub - 1, 0), part))

    out_spec = pl.BlockSpec((1, tq, DIL_GW), lambda i, p, t: (i * r + p, t, 0))
    out_shape = jax.ShapeDtypeStruct((b * r, sr, DIL_GW), F32)
    o, lse = pl.pallas_call(
        functools.partial(_dil_kernel, tq=tq),
        grid=(b, r, sr // tq),
        in_specs=[cur(0), cur(1), cur(2), prev(1), prev(2),
                  pl.BlockSpec((1, 1, tq, 1), lambda i, p, t: (i, p, t, 0)),
                  pl.BlockSpec((1, 1, 1, tq), lambda i, p, t: (i, p, 0, t)),
                  pl.BlockSpec((1, 1, 1, w), lambda i, p, t: (i, p, 0, jnp.maximum(t * nsub - 1, 0))),
                  pl.BlockSpec(memory_space=pltpu.SMEM)],
        out_specs=[out_spec, out_spec],
        out_shape=[out_shape, out_shape],
        compiler_params=_cp(("arbitrary", "arbitrary", "arbitrary")),
        name=f"dil_attention_g{g}",
    )(view, view, view, view, view, pos_q, pos_k, pos_k, slopes2)
    return o.reshape(b, r, sr, DIL_GW), lse.reshape(b, r, sr, DIL_GW)


def _token_order(ref, scr):
    _, r, n, width = ref.shape
    if r == 1:
        return ref[0, 0]
    slabs = width // V7X_LANES
    for rho in range(r):
        for c in range(slabs):
            scr[c, pl.ds(rho, n, stride=r), :] = ref[0, rho, :, c * V7X_LANES:(c + 1) * V7X_LANES]
    return jnp.concatenate([scr[c] for c in range(slabs)], axis=1)


def _merge_kernel(ym_ref, o0_ref, o1_ref, o2_ref, l0_ref, l1_ref, l2_ref, ys_ref,
                  g0_ref, g1_ref, g2_ref, wm_ref, wd_ref, ws_ref, out_ref, *scr):
    o0, o1, o2 = _token_order(o0_ref, scr[0]), _token_order(o1_ref, scr[1]), _token_order(o2_ref, scr[2])
    l0, l1, l2 = _token_order(l0_ref, scr[3]), _token_order(l1_ref, scr[4]), _token_order(l2_ref, scr[5])
    mx = jnp.maximum(jnp.maximum(l0, l1), l2)
    e0, e1, e2 = jnp.exp(l0 - mx), jnp.exp(l1 - mx), jnp.exp(l2 - mx)
    y_dil = (e0 * o0 + e1 * o1 + e2 * o2) / (e0 + e1 + e2)

    def sig(ref):
        return 1.0 / (1.0 + jnp.exp(-ref[0].astype(F32)))

    merged = (sig(g0_ref) * jnp.dot(ym_ref[0], wm_ref[...], preferred_element_type=F32)
              + sig(g1_ref) * jnp.dot(y_dil.astype(BF16), wd_ref[...], preferred_element_type=F32)
              + sig(g2_ref) * jnp.dot(ys_ref[0], ws_ref[...], preferred_element_type=F32))
    out_ref[0] = merged.astype(out_ref.dtype)


def merge_branches(y_mla, dil_o, dil_lse, y_sb, gates, w_o_mla, w_o_dil, w_o_sb, tm=256):
    b, s, _ = y_mla.shape
    d = D_MODEL
    tm = min(tm, s)

    def row(width, cb=0):
        return pl.BlockSpec((1, tm, width), lambda i, j: (i, j, cb))

    def by_residue(r):
        return pl.BlockSpec((1, r, tm // r, DIL_GW), lambda i, j: (i, 0, j, 0))

    def full(w):
        return pl.BlockSpec(w.shape, lambda i, j: (0, 0))

    dil_specs = [by_residue(r) for r in DIL_RATES]
    return pl.pallas_call(
        _merge_kernel, grid=(b, s // tm),
        in_specs=[row(y_mla.shape[2])] + dil_specs + dil_specs + [row(SB_WIDTH),
                  row(d, 0), row(d, 1), row(d, 2), full(w_o_mla), full(w_o_dil), full(w_o_sb)],
        out_specs=row(d),
        out_shape=jax.ShapeDtypeStruct((b, s, d), BF16),
        scratch_shapes=[pltpu.VMEM((DIL_GW // V7X_LANES, tm, V7X_LANES), F32)] * (2 * DIL_GROUPS),
        compiler_params=_cp(("arbitrary", "arbitrary")),
        name="merge_branches",
    )(y_mla, *dil_o, *dil_lse, y_sb, gates, gates, gates, w_o_mla, w_o_dil, w_o_sb)


def _resid_proj_kernel(a_ref, w_ref, x_ref, g_ref, o_ref):
    y = jnp.dot(a_ref[0], w_ref[...], preferred_element_type=F32)
    o_ref[0] = x_ref[0] + g_ref[0] * y


def resid_project(a, w, x, mod3, gate_idx, tm=512, tn=1024):
    b, s, k = a.shape
    n = w.shape[1]
    tm = min(tm, s)
    return pl.pallas_call(
        _resid_proj_kernel, grid=(b, s // tm, n // tn),
        in_specs=[pl.BlockSpec((1, tm, k), lambda i, j, c: (i, j, 0)),
                  pl.BlockSpec((k, tn), lambda i, j, c: (0, c)),
                  pl.BlockSpec((1, tm, tn), lambda i, j, c: (i, j, c)),
                  pl.BlockSpec((1, 1, tn), lambda i, j, c: (i, 0, gate_idx * (n // tn) + c))],
        out_specs=pl.BlockSpec((1, tm, tn), lambda i, j, c: (i, j, c)),
        out_shape=jax.ShapeDtypeStruct((b, s, n), F32),
        compiler_params=_cp(("arbitrary", "arbitrary", "arbitrary")),
        name="resid_project",
    )(a, w, x, mod3)


def _router_kernel(lg_ref, b_ref, e0_ref, e1_ref, w0_ref, w1_ref):
    lg = lg_ref[0]
    tm = lg.shape[1]
    sc = [1.0 / (1.0 + jnp.exp(-lg[e:e + 1, :])) for e in range(N_EXPERTS)]
    bi = [sc[e] + b_ref[e] for e in range(N_EXPERTS)]
    n = EXPERTS_PER_GROUP
    gscore = []
    for g in range(N_GROUPS):
        v = bi[g * n:(g + 1) * n]
        pair_max = None
        for a in range(n):
            for c in range(a + 1, n):
                pm = v[a] + v[c]
                pair_max = pm if pair_max is None else jnp.maximum(pair_max, pm)
        gscore.append(pair_max)
    best = gscore[0]
    gsel = jnp.zeros((1, tm), jnp.int32)
    for g in range(1, N_GROUPS):
        better = gscore[g] > best
        best = jnp.where(better, gscore[g], best)
        gsel = jnp.where(better, g, gsel)
    gb = [sum(jnp.where(gsel == g, bi[g * n + i], 0.0) for g in range(N_GROUPS)) for i in range(n)]
    gs = [sum(jnp.where(gsel == g, sc[g * n + i], 0.0) for g in range(N_GROUPS)) for i in range(n)]
    sel = []
    for i in range(n):
        beaten = jnp.zeros((1, tm), jnp.int32)
        for j in range(n):
            if j == i:
                continue
            wins = (gb[j] > gb[i]) if j > i else (gb[j] >= gb[i])
            beaten = beaten + wins.astype(jnp.int32)
        sel.append(beaten < 2)
    den = sum(jnp.where(sel[i], gs[i], 0.0) for i in range(n))
    i0 = jnp.where(sel[0], 0, jnp.where(sel[1], 1, 2))
    i1 = jnp.where(sel[3], 3, jnp.where(sel[2], 2, 1))
    e0_ref[0] = gsel * n + i0
    e1_ref[0] = gsel * n + i1
    w0_ref[0] = sum(jnp.where(i0 == i, gs[i], 0.0) for i in range(n)) / den
    w1_ref[0] = sum(jnp.where(i1 == i, gs[i], 0.0) for i in range(n)) / den


def router_top2(logits_t, b_router, tm=1024):
    b, e, s = logits_t.shape
    tm = min(tm, s)
    row = pl.BlockSpec((1, 1, tm), lambda i, j: (i, 0, j))
    ishape = jax.ShapeDtypeStruct((b, 1, s), jnp.int32)
    fshape = jax.ShapeDtypeStruct((b, 1, s), F32)
    return pl.pallas_call(
        _router_kernel, grid=(b, s // tm),
        in_specs=[pl.BlockSpec((1, e, tm), lambda i, j: (i, 0, j)),
                  pl.BlockSpec(memory_space=pltpu.SMEM)],
        out_specs=[row, row, row, row],
        out_shape=[ishape, ishape, fshape, fshape],
        compiler_params=_cp(("arbitrary", "arbitrary")),
        name="router_top2",
    )(logits_t, b_router)


def _rank_kernel(e0_ref, e1_ref, tri_ref, r0_ref, r1_ref, cnt_ref, carry_scr):
    @pl.when(jnp.logical_and(pl.program_id(0) == 0, pl.program_id(1) == 0))
    def _():
        carry_scr[...] = jnp.zeros(carry_scr.shape, F32)

    e0, e1 = e0_ref[0], e1_ref[0]
    tb = e0.shape[1]
    eid = lax.broadcasted_iota(jnp.int32, (N_EXPERTS, tb), 0)
    hit0 = eid == e0
    hit1 = eid == e1
    used = jnp.where(jnp.logical_or(hit0, hit1), 1.0, 0.0)
    before = jnp.dot(used.astype(BF16), tri_ref[...], preferred_element_type=F32)
    rank = before + _lane_tile(carry_scr[...], tb // V7X_LANES)
    r0_ref[0] = jnp.sum(jnp.where(hit0, rank, 0.0), axis=0, keepdims=True).astype(jnp.int32)
    r1_ref[0] = jnp.sum(jnp.where(hit1, rank, 0.0), axis=0, keepdims=True).astype(jnp.int32)
    carry_scr[...] = carry_scr[...] + jnp.sum(used, axis=1, keepdims=True)
    cnt_ref[...] = carry_scr[...]


def expert_ranks(e0, e1, tb=512):
    b, _, s = e0.shape
    tb = min(tb, s)
    tri = (lax.broadcasted_iota(jnp.int32, (tb, tb), 0)
           < lax.broadcasted_iota(jnp.int32, (tb, tb), 1)).astype(BF16)
    row = pl.BlockSpec((1, 1, tb), lambda i, j: (i, 0, j))
    ishape = jax.ShapeDtypeStruct((b, 1, s), jnp.int32)
    return pl.pallas_call(
        _rank_kernel, grid=(b, s // tb),
        in_specs=[row, row, pl.BlockSpec((tb, tb), lambda i, j: (0, 0))],
        out_specs=[row, row, pl.BlockSpec((N_EXPERTS, V7X_LANES), lambda i, j: (0, 0))],
        out_shape=[ishape, ishape, jax.ShapeDtypeStruct((N_EXPERTS, V7X_LANES), F32)],
        scratch_shapes=[pltpu.VMEM((N_EXPERTS, V7X_LANES), F32)],
        compiler_params=_cp(("arbitrary", "arbitrary")),
        name="expert_ranks",
    )(e0, e1, tri)


MOE_TM = 256


def _row_copy(src_hbm, row, dst, r, sem):
    def slabs(i):
        start = i * ROW_SLABS
        return pl.ds(start if isinstance(start, int) else pl.multiple_of(start, ROW_SLABS), ROW_SLABS)

    return pltpu.make_async_copy(src_hbm.at[slabs(row), :], dst.at[slabs(r), :], sem)


def _row_gather(idx_ref, base, src_hbm, dst, sem, n):
    def issue(r, carry):
        _row_copy(src_hbm, idx_ref[base + r], dst, r, sem).start()
        return carry

    lax.fori_loop(0, n, issue, 0, unroll=8)


def _wait_rows(buf, sem):
    pltpu.make_async_copy(buf, buf, sem).wait()


def _moe_group_kernel(te_ref, nu_ref, src_ref, x_hbm, wg_ref, wu_ref, wd_ref, y_ref,
                      xbuf, wg_bf, wu_bf, wd_bf, sem):
    i = pl.program_id(0)
    n_used = nu_ref[0]
    tm = xbuf.shape[1] // ROW_SLABS

    @pl.when(i == 0)
    def _():
        _row_gather(src_ref, 0, x_hbm, xbuf.at[0], sem.at[0], tm)

    @pl.when(jnp.logical_or(i == 0, te_ref[i] != te_ref[jnp.maximum(i - 1, 0)]))
    def _():
        wg_bf[...] = wg_ref[0, 0].astype(BF16)
        wu_bf[...] = wu_ref[0, 0].astype(BF16)
        wd_bf[...] = wd_ref[0, 0].astype(BF16)

    @pl.when(i < n_used)
    def _():
        slot = i % 2
        nxt = 1 - slot
        _wait_rows(xbuf.at[slot], sem.at[slot])
        base = jnp.minimum(i + 1, n_used - 1) * tm
        for r in range(tm):
            _row_copy(x_hbm, src_ref[base + r], xbuf.at[nxt], r, sem.at[nxt]).start()
        xb = jnp.concatenate([_load_row_slabs(xbuf.at[slot], tm, c).astype(BF16) for c in range(ROW_SLABS)],
                             axis=1)
        hg = jnp.dot(xb, wg_bf[...], preferred_element_type=F32)
        hu = jnp.dot(xb, wu_bf[...], preferred_element_type=F32)
        h = (hg / (1.0 + jnp.exp(-hg))) * hu
        _store_row_slabs(y_ref, jnp.dot(h.astype(BF16), wd_bf[...], preferred_element_type=F32))

    @pl.when(i == n_used - 1)
    def _():
        _wait_rows(xbuf.at[(i + 1) % 2], sem.at[(i + 1) % 2])

    @pl.when(i >= n_used)
    def _():
        y_ref[...] = jnp.zeros(y_ref.shape, y_ref.dtype)


def moe_grouped_ffn(x_rows, src, tile_expert, n_used, w_gate, w_up, w_down, layer):
    d = D_MODEL
    p = src.shape[0]
    tm = MOE_TM
    grid_spec = pltpu.PrefetchScalarGridSpec(
        num_scalar_prefetch=3,
        grid=(p // tm,),
        in_specs=[pl.BlockSpec(memory_space=pl.ANY),
                  pl.BlockSpec((1, 1, d, D_EXPERT), lambda i, te, nu, sr: (layer, te[i], 0, 0)),
                  pl.BlockSpec((1, 1, d, D_EXPERT), lambda i, te, nu, sr: (layer, te[i], 0, 0)),
                  pl.BlockSpec((1, 1, D_EXPERT, d), lambda i, te, nu, sr: (layer, te[i], 0, 0))],
        out_specs=pl.BlockSpec((tm * ROW_SLABS, V7X_LANES), lambda i, te, nu, sr: (i, 0)),
        scratch_shapes=[pltpu.VMEM((2, tm * ROW_SLABS, V7X_LANES), F32),
                        pltpu.VMEM((d, D_EXPERT), BF16), pltpu.VMEM((d, D_EXPERT), BF16),
                        pltpu.VMEM((D_EXPERT, d), BF16), pltpu.SemaphoreType.DMA((2,))],
    )
    return pl.pallas_call(
        _moe_group_kernel, grid_spec=grid_spec,
        out_shape=jax.ShapeDtypeStruct((p * ROW_SLABS, V7X_LANES), F32),
        compiler_params=_cp(("arbitrary",)),
        name="moe_grouped_ffn",
    )(tile_expert, n_used, src, x_rows, w_gate, w_up, w_down)


def _moe_combine_kernel(p0_ref, p1_ref, y_hbm, w0_ref, w1_ref, x_ref, gm_ref, o_ref, buf0, buf1, sem):
    i = pl.program_id(0)
    tm = buf0.shape[1] // ROW_SLABS

    def gather(tile, slot):
        _row_gather(p0_ref, tile * tm, y_hbm, buf0.at[slot], sem.at[0, slot], tm)
        _row_gather(p1_ref, tile * tm, y_hbm, buf1.at[slot], sem.at[1, slot], tm)

    @pl.when(i == 0)
    def _():
        gather(0, 0)

    @pl.when(i + 1 < pl.num_programs(0))
    def _():
        gather(i + 1, (i + 1) % 2)

    slot = i % 2
    _wait_rows(buf0.at[slot], sem.at[0, slot])
    _wait_rows(buf1.at[slot], sem.at[1, slot])
    w0 = jnp.broadcast_to(w0_ref[...], (tm, V7X_LANES))
    w1 = jnp.broadcast_to(w1_ref[...], (tm, V7X_LANES))
    for c in range(ROW_SLABS):
        cols = slice(c * V7X_LANES, (c + 1) * V7X_LANES)
        moe = w0 * _load_row_slabs(buf0.at[slot], tm, c) + w1 * _load_row_slabs(buf1.at[slot], tm, c)
        o_ref[:, cols] = x_ref[:, cols] + gm_ref[0, :, cols] * moe


def moe_combine(y, pos0, pos1, w0, w1, x, mod3, gate_idx, tm=256):
    b, s, d = x.shape
    t = b * s
    tm = min(tm, s)
    per_batch = s // tm
    grid_spec = pltpu.PrefetchScalarGridSpec(
        num_scalar_prefetch=2,
        grid=(t // tm,),
        in_specs=[pl.BlockSpec(memory_space=pl.ANY),
                  pl.BlockSpec((tm, 1), lambda i, p0, p1: (i, 0)),
                  pl.BlockSpec((tm, 1), lambda i, p0, p1: (i, 0)),
                  pl.BlockSpec((tm, d), lambda i, p0, p1: (i, 0)),
                  pl.BlockSpec((1, 1, d), lambda i, p0, p1: (i // per_batch, 0, gate_idx))],
        out_specs=pl.BlockSpec((tm, d), lambda i, p0, p1: (i, 0)),
        scratch_shapes=[pltpu.VMEM((2, tm * ROW_SLABS, V7X_LANES), F32),
                        pltpu.VMEM((2, tm * ROW_SLABS, V7X_LANES), F32),
                        pltpu.SemaphoreType.DMA((2, 2))],
    )
    out = pl.pallas_call(
        _moe_combine_kernel, grid_spec=grid_spec,
        out_shape=jax.ShapeDtypeStruct((t, d), F32),
        compiler_params=_cp(("arbitrary",)),
        name="moe_combine",
    )(pos0, pos1, y, w0.reshape(t, 1), w1.reshape(t, 1), x.reshape(t, d), mod3)
    return out.reshape(b, s, d)


def moe_sparse(a2, e0, e1, w0, w1, w_gate, w_up, w_down, layer, x, mod3, gate_idx):
    b, s, d = x.shape
    t = b * s
    tm = MOE_TM
    n_tiles = (2 * t + N_EXPERTS * (tm - 1)) // tm
    r0, r1, cnt = expert_ranks(e0, e1)
    counts = cnt[:, 0].astype(jnp.int32)
    padded = (counts + tm - 1) // tm * tm
    ends = jnp.cumsum(padded)
    offs = ends - padded
    e0f, e1f = e0.reshape(t), e1.reshape(t)
    pos0 = offs[e0f] + r0.reshape(t)
    pos1 = offs[e1f] + r1.reshape(t)
    tok = jnp.arange(t, dtype=jnp.int32)
    src = jnp.zeros((n_tiles * tm,), jnp.int32).at[jnp.concatenate([pos0, pos1])].set(
        jnp.concatenate([tok, tok]), unique_indices=True)
    tile_start = jnp.arange(n_tiles, dtype=jnp.int32) * tm
    tile_expert = jnp.minimum(jnp.sum((tile_start[:, None] >= ends[None, :]).astype(jnp.int32), axis=1),
                              N_EXPERTS - 1)
    n_used = (ends[-1:] // tm).astype(jnp.int32)
    y = moe_grouped_ffn(a2.reshape(t * ROW_SLABS, V7X_LANES), src, tile_expert, n_used, w_gate, w_up, w_down, layer)
    return moe_combine(y, pos0, pos1, w0, w1, x, mod3, gate_idx)


def _final_norm_kernel(x_ref, g_ref, o_ref):
    x = x_ref[0]
    o_ref[0] = x * lax.rsqrt(jnp.mean(x * x, axis=-1, keepdims=True) + EPS) * g_ref[...]


def final_norm(x, g, tm=512):
    b, s, d = x.shape
    tm = min(tm, s)
    return pl.pallas_call(
        _final_norm_kernel, grid=(b, s // tm),
        in_specs=[pl.BlockSpec((1, tm, d), lambda i, j: (i, j, 0)),
                  pl.BlockSpec((1, d), lambda i, j: (0, 0))],
        out_specs=pl.BlockSpec((1, tm, d), lambda i, j: (i, j, 0)),
        out_shape=jax.ShapeDtypeStruct((b, s, d), F32),
        compiler_params=_cp(("arbitrary", "arbitrary")),
        name="final_norm",
    )(x, g.reshape(1, d))


def _rot_cols(w):
    half = w.shape[-1] // 2
    return jnp.concatenate([-w[..., half:], w[..., :half]], axis=-1)


def _prep_layer(w_in, w_uq, w_ukv, w_o_mla, w_o_dil, w_o_sb, w_out):
    o1 = Q_LORA
    o2 = o1 + KV_LORA
    o3 = o2 + ROPE_DIM
    o4 = o3 + 3 * DIL_WIDTH
    o5 = o4 + 3 * SB_WIDTH
    w_kr = w_in[:, o2:o3]
    w_lat = jnp.concatenate([w_in[:, :o2], w_kr, _rot_cols(w_kr)], axis=1).astype(BF16)
    dil_scale = DIL_HEAD_DIM ** -0.5 * LOG2E
    def dil_cols(part, g):
        lo = o3 + part * DIL_WIDTH + g * DIL_GW
        return w_in[:, lo:lo + DIL_GW]

    w_dil = [jnp.concatenate([dil_cols(0, g) * dil_scale, dil_cols(1, g), dil_cols(2, g)], axis=1).astype(BF16)
             for g in range(DIL_GROUPS)]
    sb_scale = -(SB_HEAD_DIM ** -0.5) * LOG2E
    w_sb = jnp.concatenate([w_in[:, o4:o4 + SB_WIDTH] * sb_scale, w_in[:, o4 + SB_WIDTH:o5]],
                           axis=1).astype(BF16)
    w_gl = w_in[:, o5:].astype(BF16)
    q_scale = MLA_QK ** -0.5 * LOG2E
    wq = (w_uq * q_scale).reshape(Q_LORA, MLA_HEADS, MLA_QK).transpose(1, 0, 2)
    wq_h = jnp.concatenate([wq, _rot_cols(wq[..., NOPE_DIM:])], axis=-1).astype(BF16)
    wkv_h = w_ukv.reshape(KV_LORA, MLA_HEADS, NOPE_DIM + MLA_V_DIM).transpose(1, 0, 2).astype(BF16)
    return dict(w_lat=w_lat, w_dil=w_dil, w_sb=w_sb, w_gl=w_gl, wq_h=wq_h, wkv_h=wkv_h,
                w_o_mla=w_o_mla.astype(BF16), w_o_dil=w_o_dil.astype(BF16), w_o_sb=w_o_sb.astype(BF16),
                w_out=w_out.astype(BF16))


def kernel(x, c, positions, w_ada, b_ada, g_mix, g_moe, w_in, g_q, w_uq, g_kv, w_ukv, w_o_mla, w_o_dil,
           w_o_sb, w_out, w_router, b_router, w_gate, w_up, w_down, g_final):
    b, s, d = x.shape
    depth = w_ada.shape[0]
    pos_f = positions.astype(F32)
    cos, sin = rope_tables(pos_f.reshape(b, s, 1))
    mod = ada_mod(c, w_ada, b_ada)
    w_router_t = w_router.T
    for l in range(depth):
        p = _prep_layer(w_in[l], w_uq[l], w_ukv[l], w_o_mla[l], w_o_dil[l], w_o_sb[l], w_out[l])
        mod3 = mod[l].reshape(b, 1, 6 * d)

        a = norm_mod(x, g_mix[l], mod3, sc_idx=1, sh_idx=0)
        lat = project(a, p["w_lat"], tn=p["w_lat"].shape[1])
        dil = [project_by_residue(a, p["w_dil"][g], DIL_RATES[g]) for g in range(DIL_GROUPS)]
        sbp = project(a, p["w_sb"])
        gl = project(a, p["w_gl"])

        q, k, v = mla_project(lat, g_q[l], p["wq_h"], g_kv[l], p["wkv_h"], cos, sin)
        y_mla = mla_attention(q, k, v)
        dil_out = [dil_group_attention(dil[g], pos_f, g) for g in range(DIL_GROUPS)]
        y_sb = sb_attention(sbp)
        merged = merge_branches(y_mla, [o for o, _ in dil_out], [e for _, e in dil_out], y_sb, gl,
                                p["w_o_mla"], p["w_o_dil"], p["w_o_sb"])
        x = resid_project(merged, p["w_out"], x, mod3, gate_idx=2)

        a2, logits_t = norm_mod(x, g_moe[l], mod3, sc_idx=4, sh_idx=3, w_router_t=w_router_t)
        e0, e1, w0, w1 = router_top2(logits_t, b_router)
        x = moe_sparse(a2, e0, e1, w0, w1, w_gate, w_up, w_down, l, x, mod3, gate_idx=5)
    return final_norm(x, g_final)
```

```python
import functools
import math

import jax
import jax.numpy as jnp
from jax import lax
from jax.experimental import pallas as pl
from jax.experimental.pallas import tpu as pltpu

D_MODEL = 2048
EPS = 1e-6
MLA_HEADS = 8
Q_LORA = 512
KV_LORA = 512
NOPE_DIM = 128
ROPE_DIM = 64
MLA_V_DIM = 128
ROPE_THETA = 10000.0
MLA_QK = NOPE_DIM + ROPE_DIM
DIL_WINDOWS = (128, 512, 2048)
DIL_RATES = (1, 4, 16)
DIL_GROUPS = 3
DIL_HEADS = 4
DIL_HEAD_DIM = 128
DIL_SPAN = DIL_WINDOWS[0] // DIL_RATES[0]
DIL_GW = DIL_HEADS * DIL_HEAD_DIM
DIL_WIDTH = DIL_GROUPS * DIL_GW
ALIBI_MAX_BIAS = 8.0
SB_HEADS = 8
SB_HEAD_DIM = 128
SB_WIDTH = SB_HEADS * SB_HEAD_DIM
N_EXPERTS = 16
N_GROUPS = 4
EXPERTS_PER_GROUP = N_EXPERTS // N_GROUPS
D_EXPERT = 512

LOG2E = math.log2(math.e)
LN2 = math.log(2.0)
NEG_BIG = -1e30

V7X_LANES = 128
V7X_VMEM_BYTES = 64 * 1024 * 1024
VMEM_LIMIT = 56 * 1024 * 1024

F32 = jnp.float32
BF16 = jnp.bfloat16


def _cp(sem, vmem=VMEM_LIMIT):
    return pltpu.CompilerParams(dimension_semantics=sem, vmem_limit_bytes=vmem)


def _lane_tile(x, n):
    return x if n == 1 else jnp.concatenate([x] * n, axis=1)


def _nt_dot(a, b):
    return lax.dot_general(a, b, (((1,), (1,)), ((), ())), preferred_element_type=F32)


def _ada_kernel(c_ref, w_ref, b_ref, o_ref):
    c = c_ref[...]
    ca = c * (1.0 / (1.0 + jnp.exp(-c)))
    o_ref[0] = jnp.dot(ca, w_ref[0], preferred_element_type=F32,
                       precision=lax.Precision.HIGHEST) + b_ref[0]


def ada_mod(c, w_ada, b_ada, tn=1024):
    depth, d, n = w_ada.shape
    b = c.shape[0]
    return pl.pallas_call(
        _ada_kernel,
        grid=(depth, n // tn),
        in_specs=[pl.BlockSpec((b, d), lambda l, j: (0, 0)),
                  pl.BlockSpec((1, d, tn), lambda l, j: (l, 0, j)),
                  pl.BlockSpec((1, 1, tn), lambda l, j: (l, 0, j))],
        out_specs=pl.BlockSpec((1, b, tn), lambda l, j: (l, 0, j)),
        out_shape=jax.ShapeDtypeStruct((depth, b, n), F32),
        compiler_params=_cp(("arbitrary", "arbitrary")),
        name="ada_mod",
    )(c, w_ada, b_ada.reshape(depth, 1, n))


def _rope_table_kernel(pos_ref, inv_ref, cos_ref, sin_ref):
    ang = pos_ref[0] * inv_ref[...]
    cos_ref[0] = jnp.cos(ang)
    sin_ref[0] = jnp.sin(ang)


def rope_tables(pos_col, tm=512):
    b, s, _ = pos_col.shape
    tm = min(tm, s)
    half = ROPE_DIM // 2
    inv = ROPE_THETA ** (-jnp.arange(half, dtype=F32) / half)
    inv2 = jnp.concatenate([inv, inv]).reshape(1, ROPE_DIM)
    shp = jax.ShapeDtypeStruct((b, s, ROPE_DIM), F32)
    return pl.pallas_call(
        _rope_table_kernel,
        grid=(b, s // tm),
        in_specs=[pl.BlockSpec((1, tm, 1), lambda i, j: (i, j, 0)),
                  pl.BlockSpec((1, ROPE_DIM), lambda i, j: (0, 0))],
        out_specs=[pl.BlockSpec((1, tm, ROPE_DIM), lambda i, j: (i, j, 0))] * 2,
        out_shape=[shp, shp],
        compiler_params=_cp(("arbitrary", "arbitrary")),
        name="rope_tables",
    )(pos_col, inv2)


def _norm_mod(x, g, scale, shift):
    y = x * lax.rsqrt(jnp.mean(x * x, axis=-1, keepdims=True) + EPS)
    return (y * g) * (1.0 + scale) + shift


def _norm_kernel(x_ref, g_ref, sc_ref, sh_ref, o_ref):
    o_ref[0] = _norm_mod(x_ref[0], g_ref[...], sc_ref[0], sh_ref[0]).astype(o_ref.dtype)


def _norm_router_kernel(x_ref, g_ref, sc_ref, sh_ref, wr_ref, o_ref, lg_ref):
    a = _norm_mod(x_ref[0], g_ref[...], sc_ref[0], sh_ref[0])
    o_ref[0] = a
    lg_ref[0] = lax.dot_general(wr_ref[...], a, (((1,), (1,)), ((), ())),
                                preferred_element_type=F32, precision=lax.Precision.HIGHEST)


def norm_mod(x, g, mod3, sc_idx, sh_idx, w_router_t=None, tm=512):
    b, s, d = x.shape
    tm = min(tm, s)
    in_specs = [pl.BlockSpec((1, tm, d), lambda i, j: (i, j, 0)),
                pl.BlockSpec((1, d), lambda i, j: (0, 0)),
                pl.BlockSpec((1, 1, d), lambda i, j: (i, 0, sc_idx)),
                pl.BlockSpec((1, 1, d), lambda i, j: (i, 0, sh_idx))]
    if w_router_t is None:
        return pl.pallas_call(
            _norm_kernel, grid=(b, s // tm), in_specs=in_specs,
            out_specs=pl.BlockSpec((1, tm, d), lambda i, j: (i, j, 0)),
            out_shape=jax.ShapeDtypeStruct((b, s, d), BF16),
            compiler_params=_cp(("arbitrary", "arbitrary")), name="norm_mod",
        )(x, g.reshape(1, d), mod3, mod3)
    e = w_router_t.shape[0]
    return pl.pallas_call(
        _norm_router_kernel, grid=(b, s // tm),
        in_specs=in_specs + [pl.BlockSpec((e, d), lambda i, j: (0, 0))],
        out_specs=[pl.BlockSpec((1, tm, d), lambda i, j: (i, j, 0)),
                   pl.BlockSpec((1, e, tm), lambda i, j: (i, 0, j))],
        out_shape=[jax.ShapeDtypeStruct((b, s, d), F32),
                   jax.ShapeDtypeStruct((b, e, s), F32)],
        compiler_params=_cp(("arbitrary", "arbitrary")), name="norm_router",
    )(x, g.reshape(1, d), mod3, mod3, w_router_t)


def _matmul_kernel(a_ref, w_ref, o_ref):
    o_ref[0] = jnp.dot(a_ref[0], w_ref[...], preferred_element_type=F32).astype(o_ref.dtype)


def project(a, w, col0, n, tm=1024, tn=512, out_dtype=BF16):
    b, s, k = a.shape
    tm = min(tm, s)
    tn = min(tn, n)
    assert n % tn == 0 and s % tm == 0 and col0 % tn == 0
    cb0 = col0 // tn
    return pl.pallas_call(
        _matmul_kernel,
        grid=(b, s // tm, n // tn),
        in_specs=[pl.BlockSpec((1, tm, k), lambda i, j, c: (i, j, 0)),
                  pl.BlockSpec((k, tn), lambda i, j, c: (0, cb0 + c))],
        out_specs=pl.BlockSpec((1, tm, tn), lambda i, j, c: (i, j, c)),
        out_shape=jax.ShapeDtypeStruct((b, s, n), out_dtype),
        compiler_params=_cp(("arbitrary", "arbitrary", "arbitrary")),
        name="project",
    )(a, w)


def _matmul_residue_kernel(a_ref, w_ref, o_ref, res_scr, *, rate):
    res = jnp.dot(a_ref[0], w_ref[...], preferred_element_type=F32)
    slabs, rows, lanes = res_scr.shape
    for c in range(slabs):
        res_scr[c] = res[:, c * lanes:(c + 1) * lanes]
    n = rows // rate
    for rho in range(rate):
        for c in range(slabs):
            o_ref[0, rho, :, c * lanes:(c + 1) * lanes] = (
                res_scr[c, pl.ds(rho, n, stride=rate), :].astype(o_ref.dtype))


def project_by_residue(a, w, col0, n, rate, tm=1024, tn=512):
    b, s, k = a.shape
    if rate == 1:
        return project(a, w, col0, n, tm=tm, tn=tn).reshape(b, 1, s, n)
    tm = min(tm, s)
    assert n % tn == 0 and s % tm == 0 and tm % (rate * 16) == 0 and col0 % tn == 0
    cb0 = col0 // tn
    return pl.pallas_call(
        functools.partial(_matmul_residue_kernel, rate=rate),
        grid=(b, s // tm, n // tn),
        in_specs=[pl.BlockSpec((1, tm, k), lambda i, j, c: (i, j, 0)),
                  pl.BlockSpec((k, tn), lambda i, j, c: (0, cb0 + c))],
        out_specs=pl.BlockSpec((1, rate, tm // rate, tn), lambda i, j, c: (i, 0, j, c)),
        out_shape=jax.ShapeDtypeStruct((b, rate, s // rate, n), BF16),
        scratch_shapes=[pltpu.VMEM((tn // V7X_LANES, tm, V7X_LANES), F32)],
        compiler_params=_cp(("arbitrary", "arbitrary", "arbitrary")),
        name=f"project_by_residue_{rate}",
    )(a, w)


def _latent_norm(c, g):
    c = c.astype(F32)
    return (c * lax.rsqrt(jnp.mean(c * c, axis=-1, keepdims=True) + EPS) * g).astype(BF16)


def _mla_proj_kernel(cq_ref, ckv_ref, kr_ref, gq_ref, gkv_ref, wq_ref, wkv_ref, cos_ref, sin_ref,
                     q_ref, k_ref, v_ref):
    cos, sin = cos_ref[0], sin_ref[0]
    rq = _latent_norm(cq_ref[0], gq_ref[...])
    rkv = _latent_norm(ckv_ref[0], gkv_ref[...])
    kr = kr_ref[0].astype(F32)
    k_rope = (kr[:, :ROPE_DIM] * cos + kr[:, ROPE_DIM:] * sin).astype(k_ref.dtype)
    ones = jnp.ones((rq.shape[0], MLA_V_DIM), v_ref.dtype)
    for h in range(MLA_HEADS):
        r = jnp.dot(rq, wq_ref[h], preferred_element_type=F32)
        q_ref[0, h, :, :NOPE_DIM] = r[:, :NOPE_DIM].astype(q_ref.dtype)
        roped = r[:, NOPE_DIM:NOPE_DIM + ROPE_DIM] * cos + r[:, NOPE_DIM + ROPE_DIM:] * sin
        q_ref[0, h, :, NOPE_DIM:] = roped.astype(q_ref.dtype)
        r = jnp.dot(rkv, wkv_ref[h], preferred_element_type=F32)
        k_ref[0, h, :, :NOPE_DIM] = r[:, :NOPE_DIM].astype(k_ref.dtype)
        k_ref[0, h, :, NOPE_DIM:] = k_rope
        v_ref[0, h, :, :MLA_V_DIM] = r[:, NOPE_DIM:].astype(v_ref.dtype)
        v_ref[0, h, :, MLA_V_DIM:] = ones


def mla_project(lat, g_q, wq_h, g_kv, wkv_h, cos, sin, tm=512):
    b, s, _ = lat.shape
    tm = min(tm, s)
    h = MLA_HEADS
    kr_blk = (Q_LORA + KV_LORA) // (2 * ROPE_DIM)
    tab = pl.BlockSpec((1, tm, ROPE_DIM), lambda i, j: (i, j, 0))

    def const(shape):
        return pl.BlockSpec(shape, lambda i, j: (0,) * len(shape))

    def head_major(width):
        return pl.BlockSpec((1, h, tm, width), lambda i, j: (i, 0, j, 0))

    return pl.pallas_call(
        _mla_proj_kernel, grid=(b, s // tm),
        in_specs=[pl.BlockSpec((1, tm, Q_LORA), lambda i, j: (i, j, 0)),
                  pl.BlockSpec((1, tm, KV_LORA), lambda i, j: (i, j, 1)),
                  pl.BlockSpec((1, tm, 2 * ROPE_DIM), lambda i, j: (i, j, kr_blk)),
                  const((1, Q_LORA)), const((1, KV_LORA)),
                  const((h, Q_LORA, 256)), const((h, KV_LORA, 256)), tab, tab],
        out_specs=[head_major(MLA_QK), head_major(MLA_QK), head_major(2 * MLA_V_DIM)],
        out_shape=[jax.ShapeDtypeStruct((b, h, s, MLA_QK), BF16),
                   jax.ShapeDtypeStruct((b, h, s, MLA_QK), BF16),
                   jax.ShapeDtypeStruct((b, h, s, 2 * MLA_V_DIM), BF16)],
        compiler_params=_cp(("arbitrary", "arbitrary")), name="mla_project",
    )(lat, lat, lat, g_q.reshape(1, Q_LORA), g_kv.reshape(1, KV_LORA), wq_h, wkv_h, cos, sin)


def _mla_attn_kernel(q_ref, k_ref, v_ref, o_ref, s0_scr, s1_scr, m_scr, acc_scr, *, tq, tk):
    qi = pl.program_id(2)
    m_scr[...] = jnp.full(m_scr.shape, NEG_BIG, F32)
    acc_scr[...] = jnp.zeros(acc_scr.shape, F32)

    def scores(c, dst, row0=0):
        start = pl.multiple_of(c * tk, tk)
        dst[row0:, :] = _nt_dot(q_ref[0, 0, row0:, :], k_ref[0, 0, pl.ds(start, tk), :])

    def absorb(src, c, col_off=None, row0=0):
        start = pl.multiple_of(c * tk, tk)
        s = src[row0:, :]
        n = tq - row0
        if col_off is not None:
            row = row0 + lax.broadcasted_iota(jnp.int32, (n, tk), 0)
            col = col_off + lax.broadcasted_iota(jnp.int32, (n, tk), 1)
            s = jnp.where(col <= row, s, NEG_BIG)
        m_prev = m_scr[row0:, :]
        m_new = jnp.maximum(m_prev, jnp.max(s, axis=-1, keepdims=True))
        p = jnp.exp2(s - _lane_tile(m_new, tk // V7X_LANES))
        alpha = jnp.exp2(m_prev - m_new)
        pv = jnp.dot(p.astype(BF16), v_ref[0, 0, pl.ds(start, tk), :], preferred_element_type=F32)
        acc_scr[row0:, :] = _lane_tile(alpha, 2) * acc_scr[row0:, :] + pv
        m_scr[row0:, :] = m_new

    scores(0, s0_scr)

    def body(i, carry):
        scores(2 * i + 1, s1_scr)
        absorb(s0_scr, 2 * i)
        scores(2 * i + 2, s0_scr)
        absorb(s1_scr, 2 * i + 1)
        return carry

    lax.fori_loop(0, qi, body, 0)
    scores(2 * qi + 1, s1_scr, row0=tk)
    absorb(s0_scr, 2 * qi, col_off=0)
    absorb(s1_scr, 2 * qi + 1, col_off=tk, row0=tk)
    acc = acc_scr[...]
    o_ref[0] = (acc[:, :MLA_V_DIM] / acc[:, MLA_V_DIM:]).astype(o_ref.dtype)


def mla_attention(q, k, v, tq=1024):
    b, h, s, _ = q.shape
    tq = min(tq, s)
    tk = tq // 2
    return pl.pallas_call(
        functools.partial(_mla_attn_kernel, tq=tq, tk=tk),
        grid=(b, h, s // tq),
        in_specs=[pl.BlockSpec((1, 1, tq, MLA_QK), lambda i, j, t: (i, j, t, 0)),
                  pl.BlockSpec((1, 1, s, MLA_QK), lambda i, j, t: (i, j, 0, 0)),
                  pl.BlockSpec((1, 1, s, 2 * MLA_V_DIM), lambda i, j, t: (i, j, 0, 0))],
        out_specs=pl.BlockSpec((1, tq, MLA_V_DIM), lambda i, j, t: (i, t, j)),
        out_shape=jax.ShapeDtypeStruct((b, s, h * MLA_V_DIM), BF16),
        scratch_shapes=[pltpu.VMEM((tq, tk), F32), pltpu.VMEM((tq, tk), F32),
                        pltpu.VMEM((tq, V7X_LANES), F32), pltpu.VMEM((tq, 2 * MLA_V_DIM), F32)],
        compiler_params=_cp(("arbitrary", "arbitrary", "arbitrary")),
        name="mla_attention",
    )(q, k, v)


SB_SKIP_LOG2 = 150.0


def _sb_attn_kernel(q_ref, k_ref, v_ref, tri_ref, o_ref, c_scr, acc_scr, *, tq, hp):
    qi = pl.program_id(2)
    c_scr[...] = jnp.zeros(c_scr.shape, F32)
    acc_scr[...] = jnp.zeros(acc_scr.shape, F32)
    reps = tq // V7X_LANES

    def chunk(j, diagonal):
        start = pl.multiple_of(j * tq, tq)
        heads = range(hp)
        cols = [slice(hh * SB_HEAD_DIM, (hh + 1) * SB_HEAD_DIM) for hh in heads]
        if diagonal:
            before = (lax.broadcasted_iota(jnp.int32, (tq, tq), 1)
                      < lax.broadcasted_iota(jnp.int32, (tq, tq), 0))
        zn = [_nt_dot(q_ref[0, :, cols[hh]], k_ref[0, pl.ds(start, tq), cols[hh]]) for hh in heads]
        lk = [jnp.minimum(z, 0.0) - jnp.log2(1.0 + jnp.exp2(-jnp.abs(z))) for z in zn]
        if diagonal:
            lk = [jnp.where(before, x, 0.0) for x in lk]
        hi = [x.astype(BF16) for x in lk]
        lo = [(x - h.astype(F32)).astype(BF16) for x, h in zip(lk, hi)]
        suffix = [jnp.dot(jnp.concatenate([h, l], axis=1), tri_ref[...], preferred_element_type=F32)
                  for h, l in zip(hi, lo)]
        c = [c_scr[hh] for hh in heads]
        a = [jnp.exp2(sf - z + _lane_tile(cc, reps)) for sf, z, cc in zip(suffix, zn, c)]
        if diagonal:
            a = [jnp.where(before, x, 0.0) for x in a]
        for hh in heads:
            acc_scr[:, cols[hh]] += jnp.dot(a[hh].astype(BF16), v_ref[0, pl.ds(start, tq), cols[hh]],
                                            preferred_element_type=F32)
            c_scr[hh] = c[hh] + jnp.sum(lk[hh], axis=-1, keepdims=True)

    chunk(qi, True)

    def cond(state):
        j, go = state
        return jnp.logical_and(j >= 0, go > 0)

    def body(state):
        j, _ = state
        chunk(j, False)
        go = (jnp.max(c_scr[...]) > -SB_SKIP_LOG2).astype(jnp.int32)
        return j - 1, go

    lax.while_loop(cond, body, (qi - 1, jnp.int32(1)))
    o_ref[0] = acc_scr[...].astype(o_ref.dtype)


def sb_attention(qkv, tq=256, hp=4):
    b, s, _ = qkv.shape
    tq = min(tq, s)
    hw = hp * SB_HEAD_DIM
    nhb = SB_HEADS // hp
    tri = (lax.broadcasted_iota(jnp.int32, (tq, tq), 0)
           >= lax.broadcasted_iota(jnp.int32, (tq, tq), 1)).astype(BF16)
    tri2 = jnp.concatenate([tri, tri], axis=0)
    return pl.pallas_call(
        functools.partial(_sb_attn_kernel, tq=tq, hp=hp),
        grid=(b, nhb, s // tq),
        in_specs=[pl.BlockSpec((1, tq, hw), lambda i, j, t: (i, t, j)),
                  pl.BlockSpec((1, s, hw), lambda i, j, t: (i, 0, nhb + j)),
                  pl.BlockSpec((1, s, hw), lambda i, j, t: (i, 0, 2 * nhb + j)),
                  pl.BlockSpec((2 * tq, tq), lambda i, j, t: (0, 0))],
        out_specs=pl.BlockSpec((1, tq, hw), lambda i, j, t: (i, t, j)),
        out_shape=jax.ShapeDtypeStruct((b, s, SB_WIDTH), BF16),
        scratch_shapes=[pltpu.VMEM((hp, tq, V7X_LANES), F32), pltpu.VMEM((tq, hw), F32)],
        compiler_params=_cp(("arbitrary", "arbitrary", "arbitrary")),
        name="sb_attention",
    )(qkv, qkv, qkv, tri2)


def _dil_kernel(q_ref, kc_ref, vc_ref, kp_ref, vp_ref, pq_ref, pkc_ref, pkp_ref, sl_ref,
                o_ref, lse_ref, *, tq):
    ti = pl.program_id(2)
    w = DIL_SPAN
    a_idx = lax.broadcasted_iota(jnp.int32, (w, w), 0)
    c_idx = lax.broadcasted_iota(jnp.int32, (w, w), 1)
    own_ok = c_idx <= a_idx
    prev_tri = c_idx >= a_idx
    for sb in range(tq // w):
        rows = slice(sb * w, (sb + 1) * w)
        pq = pq_ref[0, 0, rows, :]
        pk_own = pkc_ref[0, 0, :, rows]
        if sb == 0:
            pk_prev = pkp_ref[0, 0]
            prev_ok = jnp.logical_and(prev_tri, ti > 0)
        else:
            pk_prev = pkc_ref[0, 0, :, (sb - 1) * w: sb * w]
            prev_ok = prev_tri
        dist_own = jnp.abs(pq - pk_own)
        dist_prev = jnp.abs(pq - pk_prev)
        heads = range(DIL_HEADS)
        cols = [slice(h * DIL_HEAD_DIM, (h + 1) * DIL_HEAD_DIM) for h in heads]
        prev_rows = slice((sb - 1) * w, sb * w)

        def k_prev(h):
            return kp_ref[0, :, cols[h]] if sb == 0 else kc_ref[0, prev_rows, cols[h]]

        def v_prev(h):
            return vp_ref[0, :, cols[h]] if sb == 0 else vc_ref[0, prev_rows, cols[h]]

        z_own = [_nt_dot(q_ref[0, rows, cols[h]], kc_ref[0, rows, cols[h]]) for h in heads]
        z_prev = [_nt_dot(q_ref[0, rows, cols[h]], k_prev(h)) for h in heads]
        s_own = [jnp.where(own_ok, z_own[h] - sl_ref[h] * dist_own, NEG_BIG) for h in heads]
        s_prev = [jnp.where(prev_ok, z_prev[h] - sl_ref[h] * dist_prev, NEG_BIG) for h in heads]
        m = [jnp.maximum(jnp.max(s_own[h], axis=-1, keepdims=True), jnp.max(s_prev[h], axis=-1, keepdims=True))
             for h in heads]
        p_own = [jnp.exp2(s_own[h] - m[h]) for h in heads]
        p_prev = [jnp.exp2(s_prev[h] - m[h]) for h in heads]
        den = [jnp.sum(p_own[h], axis=-1, keepdims=True) + jnp.sum(p_prev[h], axis=-1, keepdims=True)
               for h in heads]
        for h in heads:
            o = (jnp.dot(p_own[h].astype(BF16), vc_ref[0, rows, cols[h]], preferred_element_type=F32)
                 + jnp.dot(p_prev[h].astype(BF16), v_prev(h), preferred_element_type=F32)) / den[h]
            o_ref[0, rows, cols[h]] = o
            lse = (m[h] + jnp.log2(den[h])) * LN2
            lse_ref[0, rows, cols[h]] = jnp.broadcast_to(lse, (w, DIL_HEAD_DIM))


def dil_group_attention(dil_g, pos_f, g, tq=512):
    b, r, sr, c = dil_g.shape
    assert r == DIL_RATES[g]
    tq = min(tq, sr)
    w = DIL_SPAN
    assert sr % tq == 0 and tq % w == 0
    nsub = tq // w
    view = dil_g.reshape(b * r, sr, c)
    pos_r = pos_f.reshape(b, sr, r).transpose(0, 2, 1)
    pos_q = pos_r[..., None]
    pos_k = pos_r[:, :, None, :]
    n_h = DIL_GROUPS * DIL_HEADS
    slopes = 2.0 ** (-ALIBI_MAX_BIAS * jnp.arange(1, n_h + 1, dtype=F32) / n_h)
    slopes2 = (slopes * LOG2E)[g * DIL_HEADS:(g + 1) * DIL_HEADS]

    def cur(part):
        return pl.BlockSpec((1, tq, DIL_GW), lambda i, p, t: (i * r + p, t, part))

    def prev(part):
        return pl.BlockSpec((1, w, DIL_GW), lambda i, p, t: (i * r + p, jnp.maximum(t * nsub - 1, 0), part))

    out_spec = pl.BlockSpec((1, tq, DIL_GW), lambda i, p, t: (i * r + p, t, 0))
    out_shape = jax.ShapeDtypeStruct((b * r, sr, DIL_GW), F32)
    o, lse = pl.pallas_call(
        functools.partial(_dil_kernel, tq=tq),
        grid=(b, r, sr // tq),
        in_specs=[cur(0), cur(1), cur(2), prev(1), prev(2),
                  pl.BlockSpec((1, 1, tq, 1), lambda i, p, t: (i, p, t, 0)),
                  pl.BlockSpec((1, 1, 1, tq), lambda i, p, t: (i, p, 0, t)),
                  pl.BlockSpec((1, 1, 1, w), lambda i, p, t: (i, p, 0, jnp.maximum(t * nsub - 1, 0))),
                  pl.BlockSpec(memory_space=pltpu.SMEM)],
        out_specs=[out_spec, out_spec],
        out_shape=[out_shape, out_shape],
        compiler_params=_cp(("arbitrary", "arbitrary", "arbitrary")),
        name=f"dil_attention_g{g}",
    )(view, view, view, view, view, pos_q, pos_k, pos_k, slopes2)
    return o.reshape(b, r, sr, DIL_GW), lse.reshape(b, r, sr, DIL_GW)


def _token_order(ref, scr):
    _, r, n, width = ref.shape
    if r == 1:
        return ref[0, 0]
    slabs = width // V7X_LANES
    for rho in range(r):
        for c in range(slabs):
            scr[c, pl.ds(rho, n, stride=r), :] = ref[0, rho, :, c * V7X_LANES:(c + 1) * V7X_LANES]
    return jnp.concatenate([scr[c] for c in range(slabs)], axis=1)


def _merge_kernel(ym_ref, o0_ref, o1_ref, o2_ref, l0_ref, l1_ref, l2_ref, ys_ref,
                  g0_ref, g1_ref, g2_ref, wm_ref, wd_ref, ws_ref, out_ref, *scr):
    o0, o1, o2 = _token_order(o0_ref, scr[0]), _token_order(o1_ref, scr[1]), _token_order(o2_ref, scr[2])
    l0, l1, l2 = _token_order(l0_ref, scr[3]), _token_order(l1_ref, scr[4]), _token_order(l2_ref, scr[5])
    mx = jnp.maximum(jnp.maximum(l0, l1), l2)
    e0, e1, e2 = jnp.exp(l0 - mx), jnp.exp(l1 - mx), jnp.exp(l2 - mx)
    y_dil = (e0 * o0 + e1 * o1 + e2 * o2) / (e0 + e1 + e2)

    def sig(ref):
        return 1.0 / (1.0 + jnp.exp(-ref[0].astype(F32)))

    merged = (sig(g0_ref) * jnp.dot(ym_ref[0], wm_ref[...], preferred_element_type=F32)
              + sig(g1_ref) * jnp.dot(y_dil.astype(BF16), wd_ref[...], preferred_element_type=F32)
              + sig(g2_ref) * jnp.dot(ys_ref[0], ws_ref[...], preferred_element_type=F32))
    out_ref[0] = merged.astype(out_ref.dtype)


def merge_branches(y_mla, dil_o, dil_lse, y_sb, gates, w_o_mla, w_o_dil, w_o_sb, tm=256):
    b, s, _ = y_mla.shape
    d = D_MODEL
    tm = min(tm, s)

    def row(width, cb=0):
        return pl.BlockSpec((1, tm, width), lambda i, j: (i, j, cb))

    def by_residue(r):
        return pl.BlockSpec((1, r, tm // r, DIL_GW), lambda i, j: (i, 0, j, 0))

    def full(w):
        return pl.BlockSpec(w.shape, lambda i, j: (0, 0))

    dil_specs = [by_residue(r) for r in DIL_RATES]
    return pl.pallas_call(
        _merge_kernel, grid=(b, s // tm),
        in_specs=[row(y_mla.shape[2])] + dil_specs + dil_specs + [row(SB_WIDTH),
                  row(d, 0), row(d, 1), row(d, 2), full(w_o_mla), full(w_o_dil), full(w_o_sb)],
        out_specs=row(d),
        out_shape=jax.ShapeDtypeStruct((b, s, d), BF16),
        scratch_shapes=[pltpu.VMEM((DIL_GW // V7X_LANES, tm, V7X_LANES), F32)] * (2 * DIL_GROUPS),
        compiler_params=_cp(("arbitrary", "arbitrary")),
        name="merge_branches",
    )(y_mla, *dil_o, *dil_lse, y_sb, gates, gates, gates, w_o_mla, w_o_dil, w_o_sb)


def _resid_proj_kernel(a_ref, w_ref, x_ref, g_ref, o_ref):
    y = jnp.dot(a_ref[0], w_ref[...], preferred_element_type=F32)
    o_ref[0] = x_ref[0] + g_ref[0] * y


def resid_project(a, w, x, mod3, gate_idx, tm=512, tn=1024):
    b, s, k = a.shape
    n = w.shape[1]
    tm = min(tm, s)
    return pl.pallas_call(
        _resid_proj_kernel, grid=(b, s // tm, n // tn),
        in_specs=[pl.BlockSpec((1, tm, k), lambda i, j, c: (i, j, 0)),
                  pl.BlockSpec((k, tn), lambda i, j, c: (0, c)),
                  pl.BlockSpec((1, tm, tn), lambda i, j, c: (i, j, c)),
                  pl.BlockSpec((1, 1, tn), lambda i, j, c: (i, 0, gate_idx * (n // tn) + c))],
        out_specs=pl.BlockSpec((1, tm, tn), lambda i, j, c: (i, j, c)),
        out_shape=jax.ShapeDtypeStruct((b, s, n), F32),
        compiler_params=_cp(("arbitrary", "arbitrary", "arbitrary")),
        name="resid_project",
    )(a, w, x, mod3)


def _router_kernel(lg_ref, b_ref, e0_ref, e1_ref, w0_ref, w1_ref):
    lg = lg_ref[0]
    tm = lg.shape[1]
    sc = [1.0 / (1.0 + jnp.exp(-lg[e:e + 1, :])) for e in range(N_EXPERTS)]
    bi = [sc[e] + b_ref[e] for e in range(N_EXPERTS)]
    n = EXPERTS_PER_GROUP
    gscore = []
    for g in range(N_GROUPS):
        v = bi[g * n:(g + 1) * n]
        pair_max = None
        for a in range(n):
            for c in range(a + 1, n):
                pm = v[a] + v[c]
                pair_max = pm if pair_max is None else jnp.maximum(pair_max, pm)
        gscore.append(pair_max)
    best = gscore[0]
    gsel = jnp.zeros((1, tm), jnp.int32)
    for g in range(1, N_GROUPS):
        better = gscore[g] > best
        best = jnp.where(better, gscore[g], best)
        gsel = jnp.where(better, g, gsel)
    gb = [sum(jnp.where(gsel == g, bi[g * n + i], 0.0) for g in range(N_GROUPS)) for i in range(n)]
    gs = [sum(jnp.where(gsel == g, sc[g * n + i], 0.0) for g in range(N_GROUPS)) for i in range(n)]
    sel = []
    for i in range(n):
        beaten = jnp.zeros((1, tm), jnp.int32)
        for j in range(n):
            if j == i:
                continue
            wins = (gb[j] > gb[i]) if j > i else (gb[j] >= gb[i])
            beaten = beaten + wins.astype(jnp.int32)
        sel.append(beaten < 2)
    den = sum(jnp.where(sel[i], gs[i], 0.0) for i in range(n))
    i0 = jnp.where(sel[0], 0, jnp.where(sel[1], 1, 2))
    i1 = jnp.where(sel[3], 3, jnp.where(sel[2], 2, 1))
    e0_ref[0] = gsel * n + i0
    e1_ref[0] = gsel * n + i1
    w0_ref[0] = sum(jnp.where(i0 == i, gs[i], 0.0) for i in range(n)) / den
    w1_ref[0] = sum(jnp.where(i1 == i, gs[i], 0.0) for i in range(n)) / den


def router_top2(logits_t, b_router, tm=1024):
    b, e, s = logits_t.shape
    tm = min(tm, s)
    row = pl.BlockSpec((1, 1, tm), lambda i, j: (i, 0, j))
    ishape = jax.ShapeDtypeStruct((b, 1, s), jnp.int32)
    fshape = jax.ShapeDtypeStruct((b, 1, s), F32)
    return pl.pallas_call(
        _router_kernel, grid=(b, s // tm),
        in_specs=[pl.BlockSpec((1, e, tm), lambda i, j: (i, 0, j)),
                  pl.BlockSpec(memory_space=pltpu.SMEM)],
        out_specs=[row, row, row, row],
        out_shape=[ishape, ishape, fshape, fshape],
        compiler_params=_cp(("arbitrary", "arbitrary")),
        name="router_top2",
    )(logits_t, b_router)


def _rank_kernel(e0_ref, e1_ref, tri_ref, r0_ref, r1_ref, cnt_ref, carry_scr):
    @pl.when(jnp.logical_and(pl.program_id(0) == 0, pl.program_id(1) == 0))
    def _():
        carry_scr[...] = jnp.zeros(carry_scr.shape, F32)

    e0, e1 = e0_ref[0], e1_ref[0]
    tb = e0.shape[1]
    eid = lax.broadcasted_iota(jnp.int32, (N_EXPERTS, tb), 0)
    hit0 = eid == e0
    hit1 = eid == e1
    used = jnp.where(jnp.logical_or(hit0, hit1), 1.0, 0.0)
    before = jnp.dot(used.astype(BF16), tri_ref[...], preferred_element_type=F32)
    rank = before + _lane_tile(carry_scr[...], tb // V7X_LANES)
    r0_ref[0] = jnp.sum(jnp.where(hit0, rank, 0.0), axis=0, keepdims=True).astype(jnp.int32)
    r1_ref[0] = jnp.sum(jnp.where(hit1, rank, 0.0), axis=0, keepdims=True).astype(jnp.int32)
    carry_scr[...] = carry_scr[...] + jnp.sum(used, axis=1, keepdims=True)
    cnt_ref[...] = carry_scr[...]


def expert_ranks(e0, e1, tb=512):
    b, _, s = e0.shape
    tb = min(tb, s)
    tri = (lax.broadcasted_iota(jnp.int32, (tb, tb), 0)
           < lax.broadcasted_iota(jnp.int32, (tb, tb), 1)).astype(BF16)
    row = pl.BlockSpec((1, 1, tb), lambda i, j: (i, 0, j))
    ishape = jax.ShapeDtypeStruct((b, 1, s), jnp.int32)
    return pl.pallas_call(
        _rank_kernel, grid=(b, s // tb),
        in_specs=[row, row, pl.BlockSpec((tb, tb), lambda i, j: (0, 0))],
        out_specs=[row, row, pl.BlockSpec((N_EXPERTS, V7X_LANES), lambda i, j: (0, 0))],
        out_shape=[ishape, ishape, jax.ShapeDtypeStruct((N_EXPERTS, V7X_LANES), F32)],
        scratch_shapes=[pltpu.VMEM((N_EXPERTS, V7X_LANES), F32)],
        compiler_params=_cp(("arbitrary", "arbitrary")),
        name="expert_ranks",
    )(e0, e1, tri)


MOE_TM = 256


def _row_copy(src_hbm, row, dst, r, sem):
    return pltpu.make_async_copy(src_hbm.at[pl.ds(row, 1), :], dst.at[pl.ds(r, 1), :], sem)


def _row_gather(idx_ref, base, src_hbm, dst, sem, n):
    def issue(r, carry):
        _row_copy(src_hbm, idx_ref[base + r], dst, r, sem).start()
        return carry

    lax.fori_loop(0, n, issue, 0, unroll=8)


def _wait_rows(buf, sem):
    pltpu.make_async_copy(buf, buf, sem).wait()


def _moe_group_kernel(te_ref, nu_ref, src_ref, x_hbm, wg_ref, wu_ref, wd_ref, y_ref,
                      xbuf, wg_bf, wu_bf, wd_bf, sem):
    i = pl.program_id(0)
    n_used = nu_ref[0]
    tm = xbuf.shape[1]

    @pl.when(i == 0)
    def _():
        _row_gather(src_ref, 0, x_hbm, xbuf.at[0], sem.at[0], tm)

    @pl.when(jnp.logical_or(i == 0, te_ref[i] != te_ref[jnp.maximum(i - 1, 0)]))
    def _():
        wg_bf[...] = wg_ref[0, 0].astype(BF16)
        wu_bf[...] = wu_ref[0, 0].astype(BF16)
        wd_bf[...] = wd_ref[0, 0].astype(BF16)

    @pl.when(i < n_used)
    def _():
        slot = i % 2
        nxt = 1 - slot
        _wait_rows(xbuf.at[slot], sem.at[slot])
        base = jnp.minimum(i + 1, n_used - 1) * tm
        for r in range(tm):
            _row_copy(x_hbm, src_ref[base + r], xbuf.at[nxt], r, sem.at[nxt]).start()
        xb = xbuf[slot].astype(BF16)
        hg = jnp.dot(xb, wg_bf[...], preferred_element_type=F32)
        hu = jnp.dot(xb, wu_bf[...], preferred_element_type=F32)
        h = (hg / (1.0 + jnp.exp(-hg))) * hu
        y_ref[...] = jnp.dot(h.astype(BF16), wd_bf[...], preferred_element_type=F32)

    @pl.when(i == n_used - 1)
    def _():
        _wait_rows(xbuf.at[(i + 1) % 2], sem.at[(i + 1) % 2])

    @pl.when(i >= n_used)
    def _():
        y_ref[...] = jnp.zeros(y_ref.shape, y_ref.dtype)


def moe_grouped_ffn(x_rows, src, tile_expert, n_used, w_gate, w_up, w_down, layer):
    d = D_MODEL
    p = src.shape[0]
    tm = MOE_TM
    grid_spec = pltpu.PrefetchScalarGridSpec(
        num_scalar_prefetch=3,
        grid=(p // tm,),
        in_specs=[pl.BlockSpec(memory_space=pl.ANY),
                  pl.BlockSpec((1, 1, d, D_EXPERT), lambda i, te, nu, sr: (layer, te[i], 0, 0)),
                  pl.BlockSpec((1, 1, d, D_EXPERT), lambda i, te, nu, sr: (layer, te[i], 0, 0)),
                  pl.BlockSpec((1, 1, D_EXPERT, d), lambda i, te, nu, sr: (layer, te[i], 0, 0))],
        out_specs=pl.BlockSpec((tm, d), lambda i, te, nu, sr: (i, 0)),
        scratch_shapes=[pltpu.VMEM((2, tm, d), F32),
                        pltpu.VMEM((d, D_EXPERT), BF16), pltpu.VMEM((d, D_EXPERT), BF16),
                        pltpu.VMEM((D_EXPERT, d), BF16), pltpu.SemaphoreType.DMA((2,))],
    )
    return pl.pallas_call(
        _moe_group_kernel, grid_spec=grid_spec,
        out_shape=jax.ShapeDtypeStruct((p, d), F32),
        compiler_params=_cp(("arbitrary",)),
        name="moe_grouped_ffn",
    )(tile_expert, n_used, src, x_rows, w_gate, w_up, w_down)


def _moe_combine_kernel(p0_ref, p1_ref, y_hbm, w0_ref, w1_ref, x_ref, gm_ref, o_ref, buf0, buf1, sem):
    i = pl.program_id(0)
    tm = buf0.shape[1]

    def gather(tile, slot):
        _row_gather(p0_ref, tile * tm, y_hbm, buf0.at[slot], sem.at[0, slot], tm)
        _row_gather(p1_ref, tile * tm, y_hbm, buf1.at[slot], sem.at[1, slot], tm)

    @pl.when(i == 0)
    def _():
        gather(0, 0)

    @pl.when(i + 1 < pl.num_programs(0))
    def _():
        gather(i + 1, (i + 1) % 2)

    slot = i % 2
    _wait_rows(buf0.at[slot], sem.at[0, slot])
    _wait_rows(buf1.at[slot], sem.at[1, slot])
    moe = w0_ref[...] * buf0[slot] + w1_ref[...] * buf1[slot]
    o_ref[...] = x_ref[...] + gm_ref[0] * moe


def moe_combine(y, pos0, pos1, w0, w1, x, mod3, gate_idx, tm=256):
    b, s, d = x.shape
    t = b * s
    tm = min(tm, s)
    per_batch = s // tm
    grid_spec = pltpu.PrefetchScalarGridSpec(
        num_scalar_prefetch=2,
        grid=(t // tm,),
        in_specs=[pl.BlockSpec(memory_space=pl.ANY),
                  pl.BlockSpec((tm, 1), lambda i, p0, p1: (i, 0)),
                  pl.BlockSpec((tm, 1), lambda i, p0, p1: (i, 0)),
                  pl.BlockSpec((tm, d), lambda i, p0, p1: (i, 0)),
                  pl.BlockSpec((1, 1, d), lambda i, p0, p1: (i // per_batch, 0, gate_idx))],
        out_specs=pl.BlockSpec((tm, d), lambda i, p0, p1: (i, 0)),
        scratch_shapes=[pltpu.VMEM((2, tm, d), F32), pltpu.VMEM((2, tm, d), F32),
                        pltpu.SemaphoreType.DMA((2, 2))],
    )
    out = pl.pallas_call(
        _moe_combine_kernel, grid_spec=grid_spec,
        out_shape=jax.ShapeDtypeStruct((t, d), F32),
        compiler_params=_cp(("arbitrary",)),
        name="moe_combine",
    )(pos0, pos1, y, w0.reshape(t, 1), w1.reshape(t, 1), x.reshape(t, d), mod3)
    return out.reshape(b, s, d)


def moe_sparse(a2, e0, e1, w0, w1, w_gate, w_up, w_down, layer, x, mod3, gate_idx):
    b, s, d = x.shape
    t = b * s
    tm = MOE_TM
    n_tiles = (2 * t + N_EXPERTS * (tm - 1)) // tm
    r0, r1, cnt = expert_ranks(e0, e1)
    counts = cnt[:, 0].astype(jnp.int32)
    padded = (counts + tm - 1) // tm * tm
    ends = jnp.cumsum(padded)
    offs = ends - padded
    e0f, e1f = e0.reshape(t), e1.reshape(t)
    pos0 = offs[e0f] + r0.reshape(t)
    pos1 = offs[e1f] + r1.reshape(t)
    tok = jnp.arange(t, dtype=jnp.int32)
    src = jnp.zeros((n_tiles * tm,), jnp.int32).at[jnp.concatenate([pos0, pos1])].set(
        jnp.concatenate([tok, tok]), unique_indices=True)
    tile_start = jnp.arange(n_tiles, dtype=jnp.int32) * tm
    tile_expert = jnp.minimum(jnp.sum((tile_start[:, None] >= ends[None, :]).astype(jnp.int32), axis=1),
                              N_EXPERTS - 1)
    n_used = (ends[-1:] // tm).astype(jnp.int32)
    y = moe_grouped_ffn(a2.reshape(t, d), src, tile_expert, n_used, w_gate, w_up, w_down, layer)
    return moe_combine(y, pos0, pos1, w0, w1, x, mod3, gate_idx)


def _final_norm_kernel(x_ref, g_ref, o_ref):
    x = x_ref[0]
    o_ref[0] = x * lax.rsqrt(jnp.mean(x * x, axis=-1, keepdims=True) + EPS) * g_ref[...]


def final_norm(x, g, tm=512):
    b, s, d = x.shape
    tm = min(tm, s)
    return pl.pallas_call(
        _final_norm_kernel, grid=(b, s // tm),
        in_specs=[pl.BlockSpec((1, tm, d), lambda i, j: (i, j, 0)),
                  pl.BlockSpec((1, d), lambda i, j: (0, 0))],
        out_specs=pl.BlockSpec((1, tm, d), lambda i, j: (i, j, 0)),
        out_shape=jax.ShapeDtypeStruct((b, s, d), F32),
        compiler_params=_cp(("arbitrary", "arbitrary")),
        name="final_norm",
    )(x, g.reshape(1, d))


def _rot_cols(w):
    half = w.shape[-1] // 2
    return jnp.concatenate([-w[..., half:], w[..., :half]], axis=-1)


LAT_WIDTH = Q_LORA + KV_LORA + 2 * ROPE_DIM
PACKED_COLS = dict(gates=(0, 3 * D_MODEL), sb=(3 * D_MODEL, 3 * SB_WIDTH))
for _g in range(DIL_GROUPS):
    PACKED_COLS[f"dil{_g}"] = (3 * D_MODEL + 3 * SB_WIDTH + _g * 3 * DIL_GW, 3 * DIL_GW)
PACKED_COLS["lat"] = (3 * D_MODEL + 3 * SB_WIDTH + 3 * DIL_WIDTH, LAT_WIDTH)


def _prep_layer(w_in, w_uq, w_ukv, w_o_mla, w_o_dil, w_o_sb, w_out):
    o1 = Q_LORA
    o2 = o1 + KV_LORA
    o3 = o2 + ROPE_DIM
    o4 = o3 + 3 * DIL_WIDTH
    o5 = o4 + 3 * SB_WIDTH
    w_kr = w_in[:, o2:o3]
    dil_scale = DIL_HEAD_DIM ** -0.5 * LOG2E

    def dil_cols(part, g):
        lo = o3 + part * DIL_WIDTH + g * DIL_GW
        return w_in[:, lo:lo + DIL_GW]

    sb_scale = -(SB_HEAD_DIM ** -0.5) * LOG2E
    pieces = [w_in[:, o5:], w_in[:, o4:o4 + SB_WIDTH] * sb_scale, w_in[:, o4 + SB_WIDTH:o5]]
    for g in range(DIL_GROUPS):
        pieces += [dil_cols(0, g) * dil_scale, dil_cols(1, g), dil_cols(2, g)]
    pieces += [w_in[:, :o2], w_kr, _rot_cols(w_kr)]
    w_pack = jnp.concatenate(pieces, axis=1).astype(BF16)
    q_scale = MLA_QK ** -0.5 * LOG2E
    wq = (w_uq * q_scale).reshape(Q_LORA, MLA_HEADS, MLA_QK).transpose(1, 0, 2)
    wq_h = jnp.concatenate([wq, _rot_cols(wq[..., NOPE_DIM:])], axis=-1).astype(BF16)
    wkv_h = w_ukv.reshape(KV_LORA, MLA_HEADS, NOPE_DIM + MLA_V_DIM).transpose(1, 0, 2).astype(BF16)
    return dict(w_pack=w_pack, wq_h=wq_h, wkv_h=wkv_h,
                w_o_mla=w_o_mla.astype(BF16), w_o_dil=w_o_dil.astype(BF16), w_o_sb=w_o_sb.astype(BF16),
                w_out=w_out.astype(BF16))


def kernel(x, c, positions, w_ada, b_ada, g_mix, g_moe, w_in, g_q, w_uq, g_kv, w_ukv, w_o_mla, w_o_dil,
           w_o_sb, w_out, w_router, b_router, w_gate, w_up, w_down, g_final):
    b, s, d = x.shape
    depth = w_ada.shape[0]
    pos_f = positions.astype(F32)
    cos, sin = rope_tables(pos_f.reshape(b, s, 1))
    mod = ada_mod(c, w_ada, b_ada)
    w_router_t = w_router.T
    for l in range(depth):
        p = _prep_layer(w_in[l], w_uq[l], w_ukv[l], w_o_mla[l], w_o_dil[l], w_o_sb[l], w_out[l])
        mod3 = mod[l].reshape(b, 1, 6 * d)

        a = norm_mod(x, g_mix[l], mod3, sc_idx=1, sh_idx=0)
        wp = p["w_pack"]
        lat = project(a, wp, *PACKED_COLS["lat"], tn=LAT_WIDTH)
        dil = [project_by_residue(a, wp, *PACKED_COLS[f"dil{g}"], DIL_RATES[g]) for g in range(DIL_GROUPS)]
        sbp = project(a, wp, *PACKED_COLS["sb"], tn=1024)
        gl = project(a, wp, *PACKED_COLS["gates"], tn=1024)

        q, k, v = mla_project(lat, g_q[l], p["wq_h"], g_kv[l], p["wkv_h"], cos, sin)
        y_mla = mla_attention(q, k, v)
        dil_out = [dil_group_attention(dil[g], pos_f, g) for g in range(DIL_GROUPS)]
        y_sb = sb_attention(sbp)
        merged = merge_branches(y_mla, [o for o, _ in dil_out], [e for _, e in dil_out], y_sb, gl,
                                p["w_o_mla"], p["w_o_dil"], p["w_o_sb"])
        x = resid_project(merged, p["w_out"], x, mod3, gate_idx=2)

        a2, logits_t = norm_mod(x, g_moe[l], mod3, sc_idx=4, sh_idx=3, w_router_t=w_router_t)
        e0, e1, w0, w1 = router_top2(logits_t, b_router)
        x = moe_sparse(a2, e0, e1, w0, w1, w_gate, w_up, w_down, l, x, mod3, gate_idx=5)
    return final_norm(x, g_final)
```

```python
import functools
import math

import jax
import jax.numpy as jnp
from jax import lax
from jax.experimental import pallas as pl
from jax.experimental.pallas import tpu as pltpu

D_MODEL = 2048
EPS = 1e-6
MLA_HEADS = 8
Q_LORA = 512
KV_LORA = 512
NOPE_DIM = 128
ROPE_DIM = 64
MLA_V_DIM = 128
ROPE_THETA = 10000.0
MLA_QK = NOPE_DIM + ROPE_DIM
DIL_WINDOWS = (128, 512, 2048)
DIL_RATES = (1, 4, 16)
DIL_GROUPS = 3
DIL_HEADS = 4
DIL_HEAD_DIM = 128
DIL_SPAN = DIL_WINDOWS[0] // DIL_RATES[0]
DIL_GW = DIL_HEADS * DIL_HEAD_DIM
DIL_WIDTH = DIL_GROUPS * DIL_GW
ALIBI_MAX_BIAS = 8.0
SB_HEADS = 8
SB_HEAD_DIM = 128
SB_WIDTH = SB_HEADS * SB_HEAD_DIM
N_EXPERTS = 16
N_GROUPS = 4
EXPERTS_PER_GROUP = N_EXPERTS // N_GROUPS
D_EXPERT = 512

LOG2E = math.log2(math.e)
LN2 = math.log(2.0)
NEG_BIG = -1e30

V7X_LANES = 128
V7X_VMEM_BYTES = 64 * 1024 * 1024
VMEM_LIMIT = 56 * 1024 * 1024

F32 = jnp.float32
BF16 = jnp.bfloat16


def _cp(sem, vmem=VMEM_LIMIT):
    return pltpu.CompilerParams(dimension_semantics=sem, vmem_limit_bytes=vmem)


def _lane_tile(x, n):
    return x if n == 1 else jnp.concatenate([x] * n, axis=1)


def _nt_dot(a, b):
    return lax.dot_general(a, b, (((1,), (1,)), ((), ())), preferred_element_type=F32)


def _ada_kernel(c_ref, w_ref, b_ref, o_ref):
    c = c_ref[...]
    ca = c * (1.0 / (1.0 + jnp.exp(-c)))
    o_ref[0] = jnp.dot(ca, w_ref[0], preferred_element_type=F32,
                       precision=lax.Precision.HIGHEST) + b_ref[0]


def ada_mod(c, w_ada, b_ada, tn=1024):
    depth, d, n = w_ada.shape
    b = c.shape[0]
    return pl.pallas_call(
        _ada_kernel,
        grid=(depth, n // tn),
        in_specs=[pl.BlockSpec((b, d), lambda l, j: (0, 0)),
                  pl.BlockSpec((1, d, tn), lambda l, j: (l, 0, j)),
                  pl.BlockSpec((1, 1, tn), lambda l, j: (l, 0, j))],
        out_specs=pl.BlockSpec((1, b, tn), lambda l, j: (l, 0, j)),
        out_shape=jax.ShapeDtypeStruct((depth, b, n), F32),
        compiler_params=_cp(("arbitrary", "arbitrary")),
        name="ada_mod",
    )(c, w_ada, b_ada.reshape(depth, 1, n))


def _rope_table_kernel(pos_ref, inv_ref, cos_ref, sin_ref):
    ang = pos_ref[0] * inv_ref[...]
    cos_ref[0] = jnp.cos(ang)
    sin_ref[0] = jnp.sin(ang)


def rope_tables(pos_col, tm=512):
    b, s, _ = pos_col.shape
    tm = min(tm, s)
    half = ROPE_DIM // 2
    inv = ROPE_THETA ** (-jnp.arange(half, dtype=F32) / half)
    inv2 = jnp.concatenate([inv, inv]).reshape(1, ROPE_DIM)
    shp = jax.ShapeDtypeStruct((b, s, ROPE_DIM), F32)
    return pl.pallas_call(
        _rope_table_kernel,
        grid=(b, s // tm),
        in_specs=[pl.BlockSpec((1, tm, 1), lambda i, j: (i, j, 0)),
                  pl.BlockSpec((1, ROPE_DIM), lambda i, j: (0, 0))],
        out_specs=[pl.BlockSpec((1, tm, ROPE_DIM), lambda i, j: (i, j, 0))] * 2,
        out_shape=[shp, shp],
        compiler_params=_cp(("arbitrary", "arbitrary")),
        name="rope_tables",
    )(pos_col, inv2)


def _norm_mod(x, g, scale, shift):
    y = x * lax.rsqrt(jnp.mean(x * x, axis=-1, keepdims=True) + EPS)
    return (y * g) * (1.0 + scale) + shift


def _norm_kernel(x_ref, g_ref, sc_ref, sh_ref, o_ref):
    o_ref[0] = _norm_mod(x_ref[0], g_ref[...], sc_ref[0], sh_ref[0]).astype(o_ref.dtype)


def _norm_router_kernel(x_ref, g_ref, sc_ref, sh_ref, wr_ref, o_ref, lg_ref):
    a = _norm_mod(x_ref[0], g_ref[...], sc_ref[0], sh_ref[0])
    o_ref[0] = a
    lg_ref[0] = lax.dot_general(wr_ref[...], a, (((1,), (1,)), ((), ())),
                                preferred_element_type=F32, precision=lax.Precision.HIGHEST)


def norm_mod(x, g, mod3, sc_idx, sh_idx, w_router_t=None, tm=512):
    b, s, d = x.shape
    tm = min(tm, s)
    in_specs = [pl.BlockSpec((1, tm, d), lambda i, j: (i, j, 0)),
                pl.BlockSpec((1, d), lambda i, j: (0, 0)),
                pl.BlockSpec((1, 1, d), lambda i, j: (i, 0, sc_idx)),
                pl.BlockSpec((1, 1, d), lambda i, j: (i, 0, sh_idx))]
    if w_router_t is None:
        return pl.pallas_call(
            _norm_kernel, grid=(b, s // tm), in_specs=in_specs,
            out_specs=pl.BlockSpec((1, tm, d), lambda i, j: (i, j, 0)),
            out_shape=jax.ShapeDtypeStruct((b, s, d), BF16),
            compiler_params=_cp(("arbitrary", "arbitrary")), name="norm_mod",
        )(x, g.reshape(1, d), mod3, mod3)
    e = w_router_t.shape[0]
    return pl.pallas_call(
        _norm_router_kernel, grid=(b, s // tm),
        in_specs=in_specs + [pl.BlockSpec((e, d), lambda i, j: (0, 0))],
        out_specs=[pl.BlockSpec((1, tm, d), lambda i, j: (i, j, 0)),
                   pl.BlockSpec((1, e, tm), lambda i, j: (i, 0, j))],
        out_shape=[jax.ShapeDtypeStruct((b, s, d), F32),
                   jax.ShapeDtypeStruct((b, e, s), F32)],
        compiler_params=_cp(("arbitrary", "arbitrary")), name="norm_router",
    )(x, g.reshape(1, d), mod3, mod3, w_router_t)


def _matmul_kernel(a_ref, w_ref, o_ref):
    o_ref[0] = jnp.dot(a_ref[0], w_ref[...], preferred_element_type=F32).astype(o_ref.dtype)


def project(a, w, col0, n, tm=1024, tn=512, out_dtype=BF16):
    b, s, k = a.shape
    tm = min(tm, s)
    tn = min(tn, n)
    assert n % tn == 0 and s % tm == 0 and col0 % tn == 0
    cb0 = col0 // tn
    return pl.pallas_call(
        _matmul_kernel,
        grid=(b, s // tm, n // tn),
        in_specs=[pl.BlockSpec((1, tm, k), lambda i, j, c: (i, j, 0)),
                  pl.BlockSpec((k, tn), lambda i, j, c: (0, cb0 + c))],
        out_specs=pl.BlockSpec((1, tm, tn), lambda i, j, c: (i, j, c)),
        out_shape=jax.ShapeDtypeStruct((b, s, n), out_dtype),
        compiler_params=_cp(("arbitrary", "arbitrary", "arbitrary")),
        name="project",
    )(a, w)


def _matmul_residue_kernel(a_ref, w_ref, o_ref, res_scr, *, rate):
    res = jnp.dot(a_ref[0], w_ref[...], preferred_element_type=F32)
    slabs, rows, lanes = res_scr.shape
    for c in range(slabs):
        res_scr[c] = res[:, c * lanes:(c + 1) * lanes]
    n = rows // rate
    for rho in range(rate):
        for c in range(slabs):
            o_ref[0, rho, :, c * lanes:(c + 1) * lanes] = (
                res_scr[c, pl.ds(rho, n, stride=rate), :].astype(o_ref.dtype))


def project_by_residue(a, w, col0, n, rate, tm=1024, tn=512):
    b, s, k = a.shape
    if rate == 1:
        return project(a, w, col0, n, tm=tm, tn=tn).reshape(b, 1, s, n)
    tm = min(tm, s)
    assert n % tn == 0 and s % tm == 0 and tm % (rate * 16) == 0 and col0 % tn == 0
    cb0 = col0 // tn
    return pl.pallas_call(
        functools.partial(_matmul_residue_kernel, rate=rate),
        grid=(b, s // tm, n // tn),
        in_specs=[pl.BlockSpec((1, tm, k), lambda i, j, c: (i, j, 0)),
                  pl.BlockSpec((k, tn), lambda i, j, c: (0, cb0 + c))],
        out_specs=pl.BlockSpec((1, rate, tm // rate, tn), lambda i, j, c: (i, 0, j, c)),
        out_shape=jax.ShapeDtypeStruct((b, rate, s // rate, n), BF16),
        scratch_shapes=[pltpu.VMEM((tn // V7X_LANES, tm, V7X_LANES), F32)],
        compiler_params=_cp(("arbitrary", "arbitrary", "arbitrary")),
        name=f"project_by_residue_{rate}",
    )(a, w)


def _latent_norm(c, g):
    c = c.astype(F32)
    return (c * lax.rsqrt(jnp.mean(c * c, axis=-1, keepdims=True) + EPS) * g).astype(BF16)


def _mla_proj_kernel(cq_ref, ckv_ref, kr_ref, gq_ref, gkv_ref, wq_ref, wkv_ref, cos_ref, sin_ref,
                     q_ref, k_ref, v_ref):
    cos, sin = cos_ref[0], sin_ref[0]
    rq = _latent_norm(cq_ref[0], gq_ref[...])
    rkv = _latent_norm(ckv_ref[0], gkv_ref[...])
    kr = kr_ref[0].astype(F32)
    k_rope = (kr[:, :ROPE_DIM] * cos + kr[:, ROPE_DIM:] * sin).astype(k_ref.dtype)
    ones = jnp.ones((rq.shape[0], MLA_V_DIM), v_ref.dtype)
    for h in range(MLA_HEADS):
        r = jnp.dot(rq, wq_ref[h], preferred_element_type=F32)
        q_ref[0, h, :, :NOPE_DIM] = r[:, :NOPE_DIM].astype(q_ref.dtype)
        roped = r[:, NOPE_DIM:NOPE_DIM + ROPE_DIM] * cos + r[:, NOPE_DIM + ROPE_DIM:] * sin
        q_ref[0, h, :, NOPE_DIM:] = roped.astype(q_ref.dtype)
        r = jnp.dot(rkv, wkv_ref[h], preferred_element_type=F32)
        k_ref[0, h, :, :NOPE_DIM] = r[:, :NOPE_DIM].astype(k_ref.dtype)
        k_ref[0, h, :, NOPE_DIM:] = k_rope
        v_ref[0, h, :, :MLA_V_DIM] = r[:, NOPE_DIM:].astype(v_ref.dtype)
        v_ref[0, h, :, MLA_V_DIM:] = ones


def mla_project(lat, kr, g_q, wq_h, g_kv, wkv_h, cos, sin, tm=512):
    b, s, _ = lat.shape
    tm = min(tm, s)
    h = MLA_HEADS
    tab = pl.BlockSpec((1, tm, ROPE_DIM), lambda i, j: (i, j, 0))

    def const(shape):
        return pl.BlockSpec(shape, lambda i, j: (0,) * len(shape))

    def head_major(width):
        return pl.BlockSpec((1, h, tm, width), lambda i, j: (i, 0, j, 0))

    return pl.pallas_call(
        _mla_proj_kernel, grid=(b, s // tm),
        in_specs=[pl.BlockSpec((1, tm, Q_LORA), lambda i, j: (i, j, 0)),
                  pl.BlockSpec((1, tm, KV_LORA), lambda i, j: (i, j, 1)),
                  pl.BlockSpec((1, tm, 2 * ROPE_DIM), lambda i, j: (i, j, 0)),
                  const((1, Q_LORA)), const((1, KV_LORA)),
                  const((h, Q_LORA, 256)), const((h, KV_LORA, 256)), tab, tab],
        out_specs=[head_major(MLA_QK), head_major(MLA_QK), head_major(2 * MLA_V_DIM)],
        out_shape=[jax.ShapeDtypeStruct((b, h, s, MLA_QK), BF16),
                   jax.ShapeDtypeStruct((b, h, s, MLA_QK), BF16),
                   jax.ShapeDtypeStruct((b, h, s, 2 * MLA_V_DIM), BF16)],
        compiler_params=_cp(("arbitrary", "arbitrary")), name="mla_project",
    )(lat, lat, kr, g_q.reshape(1, Q_LORA), g_kv.reshape(1, KV_LORA), wq_h, wkv_h, cos, sin)


def _mla_attn_kernel(q_ref, k_ref, v_ref, o_ref, s0_scr, s1_scr, m_scr, acc_scr, *, tq, tk):
    qi = pl.program_id(2)
    m_scr[...] = jnp.full(m_scr.shape, NEG_BIG, F32)
    acc_scr[...] = jnp.zeros(acc_scr.shape, F32)

    def scores(c, dst, row0=0):
        start = pl.multiple_of(c * tk, tk)
        dst[row0:, :] = _nt_dot(q_ref[0, 0, row0:, :], k_ref[0, 0, pl.ds(start, tk), :])

    def absorb(src, c, col_off=None, row0=0):
        start = pl.multiple_of(c * tk, tk)
        s = src[row0:, :]
        n = tq - row0
        if col_off is not None:
            row = row0 + lax.broadcasted_iota(jnp.int32, (n, tk), 0)
            col = col_off + lax.broadcasted_iota(jnp.int32, (n, tk), 1)
            s = jnp.where(col <= row, s, NEG_BIG)
        m_prev = m_scr[row0:, :]
        m_new = jnp.maximum(m_prev, jnp.max(s, axis=-1, keepdims=True))
        p = jnp.exp2(s - _lane_tile(m_new, tk // V7X_LANES))
        alpha = jnp.exp2(m_prev - m_new)
        pv = jnp.dot(p.astype(BF16), v_ref[0, 0, pl.ds(start, tk), :], preferred_element_type=F32)
        acc_scr[row0:, :] = _lane_tile(alpha, 2) * acc_scr[row0:, :] + pv
        m_scr[row0:, :] = m_new

    scores(0, s0_scr)

    def body(i, carry):
        scores(2 * i + 1, s1_scr)
        absorb(s0_scr, 2 * i)
        scores(2 * i + 2, s0_scr)
        absorb(s1_scr, 2 * i + 1)
        return carry

    lax.fori_loop(0, qi, body, 0)
    scores(2 * qi + 1, s1_scr, row0=tk)
    absorb(s0_scr, 2 * qi, col_off=0)
    absorb(s1_scr, 2 * qi + 1, col_off=tk, row0=tk)
    acc = acc_scr[...]
    o_ref[0] = (acc[:, :MLA_V_DIM] / acc[:, MLA_V_DIM:]).astype(o_ref.dtype)


def mla_attention(q, k, v, tq=1024):
    b, h, s, _ = q.shape
    tq = min(tq, s)
    tk = tq // 2
    return pl.pallas_call(
        functools.partial(_mla_attn_kernel, tq=tq, tk=tk),
        grid=(b, h, s // tq),
        in_specs=[pl.BlockSpec((1, 1, tq, MLA_QK), lambda i, j, t: (i, j, t, 0)),
                  pl.BlockSpec((1, 1, s, MLA_QK), lambda i, j, t: (i, j, 0, 0)),
                  pl.BlockSpec((1, 1, s, 2 * MLA_V_DIM), lambda i, j, t: (i, j, 0, 0))],
        out_specs=pl.BlockSpec((1, tq, MLA_V_DIM), lambda i, j, t: (i, t, j)),
        out_shape=jax.ShapeDtypeStruct((b, s, h * MLA_V_DIM), BF16),
        scratch_shapes=[pltpu.VMEM((tq, tk), F32), pltpu.VMEM((tq, tk), F32),
                        pltpu.VMEM((tq, V7X_LANES), F32), pltpu.VMEM((tq, 2 * MLA_V_DIM), F32)],
        compiler_params=_cp(("arbitrary", "arbitrary", "arbitrary")),
        name="mla_attention",
    )(q, k, v)


SB_SKIP_LOG2 = 150.0


def _sb_attn_kernel(q_ref, k_ref, v_ref, tri_ref, o_ref, c_scr, acc_scr, *, tq, hp):
    qi = pl.program_id(2)
    c_scr[...] = jnp.zeros(c_scr.shape, F32)
    acc_scr[...] = jnp.zeros(acc_scr.shape, F32)
    reps = tq // V7X_LANES

    def chunk(j, diagonal):
        start = pl.multiple_of(j * tq, tq)
        heads = range(hp)
        cols = [slice(hh * SB_HEAD_DIM, (hh + 1) * SB_HEAD_DIM) for hh in heads]
        if diagonal:
            before = (lax.broadcasted_iota(jnp.int32, (tq, tq), 1)
                      < lax.broadcasted_iota(jnp.int32, (tq, tq), 0))
        zn = [_nt_dot(q_ref[0, :, cols[hh]], k_ref[0, pl.ds(start, tq), cols[hh]]) for hh in heads]
        lk = [jnp.minimum(z, 0.0) - jnp.log2(1.0 + jnp.exp2(-jnp.abs(z))) for z in zn]
        if diagonal:
            lk = [jnp.where(before, x, 0.0) for x in lk]
        hi = [x.astype(BF16) for x in lk]
        lo = [(x - h.astype(F32)).astype(BF16) for x, h in zip(lk, hi)]
        suffix = [jnp.dot(jnp.concatenate([h, l], axis=1), tri_ref[...], preferred_element_type=F32)
                  for h, l in zip(hi, lo)]
        c = [c_scr[hh] for hh in heads]
        a = [jnp.exp2(sf - z + _lane_tile(cc, reps)) for sf, z, cc in zip(suffix, zn, c)]
        if diagonal:
            a = [jnp.where(before, x, 0.0) for x in a]
        for hh in heads:
            acc_scr[:, cols[hh]] += jnp.dot(a[hh].astype(BF16), v_ref[0, pl.ds(start, tq), cols[hh]],
                                            preferred_element_type=F32)
            c_scr[hh] = c[hh] + jnp.sum(lk[hh], axis=-1, keepdims=True)

    chunk(qi, True)

    def cond(state):
        j, go = state
        return jnp.logical_and(j >= 0, go > 0)

    def body(state):
        j, _ = state
        chunk(j, False)
        go = (jnp.max(c_scr[...]) > -SB_SKIP_LOG2).astype(jnp.int32)
        return j - 1, go

    lax.while_loop(cond, body, (qi - 1, jnp.int32(1)))
    o_ref[0] = acc_scr[...].astype(o_ref.dtype)


def sb_attention(qkv, tq=256, hp=4):
    b, s, _ = qkv.shape
    tq = min(tq, s)
    hw = hp * SB_HEAD_DIM
    nhb = SB_HEADS // hp
    tri = (lax.broadcasted_iota(jnp.int32, (tq, tq), 0)
           >= lax.broadcasted_iota(jnp.int32, (tq, tq), 1)).astype(BF16)
    tri2 = jnp.concatenate([tri, tri], axis=0)
    return pl.pallas_call(
        functools.partial(_sb_attn_kernel, tq=tq, hp=hp),
        grid=(b, nhb, s // tq),
        in_specs=[pl.BlockSpec((1, tq, hw), lambda i, j, t: (i, t, j)),
                  pl.BlockSpec((1, s, hw), lambda i, j, t: (i, 0, nhb + j)),
                  pl.BlockSpec((1, s, hw), lambda i, j, t: (i, 0, 2 * nhb + j)),
                  pl.BlockSpec((2 * tq, tq), lambda i, j, t: (0, 0))],
        out_specs=pl.BlockSpec((1, tq, hw), lambda i, j, t: (i, t, j)),
        out_shape=jax.ShapeDtypeStruct((b, s, SB_WIDTH), BF16),
        scratch_shapes=[pltpu.VMEM((hp, tq, V7X_LANES), F32), pltpu.VMEM((tq, hw), F32)],
        compiler_params=_cp(("arbitrary", "arbitrary", "arbitrary")),
        name="sb_attention",
    )(qkv, qkv, qkv, tri2)


def _dil_kernel(q_ref, kc_ref, vc_ref, kp_ref, vp_ref, pq_ref, pkc_ref, pkp_ref, sl_ref,
                o_ref, lse_ref, *, tq):
    ti = pl.program_id(2)
    w = DIL_SPAN
    a_idx = lax.broadcasted_iota(jnp.int32, (w, w), 0)
    c_idx = lax.broadcasted_iota(jnp.int32, (w, w), 1)
    own_ok = c_idx <= a_idx
    prev_tri = c_idx >= a_idx
    for sb in range(tq // w):
        rows = slice(sb * w, (sb + 1) * w)
        pq = pq_ref[0, 0, rows, :]
        pk_own = pkc_ref[0, 0, :, rows]
        if sb == 0:
            pk_prev = pkp_ref[0, 0]
            prev_ok = jnp.logical_and(prev_tri, ti > 0)
        else:
            pk_prev = pkc_ref[0, 0, :, (sb - 1) * w: sb * w]
            prev_ok = prev_tri
        dist_own = jnp.abs(pq - pk_own)
        dist_prev = jnp.abs(pq - pk_prev)
        heads = range(DIL_HEADS)
        cols = [slice(h * DIL_HEAD_DIM, (h + 1) * DIL_HEAD_DIM) for h in heads]
        prev_rows = slice((sb - 1) * w, sb * w)

        def k_prev(h):
            return kp_ref[0, :, cols[h]] if sb == 0 else kc_ref[0, prev_rows, cols[h]]

        def v_prev(h):
            return vp_ref[0, :, cols[h]] if sb == 0 else vc_ref[0, prev_rows, cols[h]]

        z_own = [_nt_dot(q_ref[0, rows, cols[h]], kc_ref[0, rows, cols[h]]) for h in heads]
        z_prev = [_nt_dot(q_ref[0, rows, cols[h]], k_prev(h)) for h in heads]
        s_own = [jnp.where(own_ok, z_own[h] - sl_ref[h] * dist_own, NEG_BIG) for h in heads]
        s_prev = [jnp.where(prev_ok, z_prev[h] - sl_ref[h] * dist_prev, NEG_BIG) for h in heads]
        m = [jnp.maximum(jnp.max(s_own[h], axis=-1, keepdims=True), jnp.max(s_prev[h], axis=-1, keepdims=True))
             for h in heads]
        p_own = [jnp.exp2(s_own[h] - m[h]) for h in heads]
        p_prev = [jnp.exp2(s_prev[h] - m[h]) for h in heads]
        den = [jnp.sum(p_own[h], axis=-1, keepdims=True) + jnp.sum(p_prev[h], axis=-1, keepdims=True)
               for h in heads]
        for h in heads:
            o = (jnp.dot(p_own[h].astype(BF16), vc_ref[0, rows, cols[h]], preferred_element_type=F32)
                 + jnp.dot(p_prev[h].astype(BF16), v_prev(h), preferred_element_type=F32)) / den[h]
            o_ref[0, rows, cols[h]] = o
            lse = (m[h] + jnp.log2(den[h])) * LN2
            lse_ref[0, rows, cols[h]] = jnp.broadcast_to(lse, (w, DIL_HEAD_DIM))


def dil_group_attention(dil_g, pos_f, g, tq=512):
    b, r, sr, c = dil_g.shape
    assert r == DIL_RATES[g]
    tq = min(tq, sr)
    w = DIL_SPAN
    assert sr % tq == 0 and tq % w == 0
    nsub = tq // w
    view = dil_g.reshape(b * r, sr, c)
    pos_r = pos_f.reshape(b, sr, r).transpose(0, 2, 1)
    pos_q = pos_r[..., None]
    pos_k = pos_r[:, :, None, :]
    n_h = DIL_GROUPS * DIL_HEADS
    slopes = 2.0 ** (-ALIBI_MAX_BIAS * jnp.arange(1, n_h + 1, dtype=F32) / n_h)
    slopes2 = (slopes * LOG2E)[g * DIL_HEADS:(g + 1) * DIL_HEADS]

    def cur(part):
        return pl.BlockSpec((1, tq, DIL_GW), lambda i, p, t: (i * r + p, t, part))

    def prev(part):
        return pl.BlockSpec((1, w, DIL_GW), lambda i, p, t: (i * r + p, jnp.maximum(t * nsub - 1, 0), part))

    out_spec = pl.BlockSpec((1, tq, DIL_GW), lambda i, p, t: (i * r + p, t, 0))
    out_shape = jax.ShapeDtypeStruct((b * r, sr, DIL_GW), F32)
    o, lse = pl.pallas_call(
        functools.partial(_dil_kernel, tq=tq),
        grid=(b, r, sr // tq),
        in_specs=[cur(0), cur(1), cur(2), prev(1), prev(2),
                  pl.BlockSpec((1, 1, tq, 1), lambda i, p, t: (i, p, t, 0)),
                  pl.BlockSpec((1, 1, 1, tq), lambda i, p, t: (i, p, 0, t)),
                  pl.BlockSpec((1, 1, 1, w), lambda i, p, t: (i, p, 0, jnp.maximum(t * nsub - 1, 0))),
                  pl.BlockSpec(memory_space=pltpu.SMEM)],
        out_specs=[out_spec, out_spec],
        out_shape=[out_shape, out_shape],
        compiler_params=_cp(("arbitrary", "arbitrary", "arbitrary")),
        name=f"dil_attention_g{g}",
    )(view, view, view, view, view, pos_q, pos_k, pos_k, slopes2)
    return o.reshape(b, r, sr, DIL_GW), lse.reshape(b, r, sr, DIL_GW)


def _token_order(ref, scr):
    _, r, n, width = ref.shape
    if r == 1:
        return ref[0, 0]
    slabs = width // V7X_LANES
    for rho in range(r):
        for c in range(slabs):
            scr[c, pl.ds(rho, n, stride=r), :] = ref[0, rho, :, c * V7X_LANES:(c + 1) * V7X_LANES]
    return jnp.concatenate([scr[c] for c in range(slabs)], axis=1)


def _merge_kernel(ym_ref, o0_ref, o1_ref, o2_ref, l0_ref, l1_ref, l2_ref, ys_ref,
                  g0_ref, g1_ref, g2_ref, wm_ref, wd_ref, ws_ref, out_ref, *scr):
    o0, o1, o2 = _token_order(o0_ref, scr[0]), _token_order(o1_ref, scr[1]), _token_order(o2_ref, scr[2])
    l0, l1, l2 = _token_order(l0_ref, scr[3]), _token_order(l1_ref, scr[4]), _token_order(l2_ref, scr[5])
    mx = jnp.maximum(jnp.maximum(l0, l1), l2)
    e0, e1, e2 = jnp.exp(l0 - mx), jnp.exp(l1 - mx), jnp.exp(l2 - mx)
    y_dil = (e0 * o0 + e1 * o1 + e2 * o2) / (e0 + e1 + e2)

    def sig(ref):
        return 1.0 / (1.0 + jnp.exp(-ref[0].astype(F32)))

    merged = (sig(g0_ref) * jnp.dot(ym_ref[0], wm_ref[...], preferred_element_type=F32)
              + sig(g1_ref) * jnp.dot(y_dil.astype(BF16), wd_ref[...], preferred_element_type=F32)
              + sig(g2_ref) * jnp.dot(ys_ref[0], ws_ref[...], preferred_element_type=F32))
    out_ref[0] = merged.astype(out_ref.dtype)


def merge_branches(y_mla, dil_o, dil_lse, y_sb, gates, w_o_mla, w_o_dil, w_o_sb, tm=256):
    b, s, _ = y_mla.shape
    d = D_MODEL
    tm = min(tm, s)

    def row(width, cb=0):
        return pl.BlockSpec((1, tm, width), lambda i, j: (i, j, cb))

    def by_residue(r):
        return pl.BlockSpec((1, r, tm // r, DIL_GW), lambda i, j: (i, 0, j, 0))

    def full(w):
        return pl.BlockSpec(w.shape, lambda i, j: (0, 0))

    dil_specs = [by_residue(r) for r in DIL_RATES]
    return pl.pallas_call(
        _merge_kernel, grid=(b, s // tm),
        in_specs=[row(y_mla.shape[2])] + dil_specs + dil_specs + [row(SB_WIDTH),
                  row(d, 0), row(d, 1), row(d, 2), full(w_o_mla), full(w_o_dil), full(w_o_sb)],
        out_specs=row(d),
        out_shape=jax.ShapeDtypeStruct((b, s, d), BF16),
        scratch_shapes=[pltpu.VMEM((DIL_GW // V7X_LANES, tm, V7X_LANES), F32)] * (2 * DIL_GROUPS),
        compiler_params=_cp(("arbitrary", "arbitrary")),
        name="merge_branches",
    )(y_mla, *dil_o, *dil_lse, y_sb, gates, gates, gates, w_o_mla, w_o_dil, w_o_sb)


def _resid_proj_kernel(a_ref, w_ref, x_ref, g_ref, o_ref):
    y = jnp.dot(a_ref[0], w_ref[...], preferred_element_type=F32)
    o_ref[0] = x_ref[0] + g_ref[0] * y


def resid_project(a, w, x, mod3, gate_idx, tm=512, tn=1024):
    b, s, k = a.shape
    n = w.shape[1]
    tm = min(tm, s)
    return pl.pallas_call(
        _resid_proj_kernel, grid=(b, s // tm, n // tn),
        in_specs=[pl.BlockSpec((1, tm, k), lambda i, j, c: (i, j, 0)),
                  pl.BlockSpec((k, tn), lambda i, j, c: (0, c)),
                  pl.BlockSpec((1, tm, tn), lambda i, j, c: (i, j, c)),
                  pl.BlockSpec((1, 1, tn), lambda i, j, c: (i, 0, gate_idx * (n // tn) + c))],
        out_specs=pl.BlockSpec((1, tm, tn), lambda i, j, c: (i, j, c)),
        out_shape=jax.ShapeDtypeStruct((b, s, n), F32),
        compiler_params=_cp(("arbitrary", "arbitrary", "arbitrary")),
        name="resid_project",
    )(a, w, x, mod3)


def _router_kernel(lg_ref, b_ref, e0_ref, e1_ref, w0_ref, w1_ref):
    lg = lg_ref[0]
    tm = lg.shape[1]
    sc = [1.0 / (1.0 + jnp.exp(-lg[e:e + 1, :])) for e in range(N_EXPERTS)]
    bi = [sc[e] + b_ref[e] for e in range(N_EXPERTS)]
    n = EXPERTS_PER_GROUP
    gscore = []
    for g in range(N_GROUPS):
        v = bi[g * n:(g + 1) * n]
        pair_max = None
        for a in range(n):
            for c in range(a + 1, n):
                pm = v[a] + v[c]
                pair_max = pm if pair_max is None else jnp.maximum(pair_max, pm)
        gscore.append(pair_max)
    best = gscore[0]
    gsel = jnp.zeros((1, tm), jnp.int32)
    for g in range(1, N_GROUPS):
        better = gscore[g] > best
        best = jnp.where(better, gscore[g], best)
        gsel = jnp.where(better, g, gsel)
    gb = [sum(jnp.where(gsel == g, bi[g * n + i], 0.0) for g in range(N_GROUPS)) for i in range(n)]
    gs = [sum(jnp.where(gsel == g, sc[g * n + i], 0.0) for g in range(N_GROUPS)) for i in range(n)]
    sel = []
    for i in range(n):
        beaten = jnp.zeros((1, tm), jnp.int32)
        for j in range(n):
            if j == i:
                continue
            wins = (gb[j] > gb[i]) if j > i else (gb[j] >= gb[i])
            beaten = beaten + wins.astype(jnp.int32)
        sel.append(beaten < 2)
    den = sum(jnp.where(sel[i], gs[i], 0.0) for i in range(n))
    i0 = jnp.where(sel[0], 0, jnp.where(sel[1], 1, 2))
    i1 = jnp.where(sel[3], 3, jnp.where(sel[2], 2, 1))
    e0_ref[0] = gsel * n + i0
    e1_ref[0] = gsel * n + i1
    w0_ref[0] = sum(jnp.where(i0 == i, gs[i], 0.0) for i in range(n)) / den
    w1_ref[0] = sum(jnp.where(i1 == i, gs[i], 0.0) for i in range(n)) / den


def router_top2(logits_t, b_router, tm=1024):
    b, e, s = logits_t.shape
    tm = min(tm, s)
    row = pl.BlockSpec((1, 1, tm), lambda i, j: (i, 0, j))
    ishape = jax.ShapeDtypeStruct((b, 1, s), jnp.int32)
    fshape = jax.ShapeDtypeStruct((b, 1, s), F32)
    return pl.pallas_call(
        _router_kernel, grid=(b, s // tm),
        in_specs=[pl.BlockSpec((1, e, tm), lambda i, j: (i, 0, j)),
                  pl.BlockSpec(memory_space=pltpu.SMEM)],
        out_specs=[row, row, row, row],
        out_shape=[ishape, ishape, fshape, fshape],
        compiler_params=_cp(("arbitrary", "arbitrary")),
        name="router_top2",
    )(logits_t, b_router)


def _rank_kernel(e0_ref, e1_ref, tri_ref, r0_ref, r1_ref, cnt_ref, carry_scr):
    @pl.when(jnp.logical_and(pl.program_id(0) == 0, pl.program_id(1) == 0))
    def _():
        carry_scr[...] = jnp.zeros(carry_scr.shape, F32)

    e0, e1 = e0_ref[0], e1_ref[0]
    tb = e0.shape[1]
    eid = lax.broadcasted_iota(jnp.int32, (N_EXPERTS, tb), 0)
    hit0 = eid == e0
    hit1 = eid == e1
    used = jnp.where(jnp.logical_or(hit0, hit1), 1.0, 0.0)
    before = jnp.dot(used.astype(BF16), tri_ref[...], preferred_element_type=F32)
    rank = before + _lane_tile(carry_scr[...], tb // V7X_LANES)
    r0_ref[0] = jnp.sum(jnp.where(hit0, rank, 0.0), axis=0, keepdims=True).astype(jnp.int32)
    r1_ref[0] = jnp.sum(jnp.where(hit1, rank, 0.0), axis=0, keepdims=True).astype(jnp.int32)
    carry_scr[...] = carry_scr[...] + jnp.sum(used, axis=1, keepdims=True)
    cnt_ref[...] = carry_scr[...]


def expert_ranks(e0, e1, tb=512):
    b, _, s = e0.shape
    tb = min(tb, s)
    tri = (lax.broadcasted_iota(jnp.int32, (tb, tb), 0)
           < lax.broadcasted_iota(jnp.int32, (tb, tb), 1)).astype(BF16)
    row = pl.BlockSpec((1, 1, tb), lambda i, j: (i, 0, j))
    ishape = jax.ShapeDtypeStruct((b, 1, s), jnp.int32)
    return pl.pallas_call(
        _rank_kernel, grid=(b, s // tb),
        in_specs=[row, row, pl.BlockSpec((tb, tb), lambda i, j: (0, 0))],
        out_specs=[row, row, pl.BlockSpec((N_EXPERTS, V7X_LANES), lambda i, j: (0, 0))],
        out_shape=[ishape, ishape, jax.ShapeDtypeStruct((N_EXPERTS, V7X_LANES), F32)],
        scratch_shapes=[pltpu.VMEM((N_EXPERTS, V7X_LANES), F32)],
        compiler_params=_cp(("arbitrary", "arbitrary")),
        name="expert_ranks",
    )(e0, e1, tri)


MOE_TM = 256


def _row_copy(src_hbm, row, dst, r, sem):
    return pltpu.make_async_copy(src_hbm.at[pl.ds(row, 1), :], dst.at[pl.ds(r, 1), :], sem)


def _row_gather(idx_ref, base, src_hbm, dst, sem, n):
    def issue(r, carry):
        _row_copy(src_hbm, idx_ref[base + r], dst, r, sem).start()
        return carry

    lax.fori_loop(0, n, issue, 0, unroll=8)


def _wait_rows(buf, sem):
    pltpu.make_async_copy(buf, buf, sem).wait()


def _moe_group_kernel(te_ref, nu_ref, src_ref, x_hbm, wg_ref, wu_ref, wd_ref, y_ref,
                      xbuf, wg_bf, wu_bf, wd_bf, sem):
    i = pl.program_id(0)
    n_used = nu_ref[0]
    tm = xbuf.shape[1]

    @pl.when(i == 0)
    def _():
        _row_gather(src_ref, 0, x_hbm, xbuf.at[0], sem.at[0], tm)

    @pl.when(jnp.logical_or(i == 0, te_ref[i] != te_ref[jnp.maximum(i - 1, 0)]))
    def _():
        wg_bf[...] = wg_ref[0, 0].astype(BF16)
        wu_bf[...] = wu_ref[0, 0].astype(BF16)
        wd_bf[...] = wd_ref[0, 0].astype(BF16)

    @pl.when(i < n_used)
    def _():
        slot = i % 2
        nxt = 1 - slot
        _wait_rows(xbuf.at[slot], sem.at[slot])
        base = jnp.minimum(i + 1, n_used - 1) * tm
        for r in range(tm):
            _row_copy(x_hbm, src_ref[base + r], xbuf.at[nxt], r, sem.at[nxt]).start()
        xb = xbuf[slot].astype(BF16)
        hg = jnp.dot(xb, wg_bf[...], preferred_element_type=F32)
        hu = jnp.dot(xb, wu_bf[...], preferred_element_type=F32)
        h = (hg / (1.0 + jnp.exp(-hg))) * hu
        y_ref[...] = jnp.dot(h.astype(BF16), wd_bf[...], preferred_element_type=F32)

    @pl.when(i == n_used - 1)
    def _():
        _wait_rows(xbuf.at[(i + 1) % 2], sem.at[(i + 1) % 2])

    @pl.when(i >= n_used)
    def _():
        y_ref[...] = jnp.zeros(y_ref.shape, y_ref.dtype)


def moe_grouped_ffn(x_rows, src, tile_expert, n_used, w_gate, w_up, w_down, layer):
    d = D_MODEL
    p = src.shape[0]
    tm = MOE_TM
    grid_spec = pltpu.PrefetchScalarGridSpec(
        num_scalar_prefetch=3,
        grid=(p // tm,),
        in_specs=[pl.BlockSpec(memory_space=pl.ANY),
                  pl.BlockSpec((1, 1, d, D_EXPERT), lambda i, te, nu, sr: (layer, te[i], 0, 0)),
                  pl.BlockSpec((1, 1, d, D_EXPERT), lambda i, te, nu, sr: (layer, te[i], 0, 0)),
                  pl.BlockSpec((1, 1, D_EXPERT, d), lambda i, te, nu, sr: (layer, te[i], 0, 0))],
        out_specs=pl.BlockSpec((tm, d), lambda i, te, nu, sr: (i, 0)),
        scratch_shapes=[pltpu.VMEM((2, tm, d), F32),
                        pltpu.VMEM((d, D_EXPERT), BF16), pltpu.VMEM((d, D_EXPERT), BF16),
                        pltpu.VMEM((D_EXPERT, d), BF16), pltpu.SemaphoreType.DMA((2,))],
    )
    return pl.pallas_call(
        _moe_group_kernel, grid_spec=grid_spec,
        out_shape=jax.ShapeDtypeStruct((p, d), F32),
        compiler_params=_cp(("arbitrary",)),
        name="moe_grouped_ffn",
    )(tile_expert, n_used, src, x_rows, w_gate, w_up, w_down)


def _moe_combine_kernel(p0_ref, p1_ref, y_hbm, w0_ref, w1_ref, x_ref, gm_ref, o_ref, buf0, buf1, sem):
    i = pl.program_id(0)
    tm = buf0.shape[1]

    def gather(tile, slot):
        _row_gather(p0_ref, tile * tm, y_hbm, buf0.at[slot], sem.at[0, slot], tm)
        _row_gather(p1_ref, tile * tm, y_hbm, buf1.at[slot], sem.at[1, slot], tm)

    @pl.when(i == 0)
    def _():
        gather(0, 0)

    @pl.when(i + 1 < pl.num_programs(0))
    def _():
        gather(i + 1, (i + 1) % 2)

    slot = i % 2
    _wait_rows(buf0.at[slot], sem.at[0, slot])
    _wait_rows(buf1.at[slot], sem.at[1, slot])
    moe = w0_ref[...] * buf0[slot] + w1_ref[...] * buf1[slot]
    o_ref[...] = x_ref[...] + gm_ref[0] * moe


def moe_combine(y, pos0, pos1, w0, w1, x, mod3, gate_idx, tm=256):
    b, s, d = x.shape
    t = b * s
    tm = min(tm, s)
    per_batch = s // tm
    grid_spec = pltpu.PrefetchScalarGridSpec(
        num_scalar_prefetch=2,
        grid=(t // tm,),
        in_specs=[pl.BlockSpec(memory_space=pl.ANY),
                  pl.BlockSpec((tm, 1), lambda i, p0, p1: (i, 0)),
                  pl.BlockSpec((tm, 1), lambda i, p0, p1: (i, 0)),
                  pl.BlockSpec((tm, d), lambda i, p0, p1: (i, 0)),
                  pl.BlockSpec((1, 1, d), lambda i, p0, p1: (i // per_batch, 0, gate_idx))],
        out_specs=pl.BlockSpec((tm, d), lambda i, p0, p1: (i, 0)),
        scratch_shapes=[pltpu.VMEM((2, tm, d), F32), pltpu.VMEM((2, tm, d), F32),
                        pltpu.SemaphoreType.DMA((2, 2))],
    )
    out = pl.pallas_call(
        _moe_combine_kernel, grid_spec=grid_spec,
        out_shape=jax.ShapeDtypeStruct((t, d), F32),
        compiler_params=_cp(("arbitrary",)),
        name="moe_combine",
    )(pos0, pos1, y, w0.reshape(t, 1), w1.reshape(t, 1), x.reshape(t, d), mod3)
    return out.reshape(b, s, d)


def moe_sparse(a2, e0, e1, w0, w1, w_gate, w_up, w_down, layer, x, mod3, gate_idx):
    b, s, d = x.shape
    t = b * s
    tm = MOE_TM
    n_tiles = (2 * t + N_EXPERTS * (tm - 1)) // tm
    r0, r1, cnt = expert_ranks(e0, e1)
    counts = cnt[:, 0].astype(jnp.int32)
    padded = (counts + tm - 1) // tm * tm
    ends = jnp.cumsum(padded)
    offs = ends - padded
    e0f, e1f = e0.reshape(t), e1.reshape(t)
    pos0 = offs[e0f] + r0.reshape(t)
    pos1 = offs[e1f] + r1.reshape(t)
    tok = jnp.arange(t, dtype=jnp.int32)
    src = jnp.zeros((n_tiles * tm,), jnp.int32).at[jnp.concatenate([pos0, pos1])].set(
        jnp.concatenate([tok, tok]), unique_indices=True)
    tile_start = jnp.arange(n_tiles, dtype=jnp.int32) * tm
    tile_expert = jnp.minimum(jnp.sum((tile_start[:, None] >= ends[None, :]).astype(jnp.int32), axis=1),
                              N_EXPERTS - 1)
    n_used = (ends[-1:] // tm).astype(jnp.int32)
    y = moe_grouped_ffn(a2.reshape(t, d), src, tile_expert, n_used, w_gate, w_up, w_down, layer)
    return moe_combine(y, pos0, pos1, w0, w1, x, mod3, gate_idx)


def _final_norm_kernel(x_ref, g_ref, o_ref):
    x = x_ref[0]
    o_ref[0] = x * lax.rsqrt(jnp.mean(x * x, axis=-1, keepdims=True) + EPS) * g_ref[...]


def final_norm(x, g, tm=512):
    b, s, d = x.shape
    tm = min(tm, s)
    return pl.pallas_call(
        _final_norm_kernel, grid=(b, s // tm),
        in_specs=[pl.BlockSpec((1, tm, d), lambda i, j: (i, j, 0)),
                  pl.BlockSpec((1, d), lambda i, j: (0, 0))],
        out_specs=pl.BlockSpec((1, tm, d), lambda i, j: (i, j, 0)),
        out_shape=jax.ShapeDtypeStruct((b, s, d), F32),
        compiler_params=_cp(("arbitrary", "arbitrary")),
        name="final_norm",
    )(x, g.reshape(1, d))


def _rot_cols(w):
    half = w.shape[-1] // 2
    return jnp.concatenate([-w[..., half:], w[..., :half]], axis=-1)


_O_KR = Q_LORA + KV_LORA
_O_DIL = _O_KR + ROPE_DIM
_O_SB = _O_DIL + 3 * DIL_WIDTH
_O_GATES = _O_SB + 3 * SB_WIDTH
IN_COLS = _O_GATES + 3 * D_MODEL
DIL_Q_SCALE = DIL_HEAD_DIM ** -0.5 * LOG2E
SB_Q_SCALE = -(SB_HEAD_DIM ** -0.5) * LOG2E

_PACK_PIECES = [(_O_GATES, 3 * D_MODEL, None), (_O_SB, SB_WIDTH, SB_Q_SCALE), (_O_SB + SB_WIDTH, 2 * SB_WIDTH, None)]
for _g in range(DIL_GROUPS):
    _PACK_PIECES += [(_O_DIL + _g * DIL_GW, DIL_GW, DIL_Q_SCALE),
                     (_O_DIL + DIL_WIDTH + _g * DIL_GW, DIL_GW, None),
                     (_O_DIL + 2 * DIL_WIDTH + _g * DIL_GW, DIL_GW, None)]
_PACK_PIECES.append((0, Q_LORA + KV_LORA, None))
PACK_WIDTH = sum(width for _, width, _ in _PACK_PIECES)
PACKED_COLS = dict(gates=(0, 3 * D_MODEL), sb=(3 * D_MODEL, 3 * SB_WIDTH))
for _g in range(DIL_GROUPS):
    PACKED_COLS[f"dil{_g}"] = (3 * D_MODEL + 3 * SB_WIDTH + _g * 3 * DIL_GW, 3 * DIL_GW)
PACKED_COLS["lat"] = (3 * D_MODEL + 3 * SB_WIDTH + 3 * DIL_WIDTH, Q_LORA + KV_LORA)


PACK_BLOCK = 512


def _pack_kernel(src_ref, scale_ref, wt_ref, o_ref):
    del src_ref
    o_ref[...] = (wt_ref[0].T * scale_ref[pl.program_id(0)]).astype(o_ref.dtype)


def pack_projection_weights(w_in_t, layer):
    _, n, d = w_in_t.shape
    assert n == IN_COLS
    src_rows, scales = [], []
    for src, width, scale in _PACK_PIECES:
        assert width % PACK_BLOCK == 0
        for k in range(width // PACK_BLOCK):
            row = src + k * PACK_BLOCK
            assert row % ROPE_DIM == 0
            src_rows.append(row // ROPE_DIM)
            scales.append(1.0 if scale is None else scale)
    grid_spec = pltpu.PrefetchScalarGridSpec(
        num_scalar_prefetch=1,
        grid=(len(src_rows),),
        in_specs=[pl.BlockSpec(memory_space=pltpu.SMEM),
                  pl.BlockSpec((pl.Element(1), pl.Element(PACK_BLOCK), pl.Element(d)),
                               lambda j, src: (layer, src[j] * ROPE_DIM, 0))],
        out_specs=pl.BlockSpec((d, PACK_BLOCK), lambda j, src: (0, j)),
    )
    return pl.pallas_call(
        _pack_kernel, grid_spec=grid_spec,
        out_shape=jax.ShapeDtypeStruct((d, PACK_WIDTH), BF16),
        compiler_params=_cp(("arbitrary",)),
        name="pack_projection_weights",
    )(jnp.asarray(src_rows, jnp.int32), jnp.asarray(scales, F32), w_in_t)


def _pack_rope_kernel(wt_ref, o_ref):
    t = wt_ref[0]
    half = ROPE_DIM // 2
    both = jnp.concatenate([t, -t[half:], t[:half]], axis=0)
    o_ref[...] = both.T.astype(o_ref.dtype)


def pack_rope_weights(w_in_t, layer):
    d = w_in_t.shape[2]
    return pl.pallas_call(
        _pack_rope_kernel, grid=(1,),
        in_specs=[pl.BlockSpec((1, ROPE_DIM, d), lambda i: (layer, _O_KR // ROPE_DIM, 0))],
        out_specs=pl.BlockSpec((d, 2 * ROPE_DIM), lambda i: (0, 0)),
        out_shape=jax.ShapeDtypeStruct((d, 2 * ROPE_DIM), BF16),
        compiler_params=_cp(("arbitrary",)),
        name="pack_rope_weights",
    )(w_in_t)


def _prep_layer(w_in_t, layer, w_uq, w_ukv, w_o_mla, w_o_dil, w_o_sb, w_out):
    w_pack = pack_projection_weights(w_in_t, layer)
    w_kr = pack_rope_weights(w_in_t, layer)
    q_scale = MLA_QK ** -0.5 * LOG2E
    wq = (w_uq * q_scale).reshape(Q_LORA, MLA_HEADS, MLA_QK).transpose(1, 0, 2)
    wq_h = jnp.concatenate([wq, _rot_cols(wq[..., NOPE_DIM:])], axis=-1).astype(BF16)
    wkv_h = w_ukv.reshape(KV_LORA, MLA_HEADS, NOPE_DIM + MLA_V_DIM).transpose(1, 0, 2).astype(BF16)
    return dict(w_pack=w_pack, w_kr=w_kr, wq_h=wq_h, wkv_h=wkv_h,
                w_o_mla=w_o_mla.astype(BF16), w_o_dil=w_o_dil.astype(BF16), w_o_sb=w_o_sb.astype(BF16),
                w_out=w_out.astype(BF16))


def kernel(x, c, positions, w_ada, b_ada, g_mix, g_moe, w_in, g_q, w_uq, g_kv, w_ukv, w_o_mla, w_o_dil,
           w_o_sb, w_out, w_router, b_router, w_gate, w_up, w_down, g_final):
    b, s, d = x.shape
    depth = w_ada.shape[0]
    pos_f = positions.astype(F32)
    cos, sin = rope_tables(pos_f.reshape(b, s, 1))
    mod = ada_mod(c, w_ada, b_ada)
    w_router_t = w_router.T
    w_in_t = jnp.swapaxes(w_in, 1, 2)
    for l in range(depth):
        p = _prep_layer(w_in_t, l, w_uq[l], w_ukv[l], w_o_mla[l], w_o_dil[l], w_o_sb[l], w_out[l])
        mod3 = mod[l].reshape(b, 1, 6 * d)

        a = norm_mod(x, g_mix[l], mod3, sc_idx=1, sh_idx=0)
        wp = p["w_pack"]
        lat = project(a, wp, *PACKED_COLS["lat"])
        kr = project(a, p["w_kr"], 0, 2 * ROPE_DIM)
        dil = [project_by_residue(a, wp, *PACKED_COLS[f"dil{g}"], DIL_RATES[g]) for g in range(DIL_GROUPS)]
        sbp = project(a, wp, *PACKED_COLS["sb"], tn=1024)
        gl = project(a, wp, *PACKED_COLS["gates"], tn=1024)

        q, k, v = mla_project(lat, kr, g_q[l], p["wq_h"], g_kv[l], p["wkv_h"], cos, sin)
        y_mla = mla_attention(q, k, v)
        dil_out = [dil_group_attention(dil[g], pos_f, g) for g in range(DIL_GROUPS)]
        y_sb = sb_attention(sbp)
        merged = merge_branches(y_mla, [o for o, _ in dil_out], [e for _, e in dil_out], y_sb, gl,
                                p["w_o_mla"], p["w_o_dil"], p["w_o_sb"])
        x = resid_project(merged, p["w_out"], x, mod3, gate_idx=2)

        a2, logits_t = norm_mod(x, g_moe[l], mod3, sc_idx=4, sh_idx=3, w_router_t=w_router_t)
        e0, e1, w0, w1 = router_top2(logits_t, b_router)
        x = moe_sparse(a2, e0, e1, w0, w1, w_gate, w_up, w_down, l, x, mod3, gate_idx=5)
    return final_norm(x, g_final)
```

```python
import functools
import math

import jax
import jax.numpy as jnp
from jax import lax
from jax.experimental import pallas as pl
from jax.experimental.pallas import tpu as pltpu

D_MODEL = 2048
EPS = 1e-6
MLA_HEADS = 8
Q_LORA = 512
KV_LORA = 512
NOPE_DIM = 128
ROPE_DIM = 64
MLA_V_DIM = 128
ROPE_THETA = 10000.0
MLA_QK = NOPE_DIM + ROPE_DIM
DIL_WINDOWS = (128, 512, 2048)
DIL_RATES = (1, 4, 16)
DIL_GROUPS = 3
DIL_HEADS = 4
DIL_HEAD_DIM = 128
DIL_SPAN = DIL_WINDOWS[0] // DIL_RATES[0]
DIL_GW = DIL_HEADS * DIL_HEAD_DIM
DIL_WIDTH = DIL_GROUPS * DIL_GW
ALIBI_MAX_BIAS = 8.0
SB_HEADS = 8
SB_HEAD_DIM = 128
SB_WIDTH = SB_HEADS * SB_HEAD_DIM
N_EXPERTS = 16
N_GROUPS = 4
EXPERTS_PER_GROUP = N_EXPERTS // N_GROUPS
D_EXPERT = 512

LOG2E = math.log2(math.e)
LN2 = math.log(2.0)
NEG_BIG = -1e30

V7X_LANES = 128
V7X_VMEM_BYTES = 64 * 1024 * 1024
VMEM_LIMIT = 56 * 1024 * 1024

F32 = jnp.float32
BF16 = jnp.bfloat16


def _cp(sem, vmem=VMEM_LIMIT):
    return pltpu.CompilerParams(dimension_semantics=sem, vmem_limit_bytes=vmem)


def _lane_tile(x, n):
    return x if n == 1 else jnp.concatenate([x] * n, axis=1)


def _nt_dot(a, b):
    return lax.dot_general(a, b, (((1,), (1,)), ((), ())), preferred_element_type=F32)


def _ada_kernel(c_ref, w_ref, b_ref, o_ref):
    c = c_ref[...]
    ca = c * (1.0 / (1.0 + jnp.exp(-c)))
    o_ref[0] = jnp.dot(ca, w_ref[0], preferred_element_type=F32,
                       precision=lax.Precision.HIGHEST) + b_ref[0]


def ada_mod(c, w_ada, b_ada, tn=1024):
    depth, d, n = w_ada.shape
    b = c.shape[0]
    return pl.pallas_call(
        _ada_kernel,
        grid=(depth, n // tn),
        in_specs=[pl.BlockSpec((b, d), lambda l, j: (0, 0)),
                  pl.BlockSpec((1, d, tn), lambda l, j: (l, 0, j)),
                  pl.BlockSpec((1, 1, tn), lambda l, j: (l, 0, j))],
        out_specs=pl.BlockSpec((1, b, tn), lambda l, j: (l, 0, j)),
        out_shape=jax.ShapeDtypeStruct((depth, b, n), F32),
        compiler_params=_cp(("arbitrary", "arbitrary")),
        name="ada_mod",
    )(c, w_ada, b_ada.reshape(depth, 1, n))


def _rope_table_kernel(pos_ref, inv_ref, cos_ref, sin_ref):
    ang = pos_ref[0] * inv_ref[...]
    cos_ref[0] = jnp.cos(ang)
    sin_ref[0] = jnp.sin(ang)


def rope_tables(pos_col, tm=512):
    b, s, _ = pos_col.shape
    tm = min(tm, s)
    half = ROPE_DIM // 2
    inv = ROPE_THETA ** (-jnp.arange(half, dtype=F32) / half)
    inv2 = jnp.concatenate([inv, inv]).reshape(1, ROPE_DIM)
    shp = jax.ShapeDtypeStruct((b, s, ROPE_DIM), F32)
    return pl.pallas_call(
        _rope_table_kernel,
        grid=(b, s // tm),
        in_specs=[pl.BlockSpec((1, tm, 1), lambda i, j: (i, j, 0)),
                  pl.BlockSpec((1, ROPE_DIM), lambda i, j: (0, 0))],
        out_specs=[pl.BlockSpec((1, tm, ROPE_DIM), lambda i, j: (i, j, 0))] * 2,
        out_shape=[shp, shp],
        compiler_params=_cp(("arbitrary", "arbitrary")),
        name="rope_tables",
    )(pos_col, inv2)


def _rmsnorm(x, g):
    return x * lax.rsqrt(jnp.mean(x * x, axis=-1, keepdims=True) + EPS) * g


def _norm_mod(x, g, scale, shift):
    return _rmsnorm(x, g) * (1.0 + scale) + shift


def _norm_kernel(x_ref, g_ref, sc_ref, sh_ref, o_ref):
    o_ref[0] = _norm_mod(x_ref[0], g_ref[...], sc_ref[0], sh_ref[0]).astype(o_ref.dtype)


def norm_mod(x, g, mod3, sc_idx, sh_idx, tm=512):
    b, s, d = x.shape
    tm = min(tm, s)
    return pl.pallas_call(
        _norm_kernel, grid=(b, s // tm),
        in_specs=[pl.BlockSpec((1, tm, d), lambda i, j: (i, j, 0)),
                  pl.BlockSpec((1, d), lambda i, j: (0, 0)),
                  pl.BlockSpec((1, 1, d), lambda i, j: (i, 0, sc_idx)),
                  pl.BlockSpec((1, 1, d), lambda i, j: (i, 0, sh_idx))],
        out_specs=pl.BlockSpec((1, tm, d), lambda i, j: (i, j, 0)),
        out_shape=jax.ShapeDtypeStruct((b, s, d), BF16),
        compiler_params=_cp(("arbitrary", "arbitrary")), name="norm_mod",
    )(x, g.reshape(1, d), mod3, mod3)


def _matmul_kernel(a_ref, w_ref, o_ref):
    o_ref[0] = jnp.dot(a_ref[0], w_ref[...], preferred_element_type=F32).astype(o_ref.dtype)


def project(a, w, col0, n, tm=1024, tn=512, out_dtype=BF16):
    b, s, k = a.shape
    tm = min(tm, s)
    tn = min(tn, n)
    assert n % tn == 0 and s % tm == 0 and col0 % tn == 0
    cb0 = col0 // tn
    return pl.pallas_call(
        _matmul_kernel,
        grid=(b, s // tm, n // tn),
        in_specs=[pl.BlockSpec((1, tm, k), lambda i, j, c: (i, j, 0)),
                  pl.BlockSpec((k, tn), lambda i, j, c: (0, cb0 + c))],
        out_specs=pl.BlockSpec((1, tm, tn), lambda i, j, c: (i, j, c)),
        out_shape=jax.ShapeDtypeStruct((b, s, n), out_dtype),
        compiler_params=_cp(("arbitrary", "arbitrary", "arbitrary")),
        name="project",
    )(a, w)


def _matmul_residue_kernel(a_ref, w_ref, o_ref, res_scr, *, rate):
    res = jnp.dot(a_ref[0], w_ref[...], preferred_element_type=F32)
    slabs, rows, lanes = res_scr.shape
    for c in range(slabs):
        res_scr[c] = res[:, c * lanes:(c + 1) * lanes]
    n = rows // rate
    for rho in range(rate):
        for c in range(slabs):
            o_ref[0, rho, :, c * lanes:(c + 1) * lanes] = (
                res_scr[c, pl.ds(rho, n, stride=rate), :].astype(o_ref.dtype))


def project_by_residue(a, w, col0, n, rate, tm=1024, tn=512):
    b, s, k = a.shape
    if rate == 1:
        return project(a, w, col0, n, tm=tm, tn=tn).reshape(b, 1, s, n)
    tm = min(tm, s)
    assert n % tn == 0 and s % tm == 0 and tm % (rate * 16) == 0 and col0 % tn == 0
    cb0 = col0 // tn
    return pl.pallas_call(
        functools.partial(_matmul_residue_kernel, rate=rate),
        grid=(b, s // tm, n // tn),
        in_specs=[pl.BlockSpec((1, tm, k), lambda i, j, c: (i, j, 0)),
                  pl.BlockSpec((k, tn), lambda i, j, c: (0, cb0 + c))],
        out_specs=pl.BlockSpec((1, rate, tm // rate, tn), lambda i, j, c: (i, 0, j, c)),
        out_shape=jax.ShapeDtypeStruct((b, rate, s // rate, n), BF16),
        scratch_shapes=[pltpu.VMEM((tn // V7X_LANES, tm, V7X_LANES), F32)],
        compiler_params=_cp(("arbitrary", "arbitrary", "arbitrary")),
        name=f"project_by_residue_{rate}",
    )(a, w)


def _latent_norm(c, g):
    c = c.astype(F32)
    return (c * lax.rsqrt(jnp.mean(c * c, axis=-1, keepdims=True) + EPS) * g).astype(BF16)


def _mla_proj_kernel(cq_ref, ckv_ref, kr_ref, gq_ref, gkv_ref, wq_ref, wkv_ref, cos_ref, sin_ref,
                     q_ref, k_ref, v_ref):
    cos, sin = cos_ref[0], sin_ref[0]
    rq = _latent_norm(cq_ref[0], gq_ref[...])
    rkv = _latent_norm(ckv_ref[0], gkv_ref[...])
    kr = kr_ref[0].astype(F32)
    k_rope = (kr[:, :ROPE_DIM] * cos + kr[:, ROPE_DIM:] * sin).astype(k_ref.dtype)
    ones = jnp.ones((rq.shape[0], MLA_V_DIM), v_ref.dtype)
    for h in range(MLA_HEADS):
        r = jnp.dot(rq, wq_ref[h], preferred_element_type=F32)
        q_ref[0, h, :, :NOPE_DIM] = r[:, :NOPE_DIM].astype(q_ref.dtype)
        roped = r[:, NOPE_DIM:NOPE_DIM + ROPE_DIM] * cos + r[:, NOPE_DIM + ROPE_DIM:] * sin
        q_ref[0, h, :, NOPE_DIM:] = roped.astype(q_ref.dtype)
        r = jnp.dot(rkv, wkv_ref[h], preferred_element_type=F32)
        k_ref[0, h, :, :NOPE_DIM] = r[:, :NOPE_DIM].astype(k_ref.dtype)
        k_ref[0, h, :, NOPE_DIM:] = k_rope
        v_ref[0, h, :, :MLA_V_DIM] = r[:, NOPE_DIM:].astype(v_ref.dtype)
        v_ref[0, h, :, MLA_V_DIM:] = ones


def mla_project(lat, kr, g_q, wq_h, g_kv, wkv_h, cos, sin, tm=512):
    b, s, _ = lat.shape
    tm = min(tm, s)
    h = MLA_HEADS
    tab = pl.BlockSpec((1, tm, ROPE_DIM), lambda i, j: (i, j, 0))

    def const(shape):
        return pl.BlockSpec(shape, lambda i, j: (0,) * len(shape))

    def head_major(width):
        return pl.BlockSpec((1, h, tm, width), lambda i, j: (i, 0, j, 0))

    return pl.pallas_call(
        _mla_proj_kernel, grid=(b, s // tm),
        in_specs=[pl.BlockSpec((1, tm, Q_LORA), lambda i, j: (i, j, 0)),
                  pl.BlockSpec((1, tm, KV_LORA), lambda i, j: (i, j, 1)),
                  pl.BlockSpec((1, tm, 2 * ROPE_DIM), lambda i, j: (i, j, 0)),
                  const((1, Q_LORA)), const((1, KV_LORA)),
                  const((h, Q_LORA, 256)), const((h, KV_LORA, 256)), tab, tab],
        out_specs=[head_major(MLA_QK), head_major(MLA_QK), head_major(2 * MLA_V_DIM)],
        out_shape=[jax.ShapeDtypeStruct((b, h, s, MLA_QK), BF16),
                   jax.ShapeDtypeStruct((b, h, s, MLA_QK), BF16),
                   jax.ShapeDtypeStruct((b, h, s, 2 * MLA_V_DIM), BF16)],
        compiler_params=_cp(("arbitrary", "arbitrary")), name="mla_project",
    )(lat, lat, kr, g_q.reshape(1, Q_LORA), g_kv.reshape(1, KV_LORA), wq_h, wkv_h, cos, sin)


def _mla_attn_kernel(q_ref, k_ref, v_ref, o_ref, s0_scr, s1_scr, m_scr, acc_scr, *, tq, tk):
    qi = pl.program_id(2)
    m_scr[...] = jnp.full(m_scr.shape, NEG_BIG, F32)
    acc_scr[...] = jnp.zeros(acc_scr.shape, F32)

    def scores(c, dst, row0=0):
        start = pl.multiple_of(c * tk, tk)
        dst[row0:, :] = _nt_dot(q_ref[0, 0, row0:, :], k_ref[0, 0, pl.ds(start, tk), :])

    def absorb(src, c, col_off=None, row0=0):
        start = pl.multiple_of(c * tk, tk)
        s = src[row0:, :]
        n = tq - row0
        if col_off is not None:
            row = row0 + lax.broadcasted_iota(jnp.int32, (n, tk), 0)
            col = col_off + lax.broadcasted_iota(jnp.int32, (n, tk), 1)
            s = jnp.where(col <= row, s, NEG_BIG)
        m_prev = m_scr[row0:, :]
        m_new = jnp.maximum(m_prev, jnp.max(s, axis=-1, keepdims=True))
        p = jnp.exp2(s - _lane_tile(m_new, tk // V7X_LANES))
        alpha = jnp.exp2(m_prev - m_new)
        pv = jnp.dot(p.astype(BF16), v_ref[0, 0, pl.ds(start, tk), :], preferred_element_type=F32)
        acc_scr[row0:, :] = _lane_tile(alpha, 2) * acc_scr[row0:, :] + pv
        m_scr[row0:, :] = m_new

    scores(0, s0_scr)

    def body(i, carry):
        scores(2 * i + 1, s1_scr)
        absorb(s0_scr, 2 * i)
        scores(2 * i + 2, s0_scr)
        absorb(s1_scr, 2 * i + 1)
        return carry

    lax.fori_loop(0, qi, body, 0)
    scores(2 * qi + 1, s1_scr, row0=tk)
    absorb(s0_scr, 2 * qi, col_off=0)
    absorb(s1_scr, 2 * qi + 1, col_off=tk, row0=tk)
    acc = acc_scr[...]
    o_ref[0] = (acc[:, :MLA_V_DIM] / acc[:, MLA_V_DIM:]).astype(o_ref.dtype)


def mla_attention(q, k, v, tq=1024):
    b, h, s, _ = q.shape
    tq = min(tq, s)
    tk = tq // 2
    return pl.pallas_call(
        functools.partial(_mla_attn_kernel, tq=tq, tk=tk),
        grid=(b, h, s // tq),
        in_specs=[pl.BlockSpec((1, 1, tq, MLA_QK), lambda i, j, t: (i, j, t, 0)),
                  pl.BlockSpec((1, 1, s, MLA_QK), lambda i, j, t: (i, j, 0, 0)),
                  pl.BlockSpec((1, 1, s, 2 * MLA_V_DIM), lambda i, j, t: (i, j, 0, 0))],
        out_specs=pl.BlockSpec((1, tq, MLA_V_DIM), lambda i, j, t: (i, t, j)),
        out_shape=jax.ShapeDtypeStruct((b, s, h * MLA_V_DIM), BF16),
        scratch_shapes=[pltpu.VMEM((tq, tk), F32), pltpu.VMEM((tq, tk), F32),
                        pltpu.VMEM((tq, V7X_LANES), F32), pltpu.VMEM((tq, 2 * MLA_V_DIM), F32)],
        compiler_params=_cp(("arbitrary", "arbitrary", "arbitrary")),
        name="mla_attention",
    )(q, k, v)


SB_SKIP_LOG2 = 150.0


def _sb_attn_kernel(q_ref, k_ref, v_ref, tri_ref, o_ref, c_scr, acc_scr, *, tq, hp):
    qi = pl.program_id(2)
    c_scr[...] = jnp.zeros(c_scr.shape, F32)
    acc_scr[...] = jnp.zeros(acc_scr.shape, F32)
    reps = tq // V7X_LANES

    def chunk(j, diagonal):
        start = pl.multiple_of(j * tq, tq)
        heads = range(hp)
        cols = [slice(hh * SB_HEAD_DIM, (hh + 1) * SB_HEAD_DIM) for hh in heads]
        if diagonal:
            before = (lax.broadcasted_iota(jnp.int32, (tq, tq), 1)
                      < lax.broadcasted_iota(jnp.int32, (tq, tq), 0))
        zn = [_nt_dot(q_ref[0, :, cols[hh]], k_ref[0, pl.ds(start, tq), cols[hh]]) for hh in heads]
        lk = [jnp.minimum(z, 0.0) - jnp.log2(1.0 + jnp.exp2(-jnp.abs(z))) for z in zn]
        if diagonal:
            lk = [jnp.where(before, x, 0.0) for x in lk]
        hi = [x.astype(BF16) for x in lk]
        lo = [(x - h.astype(F32)).astype(BF16) for x, h in zip(lk, hi)]
        suffix = [jnp.dot(jnp.concatenate([h, l], axis=1), tri_ref[...], preferred_element_type=F32)
                  for h, l in zip(hi, lo)]
        c = [c_scr[hh] for hh in heads]
        a = [jnp.exp2(sf - z + _lane_tile(cc, reps)) for sf, z, cc in zip(suffix, zn, c)]
        if diagonal:
            a = [jnp.where(before, x, 0.0) for x in a]
        for hh in heads:
            acc_scr[:, cols[hh]] += jnp.dot(a[hh].astype(BF16), v_ref[0, pl.ds(start, tq), cols[hh]],
                                            preferred_element_type=F32)
            c_scr[hh] = c[hh] + jnp.sum(lk[hh], axis=-1, keepdims=True)

    chunk(qi, True)

    def cond(state):
        j, go = state
        return jnp.logical_and(j >= 0, go > 0)

    def body(state):
        j, _ = state
        chunk(j, False)
        go = (jnp.max(c_scr[...]) > -SB_SKIP_LOG2).astype(jnp.int32)
        return j - 1, go

    lax.while_loop(cond, body, (qi - 1, jnp.int32(1)))
    o_ref[0] = acc_scr[...].astype(o_ref.dtype)


def sb_attention(qkv, tq=256, hp=4):
    b, s, _ = qkv.shape
    tq = min(tq, s)
    hw = hp * SB_HEAD_DIM
    nhb = SB_HEADS // hp
    tri = (lax.broadcasted_iota(jnp.int32, (tq, tq), 0)
           >= lax.broadcasted_iota(jnp.int32, (tq, tq), 1)).astype(BF16)
    tri2 = jnp.concatenate([tri, tri], axis=0)
    return pl.pallas_call(
        functools.partial(_sb_attn_kernel, tq=tq, hp=hp),
        grid=(b, nhb, s // tq),
        in_specs=[pl.BlockSpec((1, tq, hw), lambda i, j, t: (i, t, j)),
                  pl.BlockSpec((1, s, hw), lambda i, j, t: (i, 0, nhb + j)),
                  pl.BlockSpec((1, s, hw), lambda i, j, t: (i, 0, 2 * nhb + j)),
                  pl.BlockSpec((2 * tq, tq), lambda i, j, t: (0, 0))],
        out_specs=pl.BlockSpec((1, tq, hw), lambda i, j, t: (i, t, j)),
        out_shape=jax.ShapeDtypeStruct((b, s, SB_WIDTH), BF16),
        scratch_shapes=[pltpu.VMEM((hp, tq, V7X_LANES), F32), pltpu.VMEM((tq, hw), F32)],
        compiler_params=_cp(("arbitrary", "arbitrary", "arbitrary")),
        name="sb_attention",
    )(qkv, qkv, qkv, tri2)


def _dil_kernel(q_ref, kc_ref, vc_ref, kp_ref, vp_ref, pq_ref, pkc_ref, pkp_ref, sl_ref,
                o_ref, lse_ref, *, tq):
    ti = pl.program_id(2)
    w = DIL_SPAN
    a_idx = lax.broadcasted_iota(jnp.int32, (w, w), 0)
    c_idx = lax.broadcasted_iota(jnp.int32, (w, w), 1)
    own_ok = c_idx <= a_idx
    prev_tri = c_idx >= a_idx
    for sb in range(tq // w):
        rows = slice(sb * w, (sb + 1) * w)
        pq = pq_ref[0, 0, rows, :]
        pk_own = pkc_ref[0, 0, :, rows]
        if sb == 0:
            pk_prev = pkp_ref[0, 0]
            prev_ok = jnp.logical_and(prev_tri, ti > 0)
        else:
            pk_prev = pkc_ref[0, 0, :, (sb - 1) * w: sb * w]
            prev_ok = prev_tri
        dist_own = jnp.abs(pq - pk_own)
        dist_prev = jnp.abs(pq - pk_prev)
        heads = range(DIL_HEADS)
        cols = [slice(h * DIL_HEAD_DIM, (h + 1) * DIL_HEAD_DIM) for h in heads]
        prev_rows = slice((sb - 1) * w, sb * w)

        def k_prev(h):
            return kp_ref[0, :, cols[h]] if sb == 0 else kc_ref[0, prev_rows, cols[h]]

        def v_prev(h):
            return vp_ref[0, :, cols[h]] if sb == 0 else vc_ref[0, prev_rows, cols[h]]

        z_own = [_nt_dot(q_ref[0, rows, cols[h]], kc_ref[0, rows, cols[h]]) for h in heads]
        z_prev = [_nt_dot(q_ref[0, rows, cols[h]], k_prev(h)) for h in heads]
        s_own = [jnp.where(own_ok, z_own[h] - sl_ref[h] * dist_own, NEG_BIG) for h in heads]
        s_prev = [jnp.where(prev_ok, z_prev[h] - sl_ref[h] * dist_prev, NEG_BIG) for h in heads]
        m = [jnp.maximum(jnp.max(s_own[h], axis=-1, keepdims=True), jnp.max(s_prev[h], axis=-1, keepdims=True))
             for h in heads]
        p_own = [jnp.exp2(s_own[h] - m[h]) for h in heads]
        p_prev = [jnp.exp2(s_prev[h] - m[h]) for h in heads]
        den = [jnp.sum(p_own[h], axis=-1, keepdims=True) + jnp.sum(p_prev[h], axis=-1, keepdims=True)
               for h in heads]
        for h in heads:
            o = (jnp.dot(p_own[h].astype(BF16), vc_ref[0, rows, cols[h]], preferred_element_type=F32)
                 + jnp.dot(p_prev[h].astype(BF16), v_prev(h), preferred_element_type=F32)) / den[h]
            o_ref[0, rows, cols[h]] = o
            lse = (m[h] + jnp.log2(den[h])) * LN2
            lse_ref[0, rows, cols[h]] = jnp.broadcast_to(lse, (w, DIL_HEAD_DIM))


def dil_group_attention(dil_g, pos_f, g, tq=512):
    b, r, sr, c = dil_g.shape
    assert r == DIL_RATES[g]
    tq = min(tq, sr)
    w = DIL_SPAN
    assert sr % tq == 0 and tq % w == 0
    nsub = tq // w
    view = dil_g.reshape(b * r, sr, c)
    pos_r = pos_f.reshape(b, sr, r).transpose(0, 2, 1)
    pos_q = pos_r[..., None]
    pos_k = pos_r[:, :, None, :]
    n_h = DIL_GROUPS * DIL_HEADS
    slopes = 2.0 ** (-ALIBI_MAX_BIAS * jnp.arange(1, n_h + 1, dtype=F32) / n_h)
    slopes2 = (slopes * LOG2E)[g * DIL_HEADS:(g + 1) * DIL_HEADS]

    def cur(part):
        return pl.BlockSpec((1, tq, DIL_GW), lambda i, p, t: (i * r + p, t, part))

    def prev(part):
        return pl.BlockSpec((1, w, DIL_GW), lambda i, p, t: (i * r + p, jnp.maximum(t * nsub - 1, 0), part))

    out_spec = pl.BlockSpec((1, tq, DIL_GW), lambda i, p, t: (i * r + p, t, 0))
    out_shape = jax.ShapeDtypeStruct((b * r, sr, DIL_GW), F32)
    o, lse = pl.pallas_call(
        functools.partial(_dil_kernel, tq=tq),
        grid=(b, r, sr // tq),
        in_specs=[cur(0), cur(1), cur(2), prev(1), prev(2),
                  pl.BlockSpec((1, 1, tq, 1), lambda i, p, t: (i, p, t, 0)),
                  pl.BlockSpec((1, 1, 1, tq), lambda i, p, t: (i, p, 0, t)),
                  pl.BlockSpec((1, 1, 1, w), lambda i, p, t: (i, p, 0, jnp.maximum(t * nsub - 1, 0))),
                  pl.BlockSpec(memory_space=pltpu.SMEM)],
        out_specs=[out_spec, out_spec],
        out_shape=[out_shape, out_shape],
        compiler_params=_cp(("arbitrary", "arbitrary", "arbitrary")),
        name=f"dil_attention_g{g}",
    )(view, view, view, view, view, pos_q, pos_k, pos_k, slopes2)
    return o.reshape(b, r, sr, DIL_GW), lse.reshape(b, r, sr, DIL_GW)


def _token_order(ref, scr):
    _, r, n, width = ref.shape
    if r == 1:
        return ref[0, 0]
    slabs = width // V7X_LANES
    for rho in range(r):
        for c in range(slabs):
            scr[c, pl.ds(rho, n, stride=r), :] = ref[0, rho, :, c * V7X_LANES:(c + 1) * V7X_LANES]
    return jnp.concatenate([scr[c] for c in range(slabs)], axis=1)


def _merge_kernel(ym_ref, o0_ref, o1_ref, o2_ref, l0_ref, l1_ref, l2_ref, ys_ref,
                  g0_ref, g1_ref, g2_ref, wm_ref, wd_ref, ws_ref, out_ref, *scr):
    o0, o1, o2 = _token_order(o0_ref, scr[0]), _token_order(o1_ref, scr[1]), _token_order(o2_ref, scr[2])
    l0, l1, l2 = _token_order(l0_ref, scr[3]), _token_order(l1_ref, scr[4]), _token_order(l2_ref, scr[5])
    mx = jnp.maximum(jnp.maximum(l0, l1), l2)
    e0, e1, e2 = jnp.exp(l0 - mx), jnp.exp(l1 - mx), jnp.exp(l2 - mx)
    y_dil = (e0 * o0 + e1 * o1 + e2 * o2) / (e0 + e1 + e2)

    def sig(ref):
        return 1.0 / (1.0 + jnp.exp(-ref[0].astype(F32)))

    merged = (sig(g0_ref) * jnp.dot(ym_ref[0], wm_ref[...], preferred_element_type=F32)
              + sig(g1_ref) * jnp.dot(y_dil.astype(BF16), wd_ref[...], preferred_element_type=F32)
              + sig(g2_ref) * jnp.dot(ys_ref[0], ws_ref[...], preferred_element_type=F32))
    out_ref[0] = merged.astype(out_ref.dtype)


def merge_branches(y_mla, dil_o, dil_lse, y_sb, gates, w_o_mla, w_o_dil, w_o_sb, tm=256):
    b, s, _ = y_mla.shape
    d = D_MODEL
    tm = min(tm, s)

    def row(width, cb=0):
        return pl.BlockSpec((1, tm, width), lambda i, j: (i, j, cb))

    def by_residue(r):
        return pl.BlockSpec((1, r, tm // r, DIL_GW), lambda i, j: (i, 0, j, 0))

    def full(w):
        return pl.BlockSpec(w.shape, lambda i, j: (0, 0))

    dil_specs = [by_residue(r) for r in DIL_RATES]
    return pl.pallas_call(
        _merge_kernel, grid=(b, s // tm),
        in_specs=[row(y_mla.shape[2])] + dil_specs + dil_specs + [row(SB_WIDTH),
                  row(d, 0), row(d, 1), row(d, 2), full(w_o_mla), full(w_o_dil), full(w_o_sb)],
        out_specs=row(d),
        out_shape=jax.ShapeDtypeStruct((b, s, d), BF16),
        scratch_shapes=[pltpu.VMEM((DIL_GW // V7X_LANES, tm, V7X_LANES), F32)] * (2 * DIL_GROUPS),
        compiler_params=_cp(("arbitrary", "arbitrary")),
        name="merge_branches",
    )(y_mla, *dil_o, *dil_lse, y_sb, gates, gates, gates, w_o_mla, w_o_dil, w_o_sb)


def _resid_norm_router_kernel(a_ref, w_ref, x_ref, ga_ref, g_ref, sc_ref, sh_ref, wr_ref,
                              xo_ref, a2_ref, lg_ref):
    x_new = x_ref[0] + ga_ref[0] * jnp.dot(a_ref[0], w_ref[...], preferred_element_type=F32)
    xo_ref[0] = x_new
    a2 = _norm_mod(x_new, g_ref[...], sc_ref[0], sh_ref[0])
    a2_ref[0] = a2
    lg_ref[0] = lax.dot_general(wr_ref[...], a2, (((1,), (1,)), ((), ())),
                                preferred_element_type=F32, precision=lax.Precision.HIGHEST)


def resid_project_norm_router(a, w, x, mod3, gate_idx, g, sc_idx, sh_idx, w_router_t, tm=512):
    b, s, d = x.shape
    k = a.shape[2]
    e = w_router_t.shape[0]
    tm = min(tm, s)
    row = pl.BlockSpec((1, tm, d), lambda i, j: (i, j, 0))

    def chunk(idx):
        return pl.BlockSpec((1, 1, d), lambda i, j: (i, 0, idx))

    big = jax.ShapeDtypeStruct((b, s, d), F32)
    return pl.pallas_call(
        _resid_norm_router_kernel, grid=(b, s // tm),
        in_specs=[pl.BlockSpec((1, tm, k), lambda i, j: (i, j, 0)),
                  pl.BlockSpec((k, d), lambda i, j: (0, 0)),
                  row, chunk(gate_idx),
                  pl.BlockSpec((1, d), lambda i, j: (0, 0)), chunk(sc_idx), chunk(sh_idx),
                  pl.BlockSpec((e, d), lambda i, j: (0, 0))],
        out_specs=[row, row, pl.BlockSpec((1, e, tm), lambda i, j: (i, 0, j))],
        out_shape=[big, big, jax.ShapeDtypeStruct((b, e, s), F32)],
        compiler_params=_cp(("arbitrary", "arbitrary")),
        name="resid_project_norm_router",
    )(a, w, x, mod3, g.reshape(1, d), mod3, mod3, w_router_t)


def _router_kernel(lg_ref, b_ref, e0_ref, e1_ref, w0_ref, w1_ref):
    lg = lg_ref[0]
    tm = lg.shape[1]
    sc = [1.0 / (1.0 + jnp.exp(-lg[e:e + 1, :])) for e in range(N_EXPERTS)]
    bi = [sc[e] + b_ref[e] for e in range(N_EXPERTS)]
    n = EXPERTS_PER_GROUP
    gscore = []
    for g in range(N_GROUPS):
        v = bi[g * n:(g + 1) * n]
        pair_max = None
        for a in range(n):
            for c in range(a + 1, n):
                pm = v[a] + v[c]
                pair_max = pm if pair_max is None else jnp.maximum(pair_max, pm)
        gscore.append(pair_max)
    best = gscore[0]
    gsel = jnp.zeros((1, tm), jnp.int32)
    for g in range(1, N_GROUPS):
        better = gscore[g] > best
        best = jnp.where(better, gscore[g], best)
        gsel = jnp.where(better, g, gsel)
    gb = [sum(jnp.where(gsel == g, bi[g * n + i], 0.0) for g in range(N_GROUPS)) for i in range(n)]
    gs = [sum(jnp.where(gsel == g, sc[g * n + i], 0.0) for g in range(N_GROUPS)) for i in range(n)]
    sel = []
    for i in range(n):
        beaten = jnp.zeros((1, tm), jnp.int32)
        for j in range(n):
            if j == i:
                continue
            wins = (gb[j] > gb[i]) if j > i else (gb[j] >= gb[i])
            beaten = beaten + wins.astype(jnp.int32)
        sel.append(beaten < 2)
    den = sum(jnp.where(sel[i], gs[i], 0.0) for i in range(n))
    i0 = jnp.where(sel[0], 0, jnp.where(sel[1], 1, 2))
    i1 = jnp.where(sel[3], 3, jnp.where(sel[2], 2, 1))
    e0_ref[0] = gsel * n + i0
    e1_ref[0] = gsel * n + i1
    w0_ref[0] = sum(jnp.where(i0 == i, gs[i], 0.0) for i in range(n)) / den
    w1_ref[0] = sum(jnp.where(i1 == i, gs[i], 0.0) for i in range(n)) / den


def router_top2(logits_t, b_router, tm=1024):
    b, e, s = logits_t.shape
    tm = min(tm, s)
    row = pl.BlockSpec((1, 1, tm), lambda i, j: (i, 0, j))
    ishape = jax.ShapeDtypeStruct((b, 1, s), jnp.int32)
    fshape = jax.ShapeDtypeStruct((b, 1, s), F32)
    return pl.pallas_call(
        _router_kernel, grid=(b, s // tm),
        in_specs=[pl.BlockSpec((1, e, tm), lambda i, j: (i, 0, j)),
                  pl.BlockSpec(memory_space=pltpu.SMEM)],
        out_specs=[row, row, row, row],
        out_shape=[ishape, ishape, fshape, fshape],
        compiler_params=_cp(("arbitrary", "arbitrary")),
        name="router_top2",
    )(logits_t, b_router)


def _rank_kernel(e0_ref, e1_ref, tri_ref, r0_ref, r1_ref, cnt_ref, carry_scr):
    @pl.when(jnp.logical_and(pl.program_id(0) == 0, pl.program_id(1) == 0))
    def _():
        carry_scr[...] = jnp.zeros(carry_scr.shape, F32)

    e0, e1 = e0_ref[0], e1_ref[0]
    tb = e0.shape[1]
    eid = lax.broadcasted_iota(jnp.int32, (N_EXPERTS, tb), 0)
    hit0 = eid == e0
    hit1 = eid == e1
    used = jnp.where(jnp.logical_or(hit0, hit1), 1.0, 0.0)
    before = jnp.dot(used.astype(BF16), tri_ref[...], preferred_element_type=F32)
    rank = before + _lane_tile(carry_scr[...], tb // V7X_LANES)
    r0_ref[0] = jnp.sum(jnp.where(hit0, rank, 0.0), axis=0, keepdims=True).astype(jnp.int32)
    r1_ref[0] = jnp.sum(jnp.where(hit1, rank, 0.0), axis=0, keepdims=True).astype(jnp.int32)
    carry_scr[...] = carry_scr[...] + jnp.sum(used, axis=1, keepdims=True)
    cnt_ref[...] = carry_scr[...]


def expert_ranks(e0, e1, tb=512):
    b, _, s = e0.shape
    tb = min(tb, s)
    tri = (lax.broadcasted_iota(jnp.int32, (tb, tb), 0)
           < lax.broadcasted_iota(jnp.int32, (tb, tb), 1)).astype(BF16)
    row = pl.BlockSpec((1, 1, tb), lambda i, j: (i, 0, j))
    ishape = jax.ShapeDtypeStruct((b, 1, s), jnp.int32)
    return pl.pallas_call(
        _rank_kernel, grid=(b, s // tb),
        in_specs=[row, row, pl.BlockSpec((tb, tb), lambda i, j: (0, 0))],
        out_specs=[row, row, pl.BlockSpec((N_EXPERTS, V7X_LANES), lambda i, j: (0, 0))],
        out_shape=[ishape, ishape, jax.ShapeDtypeStruct((N_EXPERTS, V7X_LANES), F32)],
        scratch_shapes=[pltpu.VMEM((N_EXPERTS, V7X_LANES), F32)],
        compiler_params=_cp(("arbitrary", "arbitrary")),
        name="expert_ranks",
    )(e0, e1, tri)


MOE_TM = 256


def _row_copy(src_hbm, row, dst, r, sem):
    return pltpu.make_async_copy(src_hbm.at[pl.ds(row, 1), :], dst.at[pl.ds(r, 1), :], sem)


def _row_gather(idx_ref, base, src_hbm, dst, sem, n):
    def issue(r, carry):
        _row_copy(src_hbm, idx_ref[base + r], dst, r, sem).start()
        return carry

    lax.fori_loop(0, n, issue, 0, unroll=8)


def _wait_rows(buf, sem):
    pltpu.make_async_copy(buf, buf, sem).wait()


def _moe_group_kernel(te_ref, nu_ref, src_ref, x_hbm, wg_ref, wu_ref, wd_ref, y_ref,
                      xbuf, wg_bf, wu_bf, wd_bf, sem):
    i = pl.program_id(0)
    n_used = nu_ref[0]
    tm = xbuf.shape[1]

    @pl.when(i == 0)
    def _():
        _row_gather(src_ref, 0, x_hbm, xbuf.at[0], sem.at[0], tm)

    @pl.when(jnp.logical_or(i == 0, te_ref[i] != te_ref[jnp.maximum(i - 1, 0)]))
    def _():
        wg_bf[...] = wg_ref[0, 0].astype(BF16)
        wu_bf[...] = wu_ref[0, 0].astype(BF16)
        wd_bf[...] = wd_ref[0, 0].astype(BF16)

    @pl.when(i < n_used)
    def _():
        slot = i % 2
        nxt = 1 - slot
        _wait_rows(xbuf.at[slot], sem.at[slot])
        base = jnp.minimum(i + 1, n_used - 1) * tm
        for r in range(tm):
            _row_copy(x_hbm, src_ref[base + r], xbuf.at[nxt], r, sem.at[nxt]).start()
        xb = xbuf[slot].astype(BF16)
        hg = jnp.dot(xb, wg_bf[...], preferred_element_type=F32)
        hu = jnp.dot(xb, wu_bf[...], preferred_element_type=F32)
        h = (hg / (1.0 + jnp.exp(-hg))) * hu
        y_ref[...] = jnp.dot(h.astype(BF16), wd_bf[...], preferred_element_type=F32)

    @pl.when(i == n_used - 1)
    def _():
        _wait_rows(xbuf.at[(i + 1) % 2], sem.at[(i + 1) % 2])

    @pl.when(i >= n_used)
    def _():
        y_ref[...] = jnp.zeros(y_ref.shape, y_ref.dtype)


def moe_grouped_ffn(x_rows, src, tile_expert, n_used, w_gate, w_up, w_down, layer):
    d = D_MODEL
    p = src.shape[0]
    tm = MOE_TM
    grid_spec = pltpu.PrefetchScalarGridSpec(
        num_scalar_prefetch=3,
        grid=(p // tm,),
        in_specs=[pl.BlockSpec(memory_space=pl.ANY),
                  pl.BlockSpec((1, 1, d, D_EXPERT), lambda i, te, nu, sr: (layer, te[i], 0, 0)),
                  pl.BlockSpec((1, 1, d, D_EXPERT), lambda i, te, nu, sr: (layer, te[i], 0, 0)),
                  pl.BlockSpec((1, 1, D_EXPERT, d), lambda i, te, nu, sr: (layer, te[i], 0, 0))],
        out_specs=pl.BlockSpec((tm, d), lambda i, te, nu, sr: (i, 0)),
        scratch_shapes=[pltpu.VMEM((2, tm, d), F32),
                        pltpu.VMEM((d, D_EXPERT), BF16), pltpu.VMEM((d, D_EXPERT), BF16),
                        pltpu.VMEM((D_EXPERT, d), BF16), pltpu.SemaphoreType.DMA((2,))],
    )
    return pl.pallas_call(
        _moe_group_kernel, grid_spec=grid_spec,
        out_shape=jax.ShapeDtypeStruct((p, d), F32),
        compiler_params=_cp(("arbitrary",)),
        name="moe_grouped_ffn",
    )(tile_expert, n_used, src, x_rows, w_gate, w_up, w_down)


def _moe_combine_kernel(p0_ref, p1_ref, y_hbm, w0_ref, w1_ref, x_ref, gm_ref, g_ref, *rest, final):
    if final:
        o_ref, buf0, buf1, sem = rest
    else:
        sc_ref, sh_ref, xo_ref, a_ref, buf0, buf1, sem = rest
    i = pl.program_id(0)
    tm = buf0.shape[1]

    def gather(tile, slot):
        _row_gather(p0_ref, tile * tm, y_hbm, buf0.at[slot], sem.at[0, slot], tm)
        _row_gather(p1_ref, tile * tm, y_hbm, buf1.at[slot], sem.at[1, slot], tm)

    @pl.when(i == 0)
    def _():
        gather(0, 0)

    @pl.when(i + 1 < pl.num_programs(0))
    def _():
        gather(i + 1, (i + 1) % 2)

    slot = i % 2
    _wait_rows(buf0.at[slot], sem.at[0, slot])
    _wait_rows(buf1.at[slot], sem.at[1, slot])
    moe = w0_ref[...] * buf0[slot] + w1_ref[...] * buf1[slot]
    x_new = x_ref[...] + gm_ref[0] * moe
    if final:
        o_ref[...] = _rmsnorm(x_new, g_ref[...])
    else:
        xo_ref[...] = x_new
        a_ref[...] = _norm_mod(x_new, g_ref[...], sc_ref[0], sh_ref[0]).astype(a_ref.dtype)


def moe_combine(y, pos0, pos1, w0, w1, x, mod3, gate_idx, g, next_mod3=None, sc_idx=None, sh_idx=None, tm=256):
    b, s, d = x.shape
    t = b * s
    tm = min(tm, s)
    per_batch = s // tm
    final = next_mod3 is None

    def chunk(idx):
        return pl.BlockSpec((1, 1, d), lambda i, p0, p1: (i // per_batch, 0, idx))

    row = pl.BlockSpec((tm, d), lambda i, p0, p1: (i, 0))
    in_specs = [pl.BlockSpec(memory_space=pl.ANY),
                pl.BlockSpec((tm, 1), lambda i, p0, p1: (i, 0)),
                pl.BlockSpec((tm, 1), lambda i, p0, p1: (i, 0)),
                row, chunk(gate_idx), pl.BlockSpec((1, d), lambda i, p0, p1: (0, 0))]
    operands = [y, w0.reshape(t, 1), w1.reshape(t, 1), x.reshape(t, d), mod3, g.reshape(1, d)]
    if final:
        out_specs, out_shape = row, jax.ShapeDtypeStruct((t, d), F32)
    else:
        in_specs += [chunk(sc_idx), chunk(sh_idx)]
        operands += [next_mod3, next_mod3]
        out_specs = [row, row]
        out_shape = [jax.ShapeDtypeStruct((t, d), F32), jax.ShapeDtypeStruct((t, d), BF16)]
    grid_spec = pltpu.PrefetchScalarGridSpec(
        num_scalar_prefetch=2, grid=(t // tm,), in_specs=in_specs, out_specs=out_specs,
        scratch_shapes=[pltpu.VMEM((2, tm, d), F32), pltpu.VMEM((2, tm, d), F32),
                        pltpu.SemaphoreType.DMA((2, 2))],
    )
    out = pl.pallas_call(
        functools.partial(_moe_combine_kernel, final=final), grid_spec=grid_spec, out_shape=out_shape,
        compiler_params=_cp(("arbitrary",)),
        name="moe_combine_final" if final else "moe_combine",
    )(pos0, pos1, *operands)
    if final:
        return out.reshape(b, s, d)
    return out[0].reshape(b, s, d), out[1].reshape(b, s, d)


def moe_sparse(a2, e0, e1, w0, w1, w_gate, w_up, w_down, layer, x, mod3, gate_idx, g, **next_norm):
    b, s, d = x.shape
    t = b * s
    tm = MOE_TM
    n_tiles = (2 * t + N_EXPERTS * (tm - 1)) // tm
    r0, r1, cnt = expert_ranks(e0, e1)
    counts = cnt[:, 0].astype(jnp.int32)
    padded = (counts + tm - 1) // tm * tm
    ends = jnp.cumsum(padded)
    offs = ends - padded
    e0f, e1f = e0.reshape(t), e1.reshape(t)
    pos0 = offs[e0f] + r0.reshape(t)
    pos1 = offs[e1f] + r1.reshape(t)
    tok = jnp.arange(t, dtype=jnp.int32)
    src = jnp.zeros((n_tiles * tm,), jnp.int32).at[jnp.concatenate([pos0, pos1])].set(
        jnp.concatenate([tok, tok]), unique_indices=True)
    tile_start = jnp.arange(n_tiles, dtype=jnp.int32) * tm
    tile_expert = jnp.minimum(jnp.sum((tile_start[:, None] >= ends[None, :]).astype(jnp.int32), axis=1),
                              N_EXPERTS - 1)
    n_used = (ends[-1:] // tm).astype(jnp.int32)
    y = moe_grouped_ffn(a2.reshape(t, d), src, tile_expert, n_used, w_gate, w_up, w_down, layer)
    return moe_combine(y, pos0, pos1, w0, w1, x, mod3, gate_idx, g, **next_norm)


def _rot_cols(w):
    half = w.shape[-1] // 2
    return jnp.concatenate([-w[..., half:], w[..., :half]], axis=-1)


_O_KR = Q_LORA + KV_LORA
_O_DIL = _O_KR + ROPE_DIM
_O_SB = _O_DIL + 3 * DIL_WIDTH
_O_GATES = _O_SB + 3 * SB_WIDTH
IN_COLS = _O_GATES + 3 * D_MODEL
DIL_Q_SCALE = DIL_HEAD_DIM ** -0.5 * LOG2E
SB_Q_SCALE = -(SB_HEAD_DIM ** -0.5) * LOG2E

_PACK_PIECES = [(_O_GATES, 3 * D_MODEL, None), (_O_SB, SB_WIDTH, SB_Q_SCALE), (_O_SB + SB_WIDTH, 2 * SB_WIDTH, None)]
for _g in range(DIL_GROUPS):
    _PACK_PIECES += [(_O_DIL + _g * DIL_GW, DIL_GW, DIL_Q_SCALE),
                     (_O_DIL + DIL_WIDTH + _g * DIL_GW, DIL_GW, None),
                     (_O_DIL + 2 * DIL_WIDTH + _g * DIL_GW, DIL_GW, None)]
_PACK_PIECES.append((0, Q_LORA + KV_LORA, None))
PACK_WIDTH = sum(width for _, width, _ in _PACK_PIECES)
PACKED_COLS = dict(gates=(0, 3 * D_MODEL), sb=(3 * D_MODEL, 3 * SB_WIDTH))
for _g in range(DIL_GROUPS):
    PACKED_COLS[f"dil{_g}"] = (3 * D_MODEL + 3 * SB_WIDTH + _g * 3 * DIL_GW, 3 * DIL_GW)
PACKED_COLS["lat"] = (3 * D_MODEL + 3 * SB_WIDTH + 3 * DIL_WIDTH, Q_LORA + KV_LORA)


PACK_BLOCK = 512


def _pack_kernel(src_ref, scale_ref, wt_ref, o_ref):
    del src_ref
    o_ref[...] = (wt_ref[0].T * scale_ref[pl.program_id(0)]).astype(o_ref.dtype)


def pack_projection_weights(w_in_t, layer):
    _, n, d = w_in_t.shape
    assert n == IN_COLS
    src_rows, scales = [], []
    for src, width, scale in _PACK_PIECES:
        assert width % PACK_BLOCK == 0
        for k in range(width // PACK_BLOCK):
            row = src + k * PACK_BLOCK
            assert row % ROPE_DIM == 0
            src_rows.append(row // ROPE_DIM)
            scales.append(1.0 if scale is None else scale)
    grid_spec = pltpu.PrefetchScalarGridSpec(
        num_scalar_prefetch=1,
        grid=(len(src_rows),),
        in_specs=[pl.BlockSpec(memory_space=pltpu.SMEM),
                  pl.BlockSpec((pl.Element(1), pl.Element(PACK_BLOCK), pl.Element(d)),
                               lambda j, src: (layer, src[j] * ROPE_DIM, 0))],
        out_specs=pl.BlockSpec((d, PACK_BLOCK), lambda j, src: (0, j)),
    )
    return pl.pallas_call(
        _pack_kernel, grid_spec=grid_spec,
        out_shape=jax.ShapeDtypeStruct((d, PACK_WIDTH), BF16),
        compiler_params=_cp(("arbitrary",)),
        name="pack_projection_weights",
    )(jnp.asarray(src_rows, jnp.int32), jnp.asarray(scales, F32), w_in_t)


def _pack_rope_kernel(wt_ref, o_ref):
    t = wt_ref[0]
    half = ROPE_DIM // 2
    both = jnp.concatenate([t, -t[half:], t[:half]], axis=0)
    o_ref[...] = both.T.astype(o_ref.dtype)


def pack_rope_weights(w_in_t, layer):
    d = w_in_t.shape[2]
    return pl.pallas_call(
        _pack_rope_kernel, grid=(1,),
        in_specs=[pl.BlockSpec((1, ROPE_DIM, d), lambda i: (layer, _O_KR // ROPE_DIM, 0))],
        out_specs=pl.BlockSpec((d, 2 * ROPE_DIM), lambda i: (0, 0)),
        out_shape=jax.ShapeDtypeStruct((d, 2 * ROPE_DIM), BF16),
        compiler_params=_cp(("arbitrary",)),
        name="pack_rope_weights",
    )(w_in_t)


def _prep_layer(w_in_t, layer, w_uq, w_ukv, w_o_mla, w_o_dil, w_o_sb, w_out):
    w_pack = pack_projection_weights(w_in_t, layer)
    w_kr = pack_rope_weights(w_in_t, layer)
    q_scale = MLA_QK ** -0.5 * LOG2E
    wq = (w_uq * q_scale).reshape(Q_LORA, MLA_HEADS, MLA_QK).transpose(1, 0, 2)
    wq_h = jnp.concatenate([wq, _rot_cols(wq[..., NOPE_DIM:])], axis=-1).astype(BF16)
    wkv_h = w_ukv.reshape(KV_LORA, MLA_HEADS, NOPE_DIM + MLA_V_DIM).transpose(1, 0, 2).astype(BF16)
    return dict(w_pack=w_pack, w_kr=w_kr, wq_h=wq_h, wkv_h=wkv_h,
                w_o_mla=w_o_mla.astype(BF16), w_o_dil=w_o_dil.astype(BF16), w_o_sb=w_o_sb.astype(BF16),
                w_out=w_out.astype(BF16))


def kernel(x, c, positions, w_ada, b_ada, g_mix, g_moe, w_in, g_q, w_uq, g_kv, w_ukv, w_o_mla, w_o_dil,
           w_o_sb, w_out, w_router, b_router, w_gate, w_up, w_down, g_final):
    b, s, d = x.shape
    depth = w_ada.shape[0]
    pos_f = positions.astype(F32)
    cos, sin = rope_tables(pos_f.reshape(b, s, 1))
    mod = ada_mod(c, w_ada, b_ada)
    w_router_t = w_router.T
    w_in_t = jnp.swapaxes(w_in, 1, 2)
    mods = [mod[l].reshape(b, 1, 6 * d) for l in range(depth)]
    a = norm_mod(x, g_mix[0], mods[0], sc_idx=1, sh_idx=0)
    for l in range(depth):
        p = _prep_layer(w_in_t, l, w_uq[l], w_ukv[l], w_o_mla[l], w_o_dil[l], w_o_sb[l], w_out[l])
        mod3 = mods[l]

        wp = p["w_pack"]
        lat = project(a, wp, *PACKED_COLS["lat"])
        kr = project(a, p["w_kr"], 0, 2 * ROPE_DIM)
        dil = [project_by_residue(a, wp, *PACKED_COLS[f"dil{g}"], DIL_RATES[g]) for g in range(DIL_GROUPS)]
        sbp = project(a, wp, *PACKED_COLS["sb"], tn=1024)
        gl = project(a, wp, *PACKED_COLS["gates"], tn=1024)

        q, k, v = mla_project(lat, kr, g_q[l], p["wq_h"], g_kv[l], p["wkv_h"], cos, sin)
        y_mla = mla_attention(q, k, v)
        dil_out = [dil_group_attention(dil[g], pos_f, g) for g in range(DIL_GROUPS)]
        y_sb = sb_attention(sbp)
        merged = merge_branches(y_mla, [o for o, _ in dil_out], [e for _, e in dil_out], y_sb, gl,
                                p["w_o_mla"], p["w_o_dil"], p["w_o_sb"])
        x, a2, logits_t = resid_project_norm_router(merged, p["w_out"], x, mod3, 2, g_moe[l], 4, 3, w_router_t)
        e0, e1, w0, w1 = router_top2(logits_t, b_router)
        moe_args = (a2, e0, e1, w0, w1, w_gate, w_up, w_down, l, x, mod3, 5)
        if l + 1 < depth:
            x, a = moe_sparse(*moe_args, g_mix[l + 1], next_mod3=mods[l + 1], sc_idx=1, sh_idx=0)
        else:
            out = moe_sparse(*moe_args, g_final)
    return out
```

```python
import functools
import math

import jax
import jax.numpy as jnp
from jax import lax
from jax.experimental import pallas as pl
from jax.experimental.pallas import tpu as pltpu

D_MODEL = 2048
EPS = 1e-6
MLA_HEADS = 8
Q_LORA = 512
KV_LORA = 512
NOPE_DIM = 128
ROPE_DIM = 64
MLA_V_DIM = 128
ROPE_THETA = 10000.0
MLA_QK = NOPE_DIM + ROPE_DIM
DIL_WINDOWS = (128, 512, 2048)
DIL_RATES = (1, 4, 16)
DIL_GROUPS = 3
DIL_HEADS = 4
DIL_HEAD_DIM = 128
DIL_SPAN = DIL_WINDOWS[0] // DIL_RATES[0]
DIL_GW = DIL_HEADS * DIL_HEAD_DIM
DIL_WIDTH = DIL_GROUPS * DIL_GW
ALIBI_MAX_BIAS = 8.0
SB_HEADS = 8
SB_HEAD_DIM = 128
SB_WIDTH = SB_HEADS * SB_HEAD_DIM
N_EXPERTS = 16
N_GROUPS = 4
EXPERTS_PER_GROUP = N_EXPERTS // N_GROUPS
D_EXPERT = 512

LOG2E = math.log2(math.e)
LN2 = math.log(2.0)
NEG_BIG = -1e30

V7X_LANES = 128
V7X_VMEM_BYTES = 64 * 1024 * 1024
VMEM_LIMIT = 56 * 1024 * 1024

F32 = jnp.float32
BF16 = jnp.bfloat16


def _cp(sem, vmem=VMEM_LIMIT):
    return pltpu.CompilerParams(dimension_semantics=sem, vmem_limit_bytes=vmem)


def _lane_tile(x, n):
    return x if n == 1 else jnp.concatenate([x] * n, axis=1)


def _nt_dot(a, b):
    return lax.dot_general(a, b, (((1,), (1,)), ((), ())), preferred_element_type=F32)


def _ada_kernel(c_ref, w_ref, b_ref, o_ref):
    c = c_ref[...]
    ca = c * (1.0 / (1.0 + jnp.exp(-c)))
    o_ref[0] = jnp.dot(ca, w_ref[0], preferred_element_type=F32,
                       precision=lax.Precision.HIGHEST) + b_ref[0]


def ada_mod(c, w_ada, b_ada, tn=1024):
    depth, d, n = w_ada.shape
    b = c.shape[0]
    return pl.pallas_call(
        _ada_kernel,
        grid=(depth, n // tn),
        in_specs=[pl.BlockSpec((b, d), lambda l, j: (0, 0)),
                  pl.BlockSpec((1, d, tn), lambda l, j: (l, 0, j)),
                  pl.BlockSpec((1, 1, tn), lambda l, j: (l, 0, j))],
        out_specs=pl.BlockSpec((1, b, tn), lambda l, j: (l, 0, j)),
        out_shape=jax.ShapeDtypeStruct((depth, b, n), F32),
        compiler_params=_cp(("arbitrary", "arbitrary")),
        name="ada_mod",
    )(c, w_ada, b_ada.reshape(depth, 1, n))


def _rope_table_kernel(pos_ref, inv_ref, cos_ref, sin_ref):
    ang = pos_ref[0] * inv_ref[...]
    cos_ref[0] = jnp.cos(ang)
    sin_ref[0] = jnp.sin(ang)


def rope_tables(pos_col, tm=512):
    b, s, _ = pos_col.shape
    tm = min(tm, s)
    half = ROPE_DIM // 2
    inv = ROPE_THETA ** (-jnp.arange(half, dtype=F32) / half)
    inv2 = jnp.concatenate([inv, inv]).reshape(1, ROPE_DIM)
    shp = jax.ShapeDtypeStruct((b, s, ROPE_DIM), F32)
    return pl.pallas_call(
        _rope_table_kernel,
        grid=(b, s // tm),
        in_specs=[pl.BlockSpec((1, tm, 1), lambda i, j: (i, j, 0)),
                  pl.BlockSpec((1, ROPE_DIM), lambda i, j: (0, 0))],
        out_specs=[pl.BlockSpec((1, tm, ROPE_DIM), lambda i, j: (i, j, 0))] * 2,
        out_shape=[shp, shp],
        compiler_params=_cp(("arbitrary", "arbitrary")),
        name="rope_tables",
    )(pos_col, inv2)


def _rmsnorm(x, g):
    return x * lax.rsqrt(jnp.mean(x * x, axis=-1, keepdims=True) + EPS) * g


def _norm_mod(x, g, scale, shift):
    return _rmsnorm(x, g) * (1.0 + scale) + shift


def _norm_kernel(x_ref, g_ref, sc_ref, sh_ref, o_ref):
    o_ref[0] = _norm_mod(x_ref[0], g_ref[...], sc_ref[0], sh_ref[0]).astype(o_ref.dtype)


def norm_mod(x, g, mod3, sc_idx, sh_idx, tm=512):
    b, s, d = x.shape
    tm = min(tm, s)
    return pl.pallas_call(
        _norm_kernel, grid=(b, s // tm),
        in_specs=[pl.BlockSpec((1, tm, d), lambda i, j: (i, j, 0)),
                  pl.BlockSpec((1, d), lambda i, j: (0, 0)),
                  pl.BlockSpec((1, 1, d), lambda i, j: (i, 0, sc_idx)),
                  pl.BlockSpec((1, 1, d), lambda i, j: (i, 0, sh_idx))],
        out_specs=pl.BlockSpec((1, tm, d), lambda i, j: (i, j, 0)),
        out_shape=jax.ShapeDtypeStruct((b, s, d), BF16),
        compiler_params=_cp(("arbitrary", "arbitrary")), name="norm_mod",
    )(x, g.reshape(1, d), mod3, mod3)


def _matmul_kernel(a_ref, w_ref, o_ref):
    o_ref[0] = jnp.dot(a_ref[0], w_ref[...], preferred_element_type=F32).astype(o_ref.dtype)


def project(a, w, col0, n, tm=1024, tn=512, out_dtype=BF16):
    b, s, k = a.shape
    tm = min(tm, s)
    tn = min(tn, n)
    assert n % tn == 0 and s % tm == 0 and col0 % tn == 0
    cb0 = col0 // tn
    return pl.pallas_call(
        _matmul_kernel,
        grid=(b, s // tm, n // tn),
        in_specs=[pl.BlockSpec((1, tm, k), lambda i, j, c: (i, j, 0)),
                  pl.BlockSpec((k, tn), lambda i, j, c: (0, cb0 + c))],
        out_specs=pl.BlockSpec((1, tm, tn), lambda i, j, c: (i, j, c)),
        out_shape=jax.ShapeDtypeStruct((b, s, n), out_dtype),
        compiler_params=_cp(("arbitrary", "arbitrary", "arbitrary")),
        name="project",
    )(a, w)


def _matmul_residue_kernel(a_ref, w_ref, o_ref, res_scr, *, rate):
    res = jnp.dot(a_ref[0], w_ref[...], preferred_element_type=F32)
    slabs, rows, lanes = res_scr.shape
    for c in range(slabs):
        res_scr[c] = res[:, c * lanes:(c + 1) * lanes]
    n = rows // rate
    for rho in range(rate):
        for c in range(slabs):
            o_ref[0, rho, :, c * lanes:(c + 1) * lanes] = (
                res_scr[c, pl.ds(rho, n, stride=rate), :].astype(o_ref.dtype))


def project_by_residue(a, w, col0, n, rate, tm=1024, tn=512):
    b, s, k = a.shape
    if rate == 1:
        return project(a, w, col0, n, tm=tm, tn=tn).reshape(b, 1, s, n)
    tm = min(tm, s)
    assert n % tn == 0 and s % tm == 0 and tm % (rate * 16) == 0 and col0 % tn == 0
    cb0 = col0 // tn
    return pl.pallas_call(
        functools.partial(_matmul_residue_kernel, rate=rate),
        grid=(b, s // tm, n // tn),
        in_specs=[pl.BlockSpec((1, tm, k), lambda i, j, c: (i, j, 0)),
                  pl.BlockSpec((k, tn), lambda i, j, c: (0, cb0 + c))],
        out_specs=pl.BlockSpec((1, rate, tm // rate, tn), lambda i, j, c: (i, 0, j, c)),
        out_shape=jax.ShapeDtypeStruct((b, rate, s // rate, n), BF16),
        scratch_shapes=[pltpu.VMEM((tn // V7X_LANES, tm, V7X_LANES), F32)],
        compiler_params=_cp(("arbitrary", "arbitrary", "arbitrary")),
        name=f"project_by_residue_{rate}",
    )(a, w)


def _latent_norm(c, g):
    c = c.astype(F32)
    return (c * lax.rsqrt(jnp.mean(c * c, axis=-1, keepdims=True) + EPS) * g).astype(BF16)


def _mla_proj_kernel(cq_ref, ckv_ref, kr_ref, gq_ref, gkv_ref, wq_ref, wkv_ref, cos_ref, sin_ref,
                     q_ref, k_ref, v_ref):
    cos, sin = cos_ref[0], sin_ref[0]
    rq = _latent_norm(cq_ref[0], gq_ref[...])
    rkv = _latent_norm(ckv_ref[0], gkv_ref[...])
    kr = kr_ref[0].astype(F32)
    k_rope = (kr[:, :ROPE_DIM] * cos + kr[:, ROPE_DIM:] * sin).astype(k_ref.dtype)
    ones = jnp.ones((rq.shape[0], MLA_V_DIM), v_ref.dtype)
    for h in range(MLA_HEADS):
        r = jnp.dot(rq, wq_ref[h], preferred_element_type=F32)
        q_ref[0, h, :, :NOPE_DIM] = r[:, :NOPE_DIM].astype(q_ref.dtype)
        roped = r[:, NOPE_DIM:NOPE_DIM + ROPE_DIM] * cos + r[:, NOPE_DIM + ROPE_DIM:] * sin
        q_ref[0, h, :, NOPE_DIM:] = roped.astype(q_ref.dtype)
        r = jnp.dot(rkv, wkv_ref[h], preferred_element_type=F32)
        k_ref[0, h, :, :NOPE_DIM] = r[:, :NOPE_DIM].astype(k_ref.dtype)
        k_ref[0, h, :, NOPE_DIM:] = k_rope
        v_ref[0, h, :, :MLA_V_DIM] = r[:, NOPE_DIM:].astype(v_ref.dtype)
        v_ref[0, h, :, MLA_V_DIM:] = ones


def mla_project(lat, kr, g_q, wq_h, g_kv, wkv_h, cos, sin, tm=512):
    b, s, _ = lat.shape
    tm = min(tm, s)
    h = MLA_HEADS
    tab = pl.BlockSpec((1, tm, ROPE_DIM), lambda i, j: (i, j, 0))

    def const(shape):
        return pl.BlockSpec(shape, lambda i, j: (0,) * len(shape))

    def head_major(width):
        return pl.BlockSpec((1, h, tm, width), lambda i, j: (i, 0, j, 0))

    return pl.pallas_call(
        _mla_proj_kernel, grid=(b, s // tm),
        in_specs=[pl.BlockSpec((1, tm, Q_LORA), lambda i, j: (i, j, 0)),
                  pl.BlockSpec((1, tm, KV_LORA), lambda i, j: (i, j, 1)),
                  pl.BlockSpec((1, tm, 2 * ROPE_DIM), lambda i, j: (i, j, 0)),
                  const((1, Q_LORA)), const((1, KV_LORA)),
                  const((h, Q_LORA, 256)), const((h, KV_LORA, 256)), tab, tab],
        out_specs=[head_major(MLA_QK), head_major(MLA_QK), head_major(2 * MLA_V_DIM)],
        out_shape=[jax.ShapeDtypeStruct((b, h, s, MLA_QK), BF16),
                   jax.ShapeDtypeStruct((b, h, s, MLA_QK), BF16),
                   jax.ShapeDtypeStruct((b, h, s, 2 * MLA_V_DIM), BF16)],
        compiler_params=_cp(("arbitrary", "arbitrary")), name="mla_project",
    )(lat, lat, kr, g_q.reshape(1, Q_LORA), g_kv.reshape(1, KV_LORA), wq_h, wkv_h, cos, sin)


def _mla_attn_kernel(q_ref, k_ref, v_ref, o_ref, s0_scr, s1_scr, m_scr, acc_scr, *, tq, tk):
    qi = pl.program_id(2)
    m_scr[...] = jnp.full(m_scr.shape, NEG_BIG, F32)
    acc_scr[...] = jnp.zeros(acc_scr.shape, F32)

    def scores(c, dst, row0=0):
        start = pl.multiple_of(c * tk, tk)
        dst[row0:, :] = _nt_dot(q_ref[0, 0, row0:, :], k_ref[0, 0, pl.ds(start, tk), :])

    def absorb(src, c, col_off=None, row0=0):
        start = pl.multiple_of(c * tk, tk)
        s = src[row0:, :]
        n = tq - row0
        if col_off is not None:
            row = row0 + lax.broadcasted_iota(jnp.int32, (n, tk), 0)
            col = col_off + lax.broadcasted_iota(jnp.int32, (n, tk), 1)
            s = jnp.where(col <= row, s, NEG_BIG)
        m_prev = m_scr[row0:, :]
        m_new = jnp.maximum(m_prev, jnp.max(s, axis=-1, keepdims=True))
        p = jnp.exp2(s - _lane_tile(m_new, tk // V7X_LANES))
        alpha = jnp.exp2(m_prev - m_new)
        pv = jnp.dot(p.astype(BF16), v_ref[0, 0, pl.ds(start, tk), :], preferred_element_type=F32)
        acc_scr[row0:, :] = _lane_tile(alpha, 2) * acc_scr[row0:, :] + pv
        m_scr[row0:, :] = m_new

    scores(0, s0_scr)

    def body(i, carry):
        scores(2 * i + 1, s1_scr)
        absorb(s0_scr, 2 * i)
        scores(2 * i + 2, s0_scr)
        absorb(s1_scr, 2 * i + 1)
        return carry

    lax.fori_loop(0, qi, body, 0)
    scores(2 * qi + 1, s1_scr, row0=tk)
    absorb(s0_scr, 2 * qi, col_off=0)
    absorb(s1_scr, 2 * qi + 1, col_off=tk, row0=tk)
    acc = acc_scr[...]
    o_ref[0] = (acc[:, :MLA_V_DIM] / acc[:, MLA_V_DIM:]).astype(o_ref.dtype)


def mla_attention(q, k, v, tq=1024):
    b, h, s, _ = q.shape
    tq = min(tq, s)
    tk = tq // 2
    return pl.pallas_call(
        functools.partial(_mla_attn_kernel, tq=tq, tk=tk),
        grid=(b, h, s // tq),
        in_specs=[pl.BlockSpec((1, 1, tq, MLA_QK), lambda i, j, t: (i, j, t, 0)),
                  pl.BlockSpec((1, 1, s, MLA_QK), lambda i, j, t: (i, j, 0, 0)),
                  pl.BlockSpec((1, 1, s, 2 * MLA_V_DIM), lambda i, j, t: (i, j, 0, 0))],
        out_specs=pl.BlockSpec((1, tq, MLA_V_DIM), lambda i, j, t: (i, t, j)),
        out_shape=jax.ShapeDtypeStruct((b, s, h * MLA_V_DIM), BF16),
        scratch_shapes=[pltpu.VMEM((tq, tk), F32), pltpu.VMEM((tq, tk), F32),
                        pltpu.VMEM((tq, V7X_LANES), F32), pltpu.VMEM((tq, 2 * MLA_V_DIM), F32)],
        compiler_params=_cp(("arbitrary", "arbitrary", "arbitrary")),
        name="mla_attention",
    )(q, k, v)


SB_SKIP_LOG2 = 150.0


def _sb_attn_kernel(q_ref, k_ref, v_ref, tri_ref, o_ref, c_scr, acc_scr, *, tq, hp):
    qi = pl.program_id(2)
    c_scr[...] = jnp.zeros(c_scr.shape, F32)
    acc_scr[...] = jnp.zeros(acc_scr.shape, F32)
    reps = tq // V7X_LANES

    def chunk(j, diagonal):
        start = pl.multiple_of(j * tq, tq)
        heads = range(hp)
        cols = [slice(hh * SB_HEAD_DIM, (hh + 1) * SB_HEAD_DIM) for hh in heads]
        if diagonal:
            before = (lax.broadcasted_iota(jnp.int32, (tq, tq), 1)
                      < lax.broadcasted_iota(jnp.int32, (tq, tq), 0))
        zn = [_nt_dot(q_ref[0, :, cols[hh]], k_ref[0, pl.ds(start, tq), cols[hh]]) for hh in heads]
        lk = [jnp.minimum(z, 0.0) - jnp.log2(1.0 + jnp.exp2(-jnp.abs(z))) for z in zn]
        if diagonal:
            lk = [jnp.where(before, x, 0.0) for x in lk]
        hi = [x.astype(BF16) for x in lk]
        lo = [(x - h.astype(F32)).astype(BF16) for x, h in zip(lk, hi)]
        suffix = [jnp.dot(jnp.concatenate([h, l], axis=1), tri_ref[...], preferred_element_type=F32)
                  for h, l in zip(hi, lo)]
        c = [c_scr[hh] for hh in heads]
        a = [jnp.exp2(sf - z + _lane_tile(cc, reps)) for sf, z, cc in zip(suffix, zn, c)]
        if diagonal:
            a = [jnp.where(before, x, 0.0) for x in a]
        for hh in heads:
            acc_scr[:, cols[hh]] += jnp.dot(a[hh].astype(BF16), v_ref[0, pl.ds(start, tq), cols[hh]],
                                            preferred_element_type=F32)
            c_scr[hh] = c[hh] + jnp.sum(lk[hh], axis=-1, keepdims=True)

    chunk(qi, True)

    def cond(state):
        j, go = state
        return jnp.logical_and(j >= 0, go > 0)

    def body(state):
        j, _ = state
        chunk(j, False)
        go = (jnp.max(c_scr[...]) > -SB_SKIP_LOG2).astype(jnp.int32)
        return j - 1, go

    lax.while_loop(cond, body, (qi - 1, jnp.int32(1)))
    o_ref[0] = acc_scr[...].astype(o_ref.dtype)


def sb_attention(qkv, tq=256, hp=4):
    b, s, _ = qkv.shape
    tq = min(tq, s)
    hw = hp * SB_HEAD_DIM
    nhb = SB_HEADS // hp
    tri = (lax.broadcasted_iota(jnp.int32, (tq, tq), 0)
           >= lax.broadcasted_iota(jnp.int32, (tq, tq), 1)).astype(BF16)
    tri2 = jnp.concatenate([tri, tri], axis=0)
    return pl.pallas_call(
        functools.partial(_sb_attn_kernel, tq=tq, hp=hp),
        grid=(b, nhb, s // tq),
        in_specs=[pl.BlockSpec((1, tq, hw), lambda i, j, t: (i, t, j)),
                  pl.BlockSpec((1, s, hw), lambda i, j, t: (i, 0, nhb + j)),
                  pl.BlockSpec((1, s, hw), lambda i, j, t: (i, 0, 2 * nhb + j)),
                  pl.BlockSpec((2 * tq, tq), lambda i, j, t: (0, 0))],
        out_specs=pl.BlockSpec((1, tq, hw), lambda i, j, t: (i, t, j)),
        out_shape=jax.ShapeDtypeStruct((b, s, SB_WIDTH), BF16),
        scratch_shapes=[pltpu.VMEM((hp, tq, V7X_LANES), F32), pltpu.VMEM((tq, hw), F32)],
        compiler_params=_cp(("arbitrary", "arbitrary", "arbitrary")),
        name="sb_attention",
    )(qkv, qkv, qkv, tri2)


def _dil_kernel(q_ref, kc_ref, vc_ref, kp_ref, vp_ref, pq_ref, pkc_ref, pkp_ref, sl_ref,
                o_ref, lse_ref, *, tq):
    ti = pl.program_id(2)
    w = DIL_SPAN
    a_idx = lax.broadcasted_iota(jnp.int32, (w, 2 * w), 0)
    c_idx = lax.broadcasted_iota(jnp.int32, (w, 2 * w), 1)
    window = jnp.logical_and(c_idx >= a_idx, c_idx <= a_idx + w)
    first_window = jnp.logical_and(window, jnp.logical_or(c_idx >= w, ti > 0))
    for sb in range(tq // w):
        rows = slice(sb * w, (sb + 1) * w)
        both = slice((sb - 1) * w, (sb + 1) * w)
        ok = first_window if sb == 0 else window
        pq = pq_ref[0, 0, rows, :]
        if sb == 0:
            pk = jnp.concatenate([pkp_ref[0, 0], pkc_ref[0, 0, :, rows]], axis=1)
        else:
            pk = pkc_ref[0, 0, :, both]
        dist = jnp.abs(pq - pk)
        heads = range(DIL_HEADS)
        cols = [slice(h * DIL_HEAD_DIM, (h + 1) * DIL_HEAD_DIM) for h in heads]

        def keys(cur_ref, prev_ref, h):
            if sb == 0:
                return jnp.concatenate([prev_ref[0, :, cols[h]], cur_ref[0, rows, cols[h]]], axis=0)
            return cur_ref[0, both, cols[h]]

        z = [_nt_dot(q_ref[0, rows, cols[h]], keys(kc_ref, kp_ref, h)) for h in heads]
        s = [jnp.where(ok, z[h] - sl_ref[h] * dist, NEG_BIG) for h in heads]
        m = [jnp.max(s[h], axis=-1, keepdims=True) for h in heads]
        p = [jnp.exp2(s[h] - m[h]) for h in heads]
        den = [jnp.sum(p[h], axis=-1, keepdims=True) for h in heads]
        for h in heads:
            o = jnp.dot(p[h].astype(BF16), keys(vc_ref, vp_ref, h), preferred_element_type=F32) / den[h]
            o_ref[0, rows, cols[h]] = o
            lse = (m[h] + jnp.log2(den[h])) * LN2
            lse_ref[0, rows, cols[h]] = jnp.broadcast_to(lse, (w, DIL_HEAD_DIM))


def dil_group_attention(dil_g, pos_f, g, tq=512):
    b, r, sr, c = dil_g.shape
    assert r == DIL_RATES[g]
    tq = min(tq, sr)
    w = DIL_SPAN
    assert sr % tq == 0 and tq % w == 0
    nsub = tq // w
    view = dil_g.reshape(b * r, sr, c)
    pos_r = pos_f.reshape(b, sr, r).transpose(0, 2, 1)
    pos_q = pos_r[..., None]
    pos_k = pos_r[:, :, None, :]
    n_h = DIL_GROUPS * DIL_HEADS
    slopes = 2.0 ** (-ALIBI_MAX_BIAS * jnp.arange(1, n_h + 1, dtype=F32) / n_h)
    slopes2 = (slopes * LOG2E)[g * DIL_HEADS:(g + 1) * DIL_HEADS]

    def cur(part):
        return pl.BlockSpec((1, tq, DIL_GW), lambda i, p, t: (i * r + p, t, part))

    def prev(part):
        return pl.BlockSpec((1, w, DIL_GW), lambda i, p, t: (i * r + p, jnp.maximum(t * nsub - 1, 0), part))

    out_spec = pl.BlockSpec((1, tq, DIL_GW), lambda i, p, t: (i * r + p, t, 0))
    out_shape = jax.ShapeDtypeStruct((b * r, sr, DIL_GW), F32)
    o, lse = pl.pallas_call(
        functools.partial(_dil_kernel, tq=tq),
        grid=(b, r, sr // tq),
        in_specs=[cur(0), cur(1), cur(2), prev(1), prev(2),
                  pl.BlockSpec((1, 1, tq, 1), lambda i, p, t: (i, p, t, 0)),
                  pl.BlockSpec((1, 1, 1, tq), lambda i, p, t: (i, p, 0, t)),
                  pl.BlockSpec((1, 1, 1, w), lambda i, p, t: (i, p, 0, jnp.maximum(t * nsub - 1, 0))),
                  pl.BlockSpec(memory_space=pltpu.SMEM)],
        out_specs=[out_spec, out_spec],
        out_shape=[out_shape, out_shape],
        compiler_params=_cp(("arbitrary", "arbitrary", "arbitrary")),
        name=f"dil_attention_g{g}",
    )(view, view, view, view, view, pos_q, pos_k, pos_k, slopes2)
    return o.reshape(b, r, sr, DIL_GW), lse.reshape(b, r, sr, DIL_GW)


def _token_order(ref, scr):
    _, r, n, width = ref.shape
    if r == 1:
        return ref[0, 0]
    slabs = width // V7X_LANES
    for rho in range(r):
        for c in range(slabs):
            scr[c, pl.ds(rho, n, stride=r), :] = ref[0, rho, :, c * V7X_LANES:(c + 1) * V7X_LANES]
    return jnp.concatenate([scr[c] for c in range(slabs)], axis=1)


def _merge_kernel(ym_ref, o0_ref, o1_ref, o2_ref, l0_ref, l1_ref, l2_ref, ys_ref,
                  g0_ref, g1_ref, g2_ref, wm_ref, wd_ref, ws_ref, out_ref, *scr):
    o0, o1, o2 = _token_order(o0_ref, scr[0]), _token_order(o1_ref, scr[1]), _token_order(o2_ref, scr[2])
    l0, l1, l2 = _token_order(l0_ref, scr[3]), _token_order(l1_ref, scr[4]), _token_order(l2_ref, scr[5])
    mx = jnp.maximum(jnp.maximum(l0, l1), l2)
    e0, e1, e2 = jnp.exp(l0 - mx), jnp.exp(l1 - mx), jnp.exp(l2 - mx)
    y_dil = (e0 * o0 + e1 * o1 + e2 * o2) / (e0 + e1 + e2)

    def sig(ref):
        return 1.0 / (1.0 + jnp.exp(-ref[0].astype(F32)))

    merged = (sig(g0_ref) * jnp.dot(ym_ref[0], wm_ref[...], preferred_element_type=F32)
              + sig(g1_ref) * jnp.dot(y_dil.astype(BF16), wd_ref[...], preferred_element_type=F32)
              + sig(g2_ref) * jnp.dot(ys_ref[0], ws_ref[...], preferred_element_type=F32))
    out_ref[0] = merged.astype(out_ref.dtype)


def merge_branches(y_mla, dil_o, dil_lse, y_sb, gates, w_o_mla, w_o_dil, w_o_sb, tm=256):
    b, s, _ = y_mla.shape
    d = D_MODEL
    tm = min(tm, s)

    def row(width, cb=0):
        return pl.BlockSpec((1, tm, width), lambda i, j: (i, j, cb))

    def by_residue(r):
        return pl.BlockSpec((1, r, tm // r, DIL_GW), lambda i, j: (i, 0, j, 0))

    def full(w):
        return pl.BlockSpec(w.shape, lambda i, j: (0, 0))

    dil_specs = [by_residue(r) for r in DIL_RATES]
    return pl.pallas_call(
        _merge_kernel, grid=(b, s // tm),
        in_specs=[row(y_mla.shape[2])] + dil_specs + dil_specs + [row(SB_WIDTH),
                  row(d, 0), row(d, 1), row(d, 2), full(w_o_mla), full(w_o_dil), full(w_o_sb)],
        out_specs=row(d),
        out_shape=jax.ShapeDtypeStruct((b, s, d), BF16),
        scratch_shapes=[pltpu.VMEM((DIL_GW // V7X_LANES, tm, V7X_LANES), F32)] * (2 * DIL_GROUPS),
        compiler_params=_cp(("arbitrary", "arbitrary")),
        name="merge_branches",
    )(y_mla, *dil_o, *dil_lse, y_sb, gates, gates, gates, w_o_mla, w_o_dil, w_o_sb)


def _resid_norm_router_kernel(a_ref, w_ref, x_ref, ga_ref, g_ref, sc_ref, sh_ref, wr_ref,
                              xo_ref, a2_ref, lg_ref):
    x_new = x_ref[0] + ga_ref[0] * jnp.dot(a_ref[0], w_ref[...], preferred_element_type=F32)
    xo_ref[0] = x_new
    a2 = _norm_mod(x_new, g_ref[...], sc_ref[0], sh_ref[0])
    a2_ref[0] = a2
    lg_ref[0] = lax.dot_general(wr_ref[...], a2, (((1,), (1,)), ((), ())),
                                preferred_element_type=F32, precision=lax.Precision.HIGHEST)


def resid_project_norm_router(a, w, x, mod3, gate_idx, g, sc_idx, sh_idx, w_router_t, tm=512):
    b, s, d = x.shape
    k = a.shape[2]
    e = w_router_t.shape[0]
    tm = min(tm, s)
    row = pl.BlockSpec((1, tm, d), lambda i, j: (i, j, 0))

    def chunk(idx):
        return pl.BlockSpec((1, 1, d), lambda i, j: (i, 0, idx))

    big = jax.ShapeDtypeStruct((b, s, d), F32)
    return pl.pallas_call(
        _resid_norm_router_kernel, grid=(b, s // tm),
        in_specs=[pl.BlockSpec((1, tm, k), lambda i, j: (i, j, 0)),
                  pl.BlockSpec((k, d), lambda i, j: (0, 0)),
                  row, chunk(gate_idx),
                  pl.BlockSpec((1, d), lambda i, j: (0, 0)), chunk(sc_idx), chunk(sh_idx),
                  pl.BlockSpec((e, d), lambda i, j: (0, 0))],
        out_specs=[row, row, pl.BlockSpec((1, e, tm), lambda i, j: (i, 0, j))],
        out_shape=[big, big, jax.ShapeDtypeStruct((b, e, s), F32)],
        compiler_params=_cp(("arbitrary", "arbitrary")),
        name="resid_project_norm_router",
    )(a, w, x, mod3, g.reshape(1, d), mod3, mod3, w_router_t)


def _router_kernel(lg_ref, b_ref, e0_ref, e1_ref, w0_ref, w1_ref):
    lg = lg_ref[0]
    tm = lg.shape[1]
    sc = [1.0 / (1.0 + jnp.exp(-lg[e:e + 1, :])) for e in range(N_EXPERTS)]
    bi = [sc[e] + b_ref[e] for e in range(N_EXPERTS)]
    n = EXPERTS_PER_GROUP
    gscore = []
    for g in range(N_GROUPS):
        v = bi[g * n:(g + 1) * n]
        pair_max = None
        for a in range(n):
            for c in range(a + 1, n):
                pm = v[a] + v[c]
                pair_max = pm if pair_max is None else jnp.maximum(pair_max, pm)
        gscore.append(pair_max)
    best = gscore[0]
    gsel = jnp.zeros((1, tm), jnp.int32)
    for g in range(1, N_GROUPS):
        better = gscore[g] > best
        best = jnp.where(better, gscore[g], best)
        gsel = jnp.where(better, g, gsel)
    gb = [sum(jnp.where(gsel == g, bi[g * n + i], 0.0) for g in range(N_GROUPS)) for i in range(n)]
    gs = [sum(jnp.where(gsel == g, sc[g * n + i], 0.0) for g in range(N_GROUPS)) for i in range(n)]
    sel = []
    for i in range(n):
        beaten = jnp.zeros((1, tm), jnp.int32)
        for j in range(n):
            if j == i:
                continue
            wins = (gb[j] > gb[i]) if j > i else (gb[j] >= gb[i])
            beaten = beaten + wins.astype(jnp.int32)
        sel.append(beaten < 2)
    den = sum(jnp.where(sel[i], gs[i], 0.0) for i in range(n))
    i0 = jnp.where(sel[0], 0, jnp.where(sel[1], 1, 2))
    i1 = jnp.where(sel[3], 3, jnp.where(sel[2], 2, 1))
    e0_ref[0] = gsel * n + i0
    e1_ref[0] = gsel * n + i1
    w0_ref[0] = sum(jnp.where(i0 == i, gs[i], 0.0) for i in range(n)) / den
    w1_ref[0] = sum(jnp.where(i1 == i, gs[i], 0.0) for i in range(n)) / den


def router_top2(logits_t, b_router, tm=1024):
    b, e, s = logits_t.shape
    tm = min(tm, s)
    row = pl.BlockSpec((1, 1, tm), lambda i, j: (i, 0, j))
    ishape = jax.ShapeDtypeStruct((b, 1, s), jnp.int32)
    fshape = jax.ShapeDtypeStruct((b, 1, s), F32)
    return pl.pallas_call(
        _router_kernel, grid=(b, s // tm),
        in_specs=[pl.BlockSpec((1, e, tm), lambda i, j: (i, 0, j)),
                  pl.BlockSpec(memory_space=pltpu.SMEM)],
        out_specs=[row, row, row, row],
        out_shape=[ishape, ishape, fshape, fshape],
        compiler_params=_cp(("arbitrary", "arbitrary")),
        name="router_top2",
    )(logits_t, b_router)


def _rank_kernel(e0_ref, e1_ref, tri_ref, r0_ref, r1_ref, cnt_ref, carry_scr):
    @pl.when(jnp.logical_and(pl.program_id(0) == 0, pl.program_id(1) == 0))
    def _():
        carry_scr[...] = jnp.zeros(carry_scr.shape, F32)

    e0, e1 = e0_ref[0], e1_ref[0]
    tb = e0.shape[1]
    eid = lax.broadcasted_iota(jnp.int32, (N_EXPERTS, tb), 0)
    hit0 = eid == e0
    hit1 = eid == e1
    used = jnp.where(jnp.logical_or(hit0, hit1), 1.0, 0.0)
    before = jnp.dot(used.astype(BF16), tri_ref[...], preferred_element_type=F32)
    rank = before + _lane_tile(carry_scr[...], tb // V7X_LANES)
    r0_ref[0] = jnp.sum(jnp.where(hit0, rank, 0.0), axis=0, keepdims=True).astype(jnp.int32)
    r1_ref[0] = jnp.sum(jnp.where(hit1, rank, 0.0), axis=0, keepdims=True).astype(jnp.int32)
    carry_scr[...] = carry_scr[...] + jnp.sum(used, axis=1, keepdims=True)
    cnt_ref[...] = carry_scr[...]


def expert_ranks(e0, e1, tb=512):
    b, _, s = e0.shape
    tb = min(tb, s)
    tri = (lax.broadcasted_iota(jnp.int32, (tb, tb), 0)
           < lax.broadcasted_iota(jnp.int32, (tb, tb), 1)).astype(BF16)
    row = pl.BlockSpec((1, 1, tb), lambda i, j: (i, 0, j))
    ishape = jax.ShapeDtypeStruct((b, 1, s), jnp.int32)
    return pl.pallas_call(
        _rank_kernel, grid=(b, s // tb),
        in_specs=[row, row, pl.BlockSpec((tb, tb), lambda i, j: (0, 0))],
        out_specs=[row, row, pl.BlockSpec((N_EXPERTS, V7X_LANES), lambda i, j: (0, 0))],
        out_shape=[ishape, ishape, jax.ShapeDtypeStruct((N_EXPERTS, V7X_LANES), F32)],
        scratch_shapes=[pltpu.VMEM((N_EXPERTS, V7X_LANES), F32)],
        compiler_params=_cp(("arbitrary", "arbitrary")),
        name="expert_ranks",
    )(e0, e1, tri)


MOE_TM = 256


def _row_copy(src_hbm, row, dst, r, sem):
    return pltpu.make_async_copy(src_hbm.at[pl.ds(row, 1), :], dst.at[pl.ds(r, 1), :], sem)


def _row_gather(idx_ref, base, src_hbm, dst, sem, n):
    def issue(r, carry):
        _row_copy(src_hbm, idx_ref[base + r], dst, r, sem).start()
        return carry

    lax.fori_loop(0, n, issue, 0, unroll=8)


def _wait_rows(buf, sem):
    pltpu.make_async_copy(buf, buf, sem).wait()


def _moe_group_kernel(te_ref, nu_ref, src_ref, x_hbm, wg_ref, wu_ref, wd_ref, y_ref,
                      xbuf, wg_bf, wu_bf, wd_bf, sem):
    i = pl.program_id(0)
    n_used = nu_ref[0]
    tm = xbuf.shape[1]

    @pl.when(i == 0)
    def _():
        _row_gather(src_ref, 0, x_hbm, xbuf.at[0], sem.at[0], tm)

    @pl.when(jnp.logical_or(i == 0, te_ref[i] != te_ref[jnp.maximum(i - 1, 0)]))
    def _():
        wg_bf[...] = wg_ref[0, 0].astype(BF16)
        wu_bf[...] = wu_ref[0, 0].astype(BF16)
        wd_bf[...] = wd_ref[0, 0].astype(BF16)

    @pl.when(i < n_used)
    def _():
        slot = i % 2
        nxt = 1 - slot
        _wait_rows(xbuf.at[slot], sem.at[slot])
        base = jnp.minimum(i + 1, n_used - 1) * tm
        for r in range(tm):
            _row_copy(x_hbm, src_ref[base + r], xbuf.at[nxt], r, sem.at[nxt]).start()
        xb = xbuf[slot].astype(BF16)
        hg = jnp.dot(xb, wg_bf[...], preferred_element_type=F32)
        hu = jnp.dot(xb, wu_bf[...], preferred_element_type=F32)
        h = (hg / (1.0 + jnp.exp(-hg))) * hu
        y_ref[...] = jnp.dot(h.astype(BF16), wd_bf[...], preferred_element_type=F32)

    @pl.when(i == n_used - 1)
    def _():
        _wait_rows(xbuf.at[(i + 1) % 2], sem.at[(i + 1) % 2])

    @pl.when(i >= n_used)
    def _():
        y_ref[...] = jnp.zeros(y_ref.shape, y_ref.dtype)


def moe_grouped_ffn(x_rows, src, tile_expert, n_used, w_gate, w_up, w_down, layer):
    d = D_MODEL
    p = src.shape[0]
    tm = MOE_TM
    grid_spec = pltpu.PrefetchScalarGridSpec(
        num_scalar_prefetch=3,
        grid=(p // tm,),
        in_specs=[pl.BlockSpec(memory_space=pl.ANY),
                  pl.BlockSpec((1, 1, d, D_EXPERT), lambda i, te, nu, sr: (layer, te[i], 0, 0)),
                  pl.BlockSpec((1, 1, d, D_EXPERT), lambda i, te, nu, sr: (layer, te[i], 0, 0)),
                  pl.BlockSpec((1, 1, D_EXPERT, d), lambda i, te, nu, sr: (layer, te[i], 0, 0))],
        out_specs=pl.BlockSpec((tm, d), lambda i, te, nu, sr: (i, 0)),
        scratch_shapes=[pltpu.VMEM((2, tm, d), F32),
                        pltpu.VMEM((d, D_EXPERT), BF16), pltpu.VMEM((d, D_EXPERT), BF16),
                        pltpu.VMEM((D_EXPERT, d), BF16), pltpu.SemaphoreType.DMA((2,))],
    )
    return pl.pallas_call(
        _moe_group_kernel, grid_spec=grid_spec,
        out_shape=jax.ShapeDtypeStruct((p, d), F32),
        compiler_params=_cp(("arbitrary",)),
        name="moe_grouped_ffn",
    )(tile_expert, n_used, src, x_rows, w_gate, w_up, w_down)


def _moe_combine_kernel(p0_ref, p1_ref, y_hbm, w0_ref, w1_ref, x_ref, gm_ref, g_ref, *rest, final):
    if final:
        o_ref, buf0, buf1, sem = rest
    else:
        sc_ref, sh_ref, xo_ref, a_ref, buf0, buf1, sem = rest
    i = pl.program_id(0)
    tm = buf0.shape[1]

    def gather(tile, slot):
        _row_gather(p0_ref, tile * tm, y_hbm, buf0.at[slot], sem.at[0, slot], tm)
        _row_gather(p1_ref, tile * tm, y_hbm, buf1.at[slot], sem.at[1, slot], tm)

    @pl.when(i == 0)
    def _():
        gather(0, 0)

    @pl.when(i + 1 < pl.num_programs(0))
    def _():
        gather(i + 1, (i + 1) % 2)

    slot = i % 2
    _wait_rows(buf0.at[slot], sem.at[0, slot])
    _wait_rows(buf1.at[slot], sem.at[1, slot])
    moe = w0_ref[...] * buf0[slot] + w1_ref[...] * buf1[slot]
    x_new = x_ref[...] + gm_ref[0] * moe
    if final:
        o_ref[...] = _rmsnorm(x_new, g_ref[...])
    else:
        xo_ref[...] = x_new
        a_ref[...] = _norm_mod(x_new, g_ref[...], sc_ref[0], sh_ref[0]).astype(a_ref.dtype)


def moe_combine(y, pos0, pos1, w0, w1, x, mod3, gate_idx, g, next_mod3=None, sc_idx=None, sh_idx=None, tm=256):
    b, s, d = x.shape
    t = b * s
    tm = min(tm, s)
    per_batch = s // tm
    final = next_mod3 is None

    def chunk(idx):
        return pl.BlockSpec((1, 1, d), lambda i, p0, p1: (i // per_batch, 0, idx))

    row = pl.BlockSpec((tm, d), lambda i, p0, p1: (i, 0))
    in_specs = [pl.BlockSpec(memory_space=pl.ANY),
                pl.BlockSpec((tm, 1), lambda i, p0, p1: (i, 0)),
                pl.BlockSpec((tm, 1), lambda i, p0, p1: (i, 0)),
                row, chunk(gate_idx), pl.BlockSpec((1, d), lambda i, p0, p1: (0, 0))]
    operands = [y, w0.reshape(t, 1), w1.reshape(t, 1), x.reshape(t, d), mod3, g.reshape(1, d)]
    if final:
        out_specs, out_shape = row, jax.ShapeDtypeStruct((t, d), F32)
    else:
        in_specs += [chunk(sc_idx), chunk(sh_idx)]
        operands += [next_mod3, next_mod3]
        out_specs = [row, row]
        out_shape = [jax.ShapeDtypeStruct((t, d), F32), jax.ShapeDtypeStruct((t, d), BF16)]
    grid_spec = pltpu.PrefetchScalarGridSpec(
        num_scalar_prefetch=2, grid=(t // tm,), in_specs=in_specs, out_specs=out_specs,
        scratch_shapes=[pltpu.VMEM((2, tm, d), F32), pltpu.VMEM((2, tm, d), F32),
                        pltpu.SemaphoreType.DMA((2, 2))],
    )
    out = pl.pallas_call(
        functools.partial(_moe_combine_kernel, final=final), grid_spec=grid_spec, out_shape=out_shape,
        compiler_params=_cp(("arbitrary",)),
        name="moe_combine_final" if final else "moe_combine",
    )(pos0, pos1, *operands)
    if final:
        return out.reshape(b, s, d)
    return out[0].reshape(b, s, d), out[1].reshape(b, s, d)


def moe_sparse(a2, e0, e1, w0, w1, w_gate, w_up, w_down, layer, x, mod3, gate_idx, g, **next_norm):
    b, s, d = x.shape
    t = b * s
    tm = MOE_TM
    n_tiles = (2 * t + N_EXPERTS * (tm - 1)) // tm
    r0, r1, cnt = expert_ranks(e0, e1)
    counts = cnt[:, 0].astype(jnp.int32)
    padded = (counts + tm - 1) // tm * tm
    ends = jnp.cumsum(padded)
    offs = ends - padded
    e0f, e1f = e0.reshape(t), e1.reshape(t)
    pos0 = offs[e0f] + r0.reshape(t)
    pos1 = offs[e1f] + r1.reshape(t)
    tok = jnp.arange(t, dtype=jnp.int32)
    src = jnp.zeros((n_tiles * tm,), jnp.int32).at[jnp.concatenate([pos0, pos1])].set(
        jnp.concatenate([tok, tok]), unique_indices=True)
    tile_start = jnp.arange(n_tiles, dtype=jnp.int32) * tm
    tile_expert = jnp.minimum(jnp.sum((tile_start[:, None] >= ends[None, :]).astype(jnp.int32), axis=1),
                              N_EXPERTS - 1)
    n_used = (ends[-1:] // tm).astype(jnp.int32)
    y = moe_grouped_ffn(a2.reshape(t, d), src, tile_expert, n_used, w_gate, w_up, w_down, layer)
    return moe_combine(y, pos0, pos1, w0, w1, x, mod3, gate_idx, g, **next_norm)


def _rot_cols(w):
    half = w.shape[-1] // 2
    return jnp.concatenate([-w[..., half:], w[..., :half]], axis=-1)


_O_KR = Q_LORA + KV_LORA
_O_DIL = _O_KR + ROPE_DIM
_O_SB = _O_DIL + 3 * DIL_WIDTH
_O_GATES = _O_SB + 3 * SB_WIDTH
IN_COLS = _O_GATES + 3 * D_MODEL
DIL_Q_SCALE = DIL_HEAD_DIM ** -0.5 * LOG2E
SB_Q_SCALE = -(SB_HEAD_DIM ** -0.5) * LOG2E

_PACK_PIECES = [(_O_GATES, 3 * D_MODEL, None), (_O_SB, SB_WIDTH, SB_Q_SCALE), (_O_SB + SB_WIDTH, 2 * SB_WIDTH, None)]
for _g in range(DIL_GROUPS):
    _PACK_PIECES += [(_O_DIL + _g * DIL_GW, DIL_GW, DIL_Q_SCALE),
                     (_O_DIL + DIL_WIDTH + _g * DIL_GW, DIL_GW, None),
                     (_O_DIL + 2 * DIL_WIDTH + _g * DIL_GW, DIL_GW, None)]
_PACK_PIECES.append((0, Q_LORA + KV_LORA, None))
PACK_WIDTH = sum(width for _, width, _ in _PACK_PIECES)
PACKED_COLS = dict(gates=(0, 3 * D_MODEL), sb=(3 * D_MODEL, 3 * SB_WIDTH))
for _g in range(DIL_GROUPS):
    PACKED_COLS[f"dil{_g}"] = (3 * D_MODEL + 3 * SB_WIDTH + _g * 3 * DIL_GW, 3 * DIL_GW)
PACKED_COLS["lat"] = (3 * D_MODEL + 3 * SB_WIDTH + 3 * DIL_WIDTH, Q_LORA + KV_LORA)


PACK_BLOCK = 512


def _pack_kernel(src_ref, scale_ref, wt_ref, o_ref):
    del src_ref
    o_ref[...] = (wt_ref[0].T * scale_ref[pl.program_id(0)]).astype(o_ref.dtype)


def pack_projection_weights(w_in_t, layer):
    _, n, d = w_in_t.shape
    assert n == IN_COLS
    src_rows, scales = [], []
    for src, width, scale in _PACK_PIECES:
        assert width % PACK_BLOCK == 0
        for k in range(width // PACK_BLOCK):
            row = src + k * PACK_BLOCK
            assert row % ROPE_DIM == 0
            src_rows.append(row // ROPE_DIM)
            scales.append(1.0 if scale is None else scale)
    grid_spec = pltpu.PrefetchScalarGridSpec(
        num_scalar_prefetch=1,
        grid=(len(src_rows),),
        in_specs=[pl.BlockSpec(memory_space=pltpu.SMEM),
                  pl.BlockSpec((pl.Element(1), pl.Element(PACK_BLOCK), pl.Element(d)),
                               lambda j, src: (layer, src[j] * ROPE_DIM, 0))],
        out_specs=pl.BlockSpec((d, PACK_BLOCK), lambda j, src: (0, j)),
    )
    return pl.pallas_call(
        _pack_kernel, grid_spec=grid_spec,
        out_shape=jax.ShapeDtypeStruct((d, PACK_WIDTH), BF16),
        compiler_params=_cp(("arbitrary",)),
        name="pack_projection_weights",
    )(jnp.asarray(src_rows, jnp.int32), jnp.asarray(scales, F32), w_in_t)


def _pack_rope_kernel(wt_ref, o_ref):
    t = wt_ref[0]
    half = ROPE_DIM // 2
    both = jnp.concatenate([t, -t[half:], t[:half]], axis=0)
    o_ref[...] = both.T.astype(o_ref.dtype)


def pack_rope_weights(w_in_t, layer):
    d = w_in_t.shape[2]
    return pl.pallas_call(
        _pack_rope_kernel, grid=(1,),
        in_specs=[pl.BlockSpec((1, ROPE_DIM, d), lambda i: (layer, _O_KR // ROPE_DIM, 0))],
        out_specs=pl.BlockSpec((d, 2 * ROPE_DIM), lambda i: (0, 0)),
        out_shape=jax.ShapeDtypeStruct((d, 2 * ROPE_DIM), BF16),
        compiler_params=_cp(("arbitrary",)),
        name="pack_rope_weights",
    )(w_in_t)


def _prep_layer(w_in_t, layer, w_uq, w_ukv, w_o_mla, w_o_dil, w_o_sb, w_out):
    w_pack = pack_projection_weights(w_in_t, layer)
    w_kr = pack_rope_weights(w_in_t, layer)
    q_scale = MLA_QK ** -0.5 * LOG2E
    wq = (w_uq * q_scale).reshape(Q_LORA, MLA_HEADS, MLA_QK).transpose(1, 0, 2)
    wq_h = jnp.concatenate([wq, _rot_cols(wq[..., NOPE_DIM:])], axis=-1).astype(BF16)
    wkv_h = w_ukv.reshape(KV_LORA, MLA_HEADS, NOPE_DIM + MLA_V_DIM).transpose(1, 0, 2).astype(BF16)
    return dict(w_pack=w_pack, w_kr=w_kr, wq_h=wq_h, wkv_h=wkv_h,
                w_o_mla=w_o_mla.astype(BF16), w_o_dil=w_o_dil.astype(BF16), w_o_sb=w_o_sb.astype(BF16),
                w_out=w_out.astype(BF16))


def kernel(x, c, positions, w_ada, b_ada, g_mix, g_moe, w_in, g_q, w_uq, g_kv, w_ukv, w_o_mla, w_o_dil,
           w_o_sb, w_out, w_router, b_router, w_gate, w_up, w_down, g_final):
    b, s, d = x.shape
    depth = w_ada.shape[0]
    pos_f = positions.astype(F32)
    cos, sin = rope_tables(pos_f.reshape(b, s, 1))
    mod = ada_mod(c, w_ada, b_ada)
    w_router_t = w_router.T
    w_in_t = jnp.swapaxes(w_in, 1, 2)
    mods = [mod[l].reshape(b, 1, 6 * d) for l in range(depth)]
    a = norm_mod(x, g_mix[0], mods[0], sc_idx=1, sh_idx=0)
    for l in range(depth):
        p = _prep_layer(w_in_t, l, w_uq[l], w_ukv[l], w_o_mla[l], w_o_dil[l], w_o_sb[l], w_out[l])
        mod3 = mods[l]

        wp = p["w_pack"]
        lat = project(a, wp, *PACKED_COLS["lat"])
        kr = project(a, p["w_kr"], 0, 2 * ROPE_DIM)
        dil = [project_by_residue(a, wp, *PACKED_COLS[f"dil{g}"], DIL_RATES[g]) for g in range(DIL_GROUPS)]
        sbp = project(a, wp, *PACKED_COLS["sb"], tn=1024)
        gl = project(a, wp, *PACKED_COLS["gates"], tn=1024)

        q, k, v = mla_project(lat, kr, g_q[l], p["wq_h"], g_kv[l], p["wkv_h"], cos, sin)
        y_mla = mla_attention(q, k, v)
        dil_out = [dil_group_attention(dil[g], pos_f, g) for g in range(DIL_GROUPS)]
        y_sb = sb_attention(sbp)
        merged = merge_branches(y_mla, [o for o, _ in dil_out], [e for _, e in dil_out], y_sb, gl,
                                p["w_o_mla"], p["w_o_dil"], p["w_o_sb"])
        x, a2, logits_t = resid_project_norm_router(merged, p["w_out"], x, mod3, 2, g_moe[l], 4, 3, w_router_t)
        e0, e1, w0, w1 = router_top2(logits_t, b_router)
        moe_args = (a2, e0, e1, w0, w1, w_gate, w_up, w_down, l, x, mod3, 5)
        if l + 1 < depth:
            x, a = moe_sparse(*moe_args, g_mix[l + 1], next_mod3=mods[l + 1], sc_idx=1, sh_idx=0)
        else:
            out = moe_sparse(*moe_args, g_final)
    return out
```

```python
import functools
import math

import jax
import jax.numpy as jnp
from jax import lax
from jax.experimental import pallas as pl
from jax.experimental.pallas import tpu as pltpu

D_MODEL = 2048
EPS = 1e-6
MLA_HEADS = 8
Q_LORA = 512
KV_LORA = 512
NOPE_DIM = 128
ROPE_DIM = 64
MLA_V_DIM = 128
ROPE_THETA = 10000.0
MLA_QK = NOPE_DIM + ROPE_DIM
DIL_WINDOWS = (128, 512, 2048)
DIL_RATES = (1, 4, 16)
DIL_GROUPS = 3
DIL_HEADS = 4
DIL_HEAD_DIM = 128
DIL_SPAN = DIL_WINDOWS[0] // DIL_RATES[0]
DIL_GW = DIL_HEADS * DIL_HEAD_DIM
DIL_WIDTH = DIL_GROUPS * DIL_GW
ALIBI_MAX_BIAS = 8.0
SB_HEADS = 8
SB_HEAD_DIM = 128
SB_WIDTH = SB_HEADS * SB_HEAD_DIM
N_EXPERTS = 16
N_GROUPS = 4
EXPERTS_PER_GROUP = N_EXPERTS // N_GROUPS
D_EXPERT = 512

LOG2E = math.log2(math.e)
LN2 = math.log(2.0)
NEG_BIG = -1e30

V7X_LANES = 128
V7X_VMEM_BYTES = 64 * 1024 * 1024
VMEM_LIMIT = 56 * 1024 * 1024

F32 = jnp.float32
BF16 = jnp.bfloat16


def _cp(sem, vmem=VMEM_LIMIT):
    return pltpu.CompilerParams(dimension_semantics=sem, vmem_limit_bytes=vmem)


def _lane_tile(x, n):
    return x if n == 1 else jnp.concatenate([x] * n, axis=1)


def _nt_dot(a, b):
    return lax.dot_general(a, b, (((1,), (1,)), ((), ())), preferred_element_type=F32)


def _ada_kernel(c_ref, w_ref, b_ref, o_ref):
    c = c_ref[...]
    ca = c * (1.0 / (1.0 + jnp.exp(-c)))
    o_ref[0] = jnp.dot(ca, w_ref[0], preferred_element_type=F32,
                       precision=lax.Precision.HIGHEST) + b_ref[0]


def ada_mod(c, w_ada, b_ada, tn=1024):
    depth, d, n = w_ada.shape
    b = c.shape[0]
    return pl.pallas_call(
        _ada_kernel,
        grid=(depth, n // tn),
        in_specs=[pl.BlockSpec((b, d), lambda l, j: (0, 0)),
                  pl.BlockSpec((1, d, tn), lambda l, j: (l, 0, j)),
                  pl.BlockSpec((1, 1, tn), lambda l, j: (l, 0, j))],
        out_specs=pl.BlockSpec((1, b, tn), lambda l, j: (l, 0, j)),
        out_shape=jax.ShapeDtypeStruct((depth, b, n), F32),
        compiler_params=_cp(("arbitrary", "arbitrary")),
        name="ada_mod",
    )(c, w_ada, b_ada.reshape(depth, 1, n))


def _rope_table_kernel(pos_ref, inv_ref, cos_ref, sin_ref):
    ang = pos_ref[0] * inv_ref[...]
    cos_ref[0] = jnp.cos(ang)
    sin_ref[0] = jnp.sin(ang)


def rope_tables(pos_col, tm=512):
    b, s, _ = pos_col.shape
    tm = min(tm, s)
    half = ROPE_DIM // 2
    inv = ROPE_THETA ** (-jnp.arange(half, dtype=F32) / half)
    inv2 = jnp.concatenate([inv, inv]).reshape(1, ROPE_DIM)
    shp = jax.ShapeDtypeStruct((b, s, ROPE_DIM), F32)
    return pl.pallas_call(
        _rope_table_kernel,
        grid=(b, s // tm),
        in_specs=[pl.BlockSpec((1, tm, 1), lambda i, j: (i, j, 0)),
                  pl.BlockSpec((1, ROPE_DIM), lambda i, j: (0, 0))],
        out_specs=[pl.BlockSpec((1, tm, ROPE_DIM), lambda i, j: (i, j, 0))] * 2,
        out_shape=[shp, shp],
        compiler_params=_cp(("arbitrary", "arbitrary")),
        name="rope_tables",
    )(pos_col, inv2)


def _rmsnorm(x, g):
    return x * lax.rsqrt(jnp.mean(x * x, axis=-1, keepdims=True) + EPS) * g


def _norm_mod(x, g, scale, shift):
    return _rmsnorm(x, g) * (1.0 + scale) + shift


def _norm_kernel(x_ref, g_ref, sc_ref, sh_ref, o_ref):
    o_ref[0] = _norm_mod(x_ref[0], g_ref[...], sc_ref[0], sh_ref[0]).astype(o_ref.dtype)


def norm_mod(x, g, mod3, sc_idx, sh_idx, tm=512):
    b, s, d = x.shape
    tm = min(tm, s)
    return pl.pallas_call(
        _norm_kernel, grid=(b, s // tm),
        in_specs=[pl.BlockSpec((1, tm, d), lambda i, j: (i, j, 0)),
                  pl.BlockSpec((1, d), lambda i, j: (0, 0)),
                  pl.BlockSpec((1, 1, d), lambda i, j: (i, 0, sc_idx)),
                  pl.BlockSpec((1, 1, d), lambda i, j: (i, 0, sh_idx))],
        out_specs=pl.BlockSpec((1, tm, d), lambda i, j: (i, j, 0)),
        out_shape=jax.ShapeDtypeStruct((b, s, d), BF16),
        compiler_params=_cp(("arbitrary", "arbitrary")), name="norm_mod",
    )(x, g.reshape(1, d), mod3, mod3)


def _matmul_kernel(a_ref, w_ref, o_ref):
    o_ref[0] = jnp.dot(a_ref[0], w_ref[...], preferred_element_type=F32).astype(o_ref.dtype)


def project(a, w, col0, n, tm=1024, tn=512, out_dtype=BF16):
    b, s, k = a.shape
    tm = min(tm, s)
    tn = min(tn, n)
    assert n % tn == 0 and s % tm == 0 and col0 % tn == 0
    cb0 = col0 // tn
    return pl.pallas_call(
        _matmul_kernel,
        grid=(b, s // tm, n // tn),
        in_specs=[pl.BlockSpec((1, tm, k), lambda i, j, c: (i, j, 0)),
                  pl.BlockSpec((k, tn), lambda i, j, c: (0, cb0 + c))],
        out_specs=pl.BlockSpec((1, tm, tn), lambda i, j, c: (i, j, c)),
        out_shape=jax.ShapeDtypeStruct((b, s, n), out_dtype),
        compiler_params=_cp(("arbitrary", "arbitrary", "arbitrary")),
        name="project",
    )(a, w)


def _matmul_residue_kernel(a_ref, w_ref, o_ref, res_scr, *, rate):
    res = jnp.dot(a_ref[0], w_ref[...], preferred_element_type=F32)
    slabs, rows, lanes = res_scr.shape
    for c in range(slabs):
        res_scr[c] = res[:, c * lanes:(c + 1) * lanes]
    n = rows // rate
    for rho in range(rate):
        for c in range(slabs):
            o_ref[0, rho, :, c * lanes:(c + 1) * lanes] = (
                res_scr[c, pl.ds(rho, n, stride=rate), :].astype(o_ref.dtype))


def project_by_residue(a, w, col0, n, rate, tm=1024, tn=512):
    b, s, k = a.shape
    if rate == 1:
        return project(a, w, col0, n, tm=tm, tn=tn).reshape(b, 1, s, n)
    tm = min(tm, s)
    assert n % tn == 0 and s % tm == 0 and tm % (rate * 16) == 0 and col0 % tn == 0
    cb0 = col0 // tn
    return pl.pallas_call(
        functools.partial(_matmul_residue_kernel, rate=rate),
        grid=(b, s // tm, n // tn),
        in_specs=[pl.BlockSpec((1, tm, k), lambda i, j, c: (i, j, 0)),
                  pl.BlockSpec((k, tn), lambda i, j, c: (0, cb0 + c))],
        out_specs=pl.BlockSpec((1, rate, tm // rate, tn), lambda i, j, c: (i, 0, j, c)),
        out_shape=jax.ShapeDtypeStruct((b, rate, s // rate, n), BF16),
        scratch_shapes=[pltpu.VMEM((tn // V7X_LANES, tm, V7X_LANES), F32)],
        compiler_params=_cp(("arbitrary", "arbitrary", "arbitrary")),
        name=f"project_by_residue_{rate}",
    )(a, w)


def _latent_norm(c, g):
    c = c.astype(F32)
    return (c * lax.rsqrt(jnp.mean(c * c, axis=-1, keepdims=True) + EPS) * g).astype(BF16)


def _mla_proj_kernel(cq_ref, ckv_ref, kr_ref, gq_ref, gkv_ref, wq_ref, wkv_ref, cos_ref, sin_ref,
                     q_ref, k_ref, v_ref):
    cos, sin = cos_ref[0], sin_ref[0]
    rq = _latent_norm(cq_ref[0], gq_ref[...])
    rkv = _latent_norm(ckv_ref[0], gkv_ref[...])
    kr = kr_ref[0].astype(F32)
    k_rope = (kr[:, :ROPE_DIM] * cos + kr[:, ROPE_DIM:] * sin).astype(k_ref.dtype)
    ones = jnp.ones((rq.shape[0], MLA_V_DIM), v_ref.dtype)
    for h in range(MLA_HEADS):
        r = jnp.dot(rq, wq_ref[h], preferred_element_type=F32)
        q_ref[0, h, :, :NOPE_DIM] = r[:, :NOPE_DIM].astype(q_ref.dtype)
        roped = r[:, NOPE_DIM:NOPE_DIM + ROPE_DIM] * cos + r[:, NOPE_DIM + ROPE_DIM:] * sin
        q_ref[0, h, :, NOPE_DIM:] = roped.astype(q_ref.dtype)
        r = jnp.dot(rkv, wkv_ref[h], preferred_element_type=F32)
        k_ref[0, h, :, :NOPE_DIM] = r[:, :NOPE_DIM].astype(k_ref.dtype)
        k_ref[0, h, :, NOPE_DIM:] = k_rope
        v_ref[0, h, :, :MLA_V_DIM] = r[:, NOPE_DIM:].astype(v_ref.dtype)
        v_ref[0, h, :, MLA_V_DIM:] = ones


def mla_project(lat, kr, g_q, wq_h, g_kv, wkv_h, cos, sin, tm=512):
    b, s, _ = lat.shape
    tm = min(tm, s)
    h = MLA_HEADS
    tab = pl.BlockSpec((1, tm, ROPE_DIM), lambda i, j: (i, j, 0))

    def const(shape):
        return pl.BlockSpec(shape, lambda i, j: (0,) * len(shape))

    def head_major(width):
        return pl.BlockSpec((1, h, tm, width), lambda i, j: (i, 0, j, 0))

    return pl.pallas_call(
        _mla_proj_kernel, grid=(b, s // tm),
        in_specs=[pl.BlockSpec((1, tm, Q_LORA), lambda i, j: (i, j, 0)),
                  pl.BlockSpec((1, tm, KV_LORA), lambda i, j: (i, j, 1)),
                  pl.BlockSpec((1, tm, 2 * ROPE_DIM), lambda i, j: (i, j, 0)),
                  const((1, Q_LORA)), const((1, KV_LORA)),
                  const((h, Q_LORA, 256)), const((h, KV_LORA, 256)), tab, tab],
        out_specs=[head_major(MLA_QK), head_major(MLA_QK), head_major(2 * MLA_V_DIM)],
        out_shape=[jax.ShapeDtypeStruct((b, h, s, MLA_QK), BF16),
                   jax.ShapeDtypeStruct((b, h, s, MLA_QK), BF16),
                   jax.ShapeDtypeStruct((b, h, s, 2 * MLA_V_DIM), BF16)],
        compiler_params=_cp(("arbitrary", "arbitrary")), name="mla_project",
    )(lat, lat, kr, g_q.reshape(1, Q_LORA), g_kv.reshape(1, KV_LORA), wq_h, wkv_h, cos, sin)


def _mla_attn_kernel(q_ref, k_ref, v_ref, o_ref, s0_scr, s1_scr, m_scr, acc_scr, *, tq, tk):
    qi = pl.program_id(2)
    m_scr[...] = jnp.full(m_scr.shape, NEG_BIG, F32)
    acc_scr[...] = jnp.zeros(acc_scr.shape, F32)

    def scores(c, dst, row0=0):
        start = pl.multiple_of(c * tk, tk)
        dst[row0:, :] = _nt_dot(q_ref[0, 0, row0:, :], k_ref[0, 0, pl.ds(start, tk), :])

    def absorb(src, c, col_off=None, row0=0):
        start = pl.multiple_of(c * tk, tk)
        s = src[row0:, :]
        n = tq - row0
        if col_off is not None:
            row = row0 + lax.broadcasted_iota(jnp.int32, (n, tk), 0)
            col = col_off + lax.broadcasted_iota(jnp.int32, (n, tk), 1)
            s = jnp.where(col <= row, s, NEG_BIG)
        m_prev = m_scr[row0:, :]
        m_new = jnp.maximum(m_prev, jnp.max(s, axis=-1, keepdims=True))
        p = jnp.exp2(s - _lane_tile(m_new, tk // V7X_LANES))
        alpha = jnp.exp2(m_prev - m_new)
        pv = jnp.dot(p.astype(BF16), v_ref[0, 0, pl.ds(start, tk), :], preferred_element_type=F32)
        acc_scr[row0:, :] = _lane_tile(alpha, 2) * acc_scr[row0:, :] + pv
        m_scr[row0:, :] = m_new

    scores(0, s0_scr)

    def body(i, carry):
        scores(2 * i + 1, s1_scr)
        absorb(s0_scr, 2 * i)
        scores(2 * i + 2, s0_scr)
        absorb(s1_scr, 2 * i + 1)
        return carry

    lax.fori_loop(0, qi, body, 0)
    scores(2 * qi + 1, s1_scr, row0=tk)
    absorb(s0_scr, 2 * qi, col_off=0)
    absorb(s1_scr, 2 * qi + 1, col_off=tk, row0=tk)
    acc = acc_scr[...]
    o_ref[0] = (acc[:, :MLA_V_DIM] / acc[:, MLA_V_DIM:]).astype(o_ref.dtype)


def mla_attention(q, k, v, tq=1024):
    b, h, s, _ = q.shape
    tq = min(tq, s)
    tk = tq // 2
    return pl.pallas_call(
        functools.partial(_mla_attn_kernel, tq=tq, tk=tk),
        grid=(b, h, s // tq),
        in_specs=[pl.BlockSpec((1, 1, tq, MLA_QK), lambda i, j, t: (i, j, t, 0)),
                  pl.BlockSpec((1, 1, s, MLA_QK), lambda i, j, t: (i, j, 0, 0)),
                  pl.BlockSpec((1, 1, s, 2 * MLA_V_DIM), lambda i, j, t: (i, j, 0, 0))],
        out_specs=pl.BlockSpec((1, tq, MLA_V_DIM), lambda i, j, t: (i, t, j)),
        out_shape=jax.ShapeDtypeStruct((b, s, h * MLA_V_DIM), BF16),
        scratch_shapes=[pltpu.VMEM((tq, tk), F32), pltpu.VMEM((tq, tk), F32),
                        pltpu.VMEM((tq, V7X_LANES), F32), pltpu.VMEM((tq, 2 * MLA_V_DIM), F32)],
        compiler_params=_cp(("arbitrary", "arbitrary", "arbitrary")),
        name="mla_attention",
    )(q, k, v)


SB_SKIP_LOG2 = 150.0


def _sb_attn_kernel(q_ref, k_ref, v_ref, tri_ref, o_ref, c_scr, acc_scr, *, tq, hp):
    qi = pl.program_id(2)
    c_scr[...] = jnp.zeros(c_scr.shape, F32)
    acc_scr[...] = jnp.zeros(acc_scr.shape, F32)
    reps = tq // V7X_LANES

    def chunk(j, diagonal):
        start = pl.multiple_of(j * tq, tq)
        heads = range(hp)
        cols = [slice(hh * SB_HEAD_DIM, (hh + 1) * SB_HEAD_DIM) for hh in heads]
        if diagonal:
            before = (lax.broadcasted_iota(jnp.int32, (tq, tq), 1)
                      < lax.broadcasted_iota(jnp.int32, (tq, tq), 0))
        zn = [_nt_dot(q_ref[0, :, cols[hh]], k_ref[0, pl.ds(start, tq), cols[hh]]) for hh in heads]
        lk = [jnp.minimum(z, 0.0) - jnp.log2(1.0 + jnp.exp2(-jnp.abs(z))) for z in zn]
        if diagonal:
            lk = [jnp.where(before, x, 0.0) for x in lk]
        hi = [x.astype(BF16) for x in lk]
        lo = [(x - h.astype(F32)).astype(BF16) for x, h in zip(lk, hi)]
        suffix = [jnp.dot(jnp.concatenate([h, l], axis=1), tri_ref[...], preferred_element_type=F32)
                  for h, l in zip(hi, lo)]
        c = [c_scr[hh] for hh in heads]
        a = [jnp.exp2(sf - z + _lane_tile(cc, reps)) for sf, z, cc in zip(suffix, zn, c)]
        if diagonal:
            a = [jnp.where(before, x, 0.0) for x in a]
        for hh in heads:
            acc_scr[:, cols[hh]] += jnp.dot(a[hh].astype(BF16), v_ref[0, pl.ds(start, tq), cols[hh]],
                                            preferred_element_type=F32)
            c_scr[hh] = c[hh] + jnp.sum(lk[hh], axis=-1, keepdims=True)

    chunk(qi, True)

    def cond(state):
        j, go = state
        return jnp.logical_and(j >= 0, go > 0)

    def body(state):
        j, _ = state
        chunk(j, False)
        go = (jnp.max(c_scr[...]) > -SB_SKIP_LOG2).astype(jnp.int32)
        return j - 1, go

    lax.while_loop(cond, body, (qi - 1, jnp.int32(1)))
    o_ref[0] = acc_scr[...].astype(o_ref.dtype)


def sb_attention(qkv, tq=256, hp=4):
    b, s, _ = qkv.shape
    tq = min(tq, s)
    hw = hp * SB_HEAD_DIM
    nhb = SB_HEADS // hp
    tri = (lax.broadcasted_iota(jnp.int32, (tq, tq), 0)
           >= lax.broadcasted_iota(jnp.int32, (tq, tq), 1)).astype(BF16)
    tri2 = jnp.concatenate([tri, tri], axis=0)
    return pl.pallas_call(
        functools.partial(_sb_attn_kernel, tq=tq, hp=hp),
        grid=(b, nhb, s // tq),
        in_specs=[pl.BlockSpec((1, tq, hw), lambda i, j, t: (i, t, j)),
                  pl.BlockSpec((1, s, hw), lambda i, j, t: (i, 0, nhb + j)),
                  pl.BlockSpec((1, s, hw), lambda i, j, t: (i, 0, 2 * nhb + j)),
                  pl.BlockSpec((2 * tq, tq), lambda i, j, t: (0, 0))],
        out_specs=pl.BlockSpec((1, tq, hw), lambda i, j, t: (i, t, j)),
        out_shape=jax.ShapeDtypeStruct((b, s, SB_WIDTH), BF16),
        scratch_shapes=[pltpu.VMEM((hp, tq, V7X_LANES), F32), pltpu.VMEM((tq, hw), F32)],
        compiler_params=_cp(("arbitrary", "arbitrary", "arbitrary")),
        name="sb_attention",
    )(qkv, qkv, qkv, tri2)


def _dil_kernel(q_ref, kc_ref, vc_ref, kp_ref, vp_ref, pq_ref, pkc_ref, pkp_ref, sl_ref,
                o_ref, lse_ref, *, tq):
    ti = pl.program_id(2)
    w = DIL_SPAN
    a_idx = lax.broadcasted_iota(jnp.int32, (w, 2 * w), 0)
    c_idx = lax.broadcasted_iota(jnp.int32, (w, 2 * w), 1)
    window = jnp.logical_and(c_idx >= a_idx, c_idx <= a_idx + w)
    first_window = jnp.logical_and(window, jnp.logical_or(c_idx >= w, ti > 0))
    for sb in range(tq // w):
        rows = slice(sb * w, (sb + 1) * w)
        both = slice((sb - 1) * w, (sb + 1) * w)
        ok = first_window if sb == 0 else window
        pq = pq_ref[0, 0, rows, :]
        if sb == 0:
            pk = jnp.concatenate([pkp_ref[0, 0], pkc_ref[0, 0, :, rows]], axis=1)
        else:
            pk = pkc_ref[0, 0, :, both]
        dist = jnp.abs(pq - pk)
        heads = range(DIL_HEADS)
        cols = [slice(h * DIL_HEAD_DIM, (h + 1) * DIL_HEAD_DIM) for h in heads]

        def keys(cur_ref, prev_ref, h):
            if sb == 0:
                return jnp.concatenate([prev_ref[0, :, cols[h]], cur_ref[0, rows, cols[h]]], axis=0)
            return cur_ref[0, both, cols[h]]

        z = [_nt_dot(q_ref[0, rows, cols[h]], keys(kc_ref, kp_ref, h)) for h in heads]
        s = [jnp.where(ok, z[h] - sl_ref[h] * dist, NEG_BIG) for h in heads]
        m = [jnp.max(s[h], axis=-1, keepdims=True) for h in heads]
        p = [jnp.exp2(s[h] - m[h]) for h in heads]
        den = [jnp.sum(p[h], axis=-1, keepdims=True) for h in heads]
        for h in heads:
            o = jnp.dot(p[h].astype(BF16), keys(vc_ref, vp_ref, h), preferred_element_type=F32) / den[h]
            o_ref[0, rows, cols[h]] = o
            lse = (m[h] + jnp.log2(den[h])) * LN2
            lse_ref[0, rows, cols[h]] = jnp.broadcast_to(lse, (w, DIL_HEAD_DIM))


def dil_group_attention(dil_g, pos_f, g, tq=512):
    b, r, sr, c = dil_g.shape
    assert r == DIL_RATES[g]
    tq = min(tq, sr)
    w = DIL_SPAN
    assert sr % tq == 0 and tq % w == 0
    nsub = tq // w
    view = dil_g.reshape(b * r, sr, c)
    pos_r = pos_f.reshape(b, sr, r).transpose(0, 2, 1)
    pos_q = pos_r[..., None]
    pos_k = pos_r[:, :, None, :]
    n_h = DIL_GROUPS * DIL_HEADS
    slopes = 2.0 ** (-ALIBI_MAX_BIAS * jnp.arange(1, n_h + 1, dtype=F32) / n_h)
    slopes2 = (slopes * LOG2E)[g * DIL_HEADS:(g + 1) * DIL_HEADS]

    def cur(part):
        return pl.BlockSpec((1, tq, DIL_GW), lambda i, p, t: (i * r + p, t, part))

    def prev(part):
        return pl.BlockSpec((1, w, DIL_GW), lambda i, p, t: (i * r + p, jnp.maximum(t * nsub - 1, 0), part))

    out_spec = pl.BlockSpec((1, tq, DIL_GW), lambda i, p, t: (i * r + p, t, 0))
    out_shape = jax.ShapeDtypeStruct((b * r, sr, DIL_GW), F32)
    o, lse = pl.pallas_call(
        functools.partial(_dil_kernel, tq=tq),
        grid=(b, r, sr // tq),
        in_specs=[cur(0), cur(1), cur(2), prev(1), prev(2),
                  pl.BlockSpec((1, 1, tq, 1), lambda i, p, t: (i, p, t, 0)),
                  pl.BlockSpec((1, 1, 1, tq), lambda i, p, t: (i, p, 0, t)),
                  pl.BlockSpec((1, 1, 1, w), lambda i, p, t: (i, p, 0, jnp.maximum(t * nsub - 1, 0))),
                  pl.BlockSpec(memory_space=pltpu.SMEM)],
        out_specs=[out_spec, out_spec],
        out_shape=[out_shape, out_shape],
        compiler_params=_cp(("arbitrary", "arbitrary", "arbitrary")),
        name=f"dil_attention_g{g}",
    )(view, view, view, view, view, pos_q, pos_k, pos_k, slopes2)
    return o.reshape(b, r, sr, DIL_GW), lse.reshape(b, r, sr, DIL_GW)


def _token_order(ref, scr):
    _, r, n, width = ref.shape
    if r == 1:
        return ref[0, 0]
    slabs = width // V7X_LANES
    for rho in range(r):
        for c in range(slabs):
            scr[c, pl.ds(rho, n, stride=r), :] = ref[0, rho, :, c * V7X_LANES:(c + 1) * V7X_LANES]
    return jnp.concatenate([scr[c] for c in range(slabs)], axis=1)


def _merge_kernel(ym_ref, o0_ref, o1_ref, o2_ref, l0_ref, l1_ref, l2_ref, ys_ref,
                  g0_ref, g1_ref, g2_ref, wm_ref, wd_ref, ws_ref, out_ref, *scr):
    o0, o1, o2 = _token_order(o0_ref, scr[0]), _token_order(o1_ref, scr[1]), _token_order(o2_ref, scr[2])
    l0, l1, l2 = _token_order(l0_ref, scr[3]), _token_order(l1_ref, scr[4]), _token_order(l2_ref, scr[5])
    mx = jnp.maximum(jnp.maximum(l0, l1), l2)
    e0, e1, e2 = jnp.exp(l0 - mx), jnp.exp(l1 - mx), jnp.exp(l2 - mx)
    y_dil = (e0 * o0 + e1 * o1 + e2 * o2) / (e0 + e1 + e2)

    def sig(ref):
        return 1.0 / (1.0 + jnp.exp(-ref[0].astype(F32)))

    merged = (sig(g0_ref) * jnp.dot(ym_ref[0], wm_ref[...], preferred_element_type=F32)
              + sig(g1_ref) * jnp.dot(y_dil.astype(BF16), wd_ref[...], preferred_element_type=F32)
              + sig(g2_ref) * jnp.dot(ys_ref[0], ws_ref[...], preferred_element_type=F32))
    out_ref[0] = merged.astype(out_ref.dtype)


def merge_branches(y_mla, dil_o, dil_lse, y_sb, gates, w_o_mla, w_o_dil, w_o_sb, tm=256):
    b, s, _ = y_mla.shape
    d = D_MODEL
    tm = min(tm, s)

    def row(width, cb=0):
        return pl.BlockSpec((1, tm, width), lambda i, j: (i, j, cb))

    def by_residue(r):
        return pl.BlockSpec((1, r, tm // r, DIL_GW), lambda i, j: (i, 0, j, 0))

    def full(w):
        return pl.BlockSpec(w.shape, lambda i, j: (0, 0))

    dil_specs = [by_residue(r) for r in DIL_RATES]
    return pl.pallas_call(
        _merge_kernel, grid=(b, s // tm),
        in_specs=[row(y_mla.shape[2])] + dil_specs + dil_specs + [row(SB_WIDTH),
                  row(d, 0), row(d, 1), row(d, 2), full(w_o_mla), full(w_o_dil), full(w_o_sb)],
        out_specs=row(d),
        out_shape=jax.ShapeDtypeStruct((b, s, d), BF16),
        scratch_shapes=[pltpu.VMEM((DIL_GW // V7X_LANES, tm, V7X_LANES), F32)] * (2 * DIL_GROUPS),
        compiler_params=_cp(("arbitrary", "arbitrary")),
        name="merge_branches",
    )(y_mla, *dil_o, *dil_lse, y_sb, gates, gates, gates, w_o_mla, w_o_dil, w_o_sb)


def _resid_norm_router_kernel(a_ref, w_ref, x_ref, ga_ref, g_ref, sc_ref, sh_ref, wr_ref,
                              xo_ref, a2_ref, lg_ref):
    x_new = x_ref[0] + ga_ref[0] * jnp.dot(a_ref[0], w_ref[...], preferred_element_type=F32)
    xo_ref[0] = x_new
    a2 = _norm_mod(x_new, g_ref[...], sc_ref[0], sh_ref[0])
    a2_ref[0] = a2
    lg_ref[0] = lax.dot_general(wr_ref[...], a2, (((1,), (1,)), ((), ())),
                                preferred_element_type=F32, precision=lax.Precision.HIGHEST)


def resid_project_norm_router(a, w, x, mod3, gate_idx, g, sc_idx, sh_idx, w_router_t, tm=512):
    b, s, d = x.shape
    k = a.shape[2]
    e = w_router_t.shape[0]
    tm = min(tm, s)
    row = pl.BlockSpec((1, tm, d), lambda i, j: (i, j, 0))

    def chunk(idx):
        return pl.BlockSpec((1, 1, d), lambda i, j: (i, 0, idx))

    big = jax.ShapeDtypeStruct((b, s, d), F32)
    return pl.pallas_call(
        _resid_norm_router_kernel, grid=(b, s // tm),
        in_specs=[pl.BlockSpec((1, tm, k), lambda i, j: (i, j, 0)),
                  pl.BlockSpec((k, d), lambda i, j: (0, 0)),
                  row, chunk(gate_idx),
                  pl.BlockSpec((1, d), lambda i, j: (0, 0)), chunk(sc_idx), chunk(sh_idx),
                  pl.BlockSpec((e, d), lambda i, j: (0, 0))],
        out_specs=[row, row, pl.BlockSpec((1, e, tm), lambda i, j: (i, 0, j))],
        out_shape=[big, big, jax.ShapeDtypeStruct((b, e, s), F32)],
        compiler_params=_cp(("arbitrary", "arbitrary")),
        name="resid_project_norm_router",
    )(a, w, x, mod3, g.reshape(1, d), mod3, mod3, w_router_t)


def _router_kernel(lg_ref, b_ref, e0_ref, e1_ref, w0_ref, w1_ref):
    lg = lg_ref[0]
    tm = lg.shape[1]
    sc = [1.0 / (1.0 + jnp.exp(-lg[e:e + 1, :])) for e in range(N_EXPERTS)]
    bi = [sc[e] + b_ref[e] for e in range(N_EXPERTS)]
    n = EXPERTS_PER_GROUP
    gscore = []
    for g in range(N_GROUPS):
        v = bi[g * n:(g + 1) * n]
        pair_max = None
        for a in range(n):
            for c in range(a + 1, n):
                pm = v[a] + v[c]
                pair_max = pm if pair_max is None else jnp.maximum(pair_max, pm)
        gscore.append(pair_max)
    best = gscore[0]
    gsel = jnp.zeros((1, tm), jnp.int32)
    for g in range(1, N_GROUPS):
        better = gscore[g] > best
        best = jnp.where(better, gscore[g], best)
        gsel = jnp.where(better, g, gsel)
    gb = [sum(jnp.where(gsel == g, bi[g * n + i], 0.0) for g in range(N_GROUPS)) for i in range(n)]
    gs = [sum(jnp.where(gsel == g, sc[g * n + i], 0.0) for g in range(N_GROUPS)) for i in range(n)]
    sel = []
    for i in range(n):
        beaten = jnp.zeros((1, tm), jnp.int32)
        for j in range(n):
            if j == i:
                continue
            wins = (gb[j] > gb[i]) if j > i else (gb[j] >= gb[i])
            beaten = beaten + wins.astype(jnp.int32)
        sel.append(beaten < 2)
    den = sum(jnp.where(sel[i], gs[i], 0.0) for i in range(n))
    i0 = jnp.where(sel[0], 0, jnp.where(sel[1], 1, 2))
    i1 = jnp.where(sel[3], 3, jnp.where(sel[2], 2, 1))
    e0_ref[0] = gsel * n + i0
    e1_ref[0] = gsel * n + i1
    w0_ref[0] = sum(jnp.where(i0 == i, gs[i], 0.0) for i in range(n)) / den
    w1_ref[0] = sum(jnp.where(i1 == i, gs[i], 0.0) for i in range(n)) / den


def router_top2(logits_t, b_router, tm=1024):
    b, e, s = logits_t.shape
    tm = min(tm, s)
    row = pl.BlockSpec((1, 1, tm), lambda i, j: (i, 0, j))
    ishape = jax.ShapeDtypeStruct((b, 1, s), jnp.int32)
    fshape = jax.ShapeDtypeStruct((b, 1, s), F32)
    return pl.pallas_call(
        _router_kernel, grid=(b, s // tm),
        in_specs=[pl.BlockSpec((1, e, tm), lambda i, j: (i, 0, j)),
                  pl.BlockSpec(memory_space=pltpu.SMEM)],
        out_specs=[row, row, row, row],
        out_shape=[ishape, ishape, fshape, fshape],
        compiler_params=_cp(("arbitrary", "arbitrary")),
        name="router_top2",
    )(logits_t, b_router)


def _rank_kernel(e0_ref, e1_ref, tri_ref, r0_ref, r1_ref, cnt_ref, carry_scr):
    @pl.when(jnp.logical_and(pl.program_id(0) == 0, pl.program_id(1) == 0))
    def _():
        carry_scr[...] = jnp.zeros(carry_scr.shape, F32)

    e0, e1 = e0_ref[0], e1_ref[0]
    tb = e0.shape[1]
    eid = lax.broadcasted_iota(jnp.int32, (N_EXPERTS, tb), 0)
    hit0 = eid == e0
    hit1 = eid == e1
    used = jnp.where(jnp.logical_or(hit0, hit1), 1.0, 0.0)
    before = jnp.dot(used.astype(BF16), tri_ref[...], preferred_element_type=F32)
    rank = before + _lane_tile(carry_scr[...], tb // V7X_LANES)
    r0_ref[0] = jnp.sum(jnp.where(hit0, rank, 0.0), axis=0, keepdims=True).astype(jnp.int32)
    r1_ref[0] = jnp.sum(jnp.where(hit1, rank, 0.0), axis=0, keepdims=True).astype(jnp.int32)
    carry_scr[...] = carry_scr[...] + jnp.sum(used, axis=1, keepdims=True)
    cnt_ref[...] = carry_scr[...]


def expert_ranks(e0, e1, tb=512):
    b, _, s = e0.shape
    tb = min(tb, s)
    tri = (lax.broadcasted_iota(jnp.int32, (tb, tb), 0)
           < lax.broadcasted_iota(jnp.int32, (tb, tb), 1)).astype(BF16)
    row = pl.BlockSpec((1, 1, tb), lambda i, j: (i, 0, j))
    ishape = jax.ShapeDtypeStruct((b, 1, s), jnp.int32)
    return pl.pallas_call(
        _rank_kernel, grid=(b, s // tb),
        in_specs=[row, row, pl.BlockSpec((tb, tb), lambda i, j: (0, 0))],
        out_specs=[row, row, pl.BlockSpec((N_EXPERTS, V7X_LANES), lambda i, j: (0, 0))],
        out_shape=[ishape, ishape, jax.ShapeDtypeStruct((N_EXPERTS, V7X_LANES), F32)],
        scratch_shapes=[pltpu.VMEM((N_EXPERTS, V7X_LANES), F32)],
        compiler_params=_cp(("arbitrary", "arbitrary")),
        name="expert_ranks",
    )(e0, e1, tri)


MOE_TM = 256


def _row_copy(src_hbm, row, dst, r, sem):
    return pltpu.make_async_copy(src_hbm.at[pl.ds(row, 1), :], dst.at[pl.ds(r, 1), :], sem)


def _row_gather(idx_ref, base, src_hbm, dst, sem, n):
    def issue(r, carry):
        _row_copy(src_hbm, idx_ref[base + r], dst, r, sem).start()
        return carry

    lax.fori_loop(0, n, issue, 0, unroll=8)


def _wait_rows(buf, sem):
    pltpu.make_async_copy(buf, buf, sem).wait()


def _moe_group_kernel(te_ref, nu_ref, src_ref, x_hbm, wg_ref, wu_ref, wd_ref, y_ref,
                      xbuf, wg_bf, wu_bf, wd_bf, sem):
    i = pl.program_id(0)
    n_used = nu_ref[0]
    tm = xbuf.shape[1]

    @pl.when(i == 0)
    def _():
        _row_gather(src_ref, 0, x_hbm, xbuf.at[0], sem.at[0], tm)

    @pl.when(jnp.logical_or(i == 0, te_ref[i] != te_ref[jnp.maximum(i - 1, 0)]))
    def _():
        wg_bf[...] = wg_ref[0, 0].astype(BF16)
        wu_bf[...] = wu_ref[0, 0].astype(BF16)
        wd_bf[...] = wd_ref[0, 0].astype(BF16)

    @pl.when(i < n_used)
    def _():
        slot = i % 2
        nxt = 1 - slot
        _wait_rows(xbuf.at[slot], sem.at[slot])
        base = jnp.minimum(i + 1, n_used - 1) * tm
        for r in range(tm):
            _row_copy(x_hbm, src_ref[base + r], xbuf.at[nxt], r, sem.at[nxt]).start()
        xb = xbuf[slot].astype(BF16)
        hg = jnp.dot(xb, wg_bf[...], preferred_element_type=F32)
        hu = jnp.dot(xb, wu_bf[...], preferred_element_type=F32)
        h = (hg / (1.0 + jnp.exp(-hg))) * hu
        y_ref[...] = jnp.dot(h.astype(BF16), wd_bf[...], preferred_element_type=F32)

    @pl.when(i == n_used - 1)
    def _():
        _wait_rows(xbuf.at[(i + 1) % 2], sem.at[(i + 1) % 2])

    @pl.when(i >= n_used)
    def _():
        y_ref[...] = jnp.zeros(y_ref.shape, y_ref.dtype)


def moe_grouped_ffn(x_rows, src, tile_expert, n_used, w_gate, w_up, w_down, layer):
    d = D_MODEL
    p = src.shape[0]
    tm = MOE_TM
    grid_spec = pltpu.PrefetchScalarGridSpec(
        num_scalar_prefetch=3,
        grid=(p // tm,),
        in_specs=[pl.BlockSpec(memory_space=pl.ANY),
                  pl.BlockSpec((1, 1, d, D_EXPERT), lambda i, te, nu, sr: (layer, te[i], 0, 0)),
                  pl.BlockSpec((1, 1, d, D_EXPERT), lambda i, te, nu, sr: (layer, te[i], 0, 0)),
                  pl.BlockSpec((1, 1, D_EXPERT, d), lambda i, te, nu, sr: (layer, te[i], 0, 0))],
        out_specs=pl.BlockSpec((tm, d), lambda i, te, nu, sr: (i, 0)),
        scratch_shapes=[pltpu.VMEM((2, tm, d), F32),
                        pltpu.VMEM((d, D_EXPERT), BF16), pltpu.VMEM((d, D_EXPERT), BF16),
                        pltpu.VMEM((D_EXPERT, d), BF16), pltpu.SemaphoreType.DMA((2,))],
    )
    return pl.pallas_call(
        _moe_group_kernel, grid_spec=grid_spec,
        out_shape=jax.ShapeDtypeStruct((p, d), F32),
        compiler_params=_cp(("arbitrary",)),
        name="moe_grouped_ffn",
    )(tile_expert, n_used, src, x_rows, w_gate, w_up, w_down)


def _moe_combine_kernel(p0_ref, p1_ref, y_hbm, w0_ref, w1_ref, x_ref, gm_ref, g_ref, *rest, final):
    if final:
        o_ref, buf0, buf1, sem = rest
    else:
        sc_ref, sh_ref, xo_ref, a_ref, buf0, buf1, sem = rest
    i = pl.program_id(0)
    tm = buf0.shape[1]

    def gather(tile, slot):
        base = tile * tm

        def issue(r, carry):
            _row_copy(y_hbm, p0_ref[base + r], buf0.at[slot], r, sem.at[0, slot]).start()
            _row_copy(y_hbm, p1_ref[base + r], buf1.at[slot], r, sem.at[1, slot]).start()
            return carry

        lax.fori_loop(0, tm, issue, 0, unroll=8)

    @pl.when(i == 0)
    def _():
        gather(0, 0)

    @pl.when(i + 1 < pl.num_programs(0))
    def _():
        gather(i + 1, (i + 1) % 2)

    slot = i % 2
    _wait_rows(buf0.at[slot], sem.at[0, slot])
    _wait_rows(buf1.at[slot], sem.at[1, slot])
    moe = w0_ref[...] * buf0[slot] + w1_ref[...] * buf1[slot]
    x_new = x_ref[...] + gm_ref[0] * moe
    if final:
        o_ref[...] = _rmsnorm(x_new, g_ref[...])
    else:
        xo_ref[...] = x_new
        a_ref[...] = _norm_mod(x_new, g_ref[...], sc_ref[0], sh_ref[0]).astype(a_ref.dtype)


def moe_combine(y, pos0, pos1, w0, w1, x, mod3, gate_idx, g, next_mod3=None, sc_idx=None, sh_idx=None, tm=256):
    b, s, d = x.shape
    t = b * s
    tm = min(tm, s)
    per_batch = s // tm
    final = next_mod3 is None

    def chunk(idx):
        return pl.BlockSpec((1, 1, d), lambda i, p0, p1: (i // per_batch, 0, idx))

    row = pl.BlockSpec((tm, d), lambda i, p0, p1: (i, 0))
    in_specs = [pl.BlockSpec(memory_space=pl.ANY),
                pl.BlockSpec((tm, 1), lambda i, p0, p1: (i, 0)),
                pl.BlockSpec((tm, 1), lambda i, p0, p1: (i, 0)),
                row, chunk(gate_idx), pl.BlockSpec((1, d), lambda i, p0, p1: (0, 0))]
    operands = [y, w0.reshape(t, 1), w1.reshape(t, 1), x.reshape(t, d), mod3, g.reshape(1, d)]
    if final:
        out_specs, out_shape = row, jax.ShapeDtypeStruct((t, d), F32)
    else:
        in_specs += [chunk(sc_idx), chunk(sh_idx)]
        operands += [next_mod3, next_mod3]
        out_specs = [row, row]
        out_shape = [jax.ShapeDtypeStruct((t, d), F32), jax.ShapeDtypeStruct((t, d), BF16)]
    grid_spec = pltpu.PrefetchScalarGridSpec(
        num_scalar_prefetch=2, grid=(t // tm,), in_specs=in_specs, out_specs=out_specs,
        scratch_shapes=[pltpu.VMEM((2, tm, d), F32), pltpu.VMEM((2, tm, d), F32),
                        pltpu.SemaphoreType.DMA((2, 2))],
    )
    out = pl.pallas_call(
        functools.partial(_moe_combine_kernel, final=final), grid_spec=grid_spec, out_shape=out_shape,
        compiler_params=_cp(("arbitrary",)),
        name="moe_combine_final" if final else "moe_combine",
    )(pos0, pos1, *operands)
    if final:
        return out.reshape(b, s, d)
    return out[0].reshape(b, s, d), out[1].reshape(b, s, d)


def moe_sparse(a2, e0, e1, w0, w1, w_gate, w_up, w_down, layer, x, mod3, gate_idx, g, **next_norm):
    b, s, d = x.shape
    t = b * s
    tm = MOE_TM
    n_tiles = (2 * t + N_EXPERTS * (tm - 1)) // tm
    r0, r1, cnt = expert_ranks(e0, e1)
    counts = cnt[:, 0].astype(jnp.int32)
    padded = (counts + tm - 1) // tm * tm
    ends = jnp.cumsum(padded)
    offs = ends - padded
    e0f, e1f = e0.reshape(t), e1.reshape(t)
    pos0 = offs[e0f] + r0.reshape(t)
    pos1 = offs[e1f] + r1.reshape(t)
    tok = jnp.arange(t, dtype=jnp.int32)
    src = jnp.zeros((n_tiles * tm,), jnp.int32).at[jnp.concatenate([pos0, pos1])].set(
        jnp.concatenate([tok, tok]), unique_indices=True)
    tile_start = jnp.arange(n_tiles, dtype=jnp.int32) * tm
    tile_expert = jnp.minimum(jnp.sum((tile_start[:, None] >= ends[None, :]).astype(jnp.int32), axis=1),
                              N_EXPERTS - 1)
    n_used = (ends[-1:] // tm).astype(jnp.int32)
    y = moe_grouped_ffn(a2.reshape(t, d), src, tile_expert, n_used, w_gate, w_up, w_down, layer)
    return moe_combine(y, pos0, pos1, w0, w1, x, mod3, gate_idx, g, **next_norm)


def _rot_cols(w):
    half = w.shape[-1] // 2
    return jnp.concatenate([-w[..., half:], w[..., :half]], axis=-1)


_O_KR = Q_LORA + KV_LORA
_O_DIL = _O_KR + ROPE_DIM
_O_SB = _O_DIL + 3 * DIL_WIDTH
_O_GATES = _O_SB + 3 * SB_WIDTH
IN_COLS = _O_GATES + 3 * D_MODEL
DIL_Q_SCALE = DIL_HEAD_DIM ** -0.5 * LOG2E
SB_Q_SCALE = -(SB_HEAD_DIM ** -0.5) * LOG2E

_PACK_PIECES = [(_O_GATES, 3 * D_MODEL, None), (_O_SB, SB_WIDTH, SB_Q_SCALE), (_O_SB + SB_WIDTH, 2 * SB_WIDTH, None)]
for _g in range(DIL_GROUPS):
    _PACK_PIECES += [(_O_DIL + _g * DIL_GW, DIL_GW, DIL_Q_SCALE),
                     (_O_DIL + DIL_WIDTH + _g * DIL_GW, DIL_GW, None),
                     (_O_DIL + 2 * DIL_WIDTH + _g * DIL_GW, DIL_GW, None)]
_PACK_PIECES.append((0, Q_LORA + KV_LORA, None))
PACK_WIDTH = sum(width for _, width, _ in _PACK_PIECES)
PACKED_COLS = dict(gates=(0, 3 * D_MODEL), sb=(3 * D_MODEL, 3 * SB_WIDTH))
for _g in range(DIL_GROUPS):
    PACKED_COLS[f"dil{_g}"] = (3 * D_MODEL + 3 * SB_WIDTH + _g * 3 * DIL_GW, 3 * DIL_GW)
PACKED_COLS["lat"] = (3 * D_MODEL + 3 * SB_WIDTH + 3 * DIL_WIDTH, Q_LORA + KV_LORA)


PACK_BLOCK = 512


def _pack_kernel(src_ref, scale_ref, wt_ref, o_ref):
    del src_ref
    o_ref[...] = (wt_ref[0].T * scale_ref[pl.program_id(0)]).astype(o_ref.dtype)


def pack_projection_weights(w_in_t, layer):
    _, n, d = w_in_t.shape
    assert n == IN_COLS
    src_rows, scales = [], []
    for src, width, scale in _PACK_PIECES:
        assert width % PACK_BLOCK == 0
        for k in range(width // PACK_BLOCK):
            row = src + k * PACK_BLOCK
            assert row % ROPE_DIM == 0
            src_rows.append(row // ROPE_DIM)
            scales.append(1.0 if scale is None else scale)
    grid_spec = pltpu.PrefetchScalarGridSpec(
        num_scalar_prefetch=1,
        grid=(len(src_rows),),
        in_specs=[pl.BlockSpec(memory_space=pltpu.SMEM),
                  pl.BlockSpec((pl.Element(1), pl.Element(PACK_BLOCK), pl.Element(d)),
                               lambda j, src: (layer, src[j] * ROPE_DIM, 0))],
        out_specs=pl.BlockSpec((d, PACK_BLOCK), lambda j, src: (0, j)),
    )
    return pl.pallas_call(
        _pack_kernel, grid_spec=grid_spec,
        out_shape=jax.ShapeDtypeStruct((d, PACK_WIDTH), BF16),
        compiler_params=_cp(("arbitrary",)),
        name="pack_projection_weights",
    )(jnp.asarray(src_rows, jnp.int32), jnp.asarray(scales, F32), w_in_t)


def _pack_rope_kernel(wt_ref, o_ref):
    t = wt_ref[0]
    half = ROPE_DIM // 2
    both = jnp.concatenate([t, -t[half:], t[:half]], axis=0)
    o_ref[...] = both.T.astype(o_ref.dtype)


def pack_rope_weights(w_in_t, layer):
    d = w_in_t.shape[2]
    return pl.pallas_call(
        _pack_rope_kernel, grid=(1,),
        in_specs=[pl.BlockSpec((1, ROPE_DIM, d), lambda i: (layer, _O_KR // ROPE_DIM, 0))],
        out_specs=pl.BlockSpec((d, 2 * ROPE_DIM), lambda i: (0, 0)),
        out_shape=jax.ShapeDtypeStruct((d, 2 * ROPE_DIM), BF16),
        compiler_params=_cp(("arbitrary",)),
        name="pack_rope_weights",
    )(w_in_t)


def _prep_layer(w_in_t, layer, w_uq, w_ukv, w_o_mla, w_o_dil, w_o_sb, w_out):
    w_pack = pack_projection_weights(w_in_t, layer)
    w_kr = pack_rope_weights(w_in_t, layer)
    q_scale = MLA_QK ** -0.5 * LOG2E
    wq = (w_uq * q_scale).reshape(Q_LORA, MLA_HEADS, MLA_QK).transpose(1, 0, 2)
    wq_h = jnp.concatenate([wq, _rot_cols(wq[..., NOPE_DIM:])], axis=-1).astype(BF16)
    wkv_h = w_ukv.reshape(KV_LORA, MLA_HEADS, NOPE_DIM + MLA_V_DIM).transpose(1, 0, 2).astype(BF16)
    return dict(w_pack=w_pack, w_kr=w_kr, wq_h=wq_h, wkv_h=wkv_h,
                w_o_mla=w_o_mla.astype(BF16), w_o_dil=w_o_dil.astype(BF16), w_o_sb=w_o_sb.astype(BF16),
                w_out=w_out.astype(BF16))


def kernel(x, c, positions, w_ada, b_ada, g_mix, g_moe, w_in, g_q, w_uq, g_kv, w_ukv, w_o_mla, w_o_dil,
           w_o_sb, w_out, w_router, b_router, w_gate, w_up, w_down, g_final):
    b, s, d = x.shape
    depth = w_ada.shape[0]
    pos_f = positions.astype(F32)
    cos, sin = rope_tables(pos_f.reshape(b, s, 1))
    mod = ada_mod(c, w_ada, b_ada)
    w_router_t = w_router.T
    w_in_t = jnp.swapaxes(w_in, 1, 2)
    mods = [mod[l].reshape(b, 1, 6 * d) for l in range(depth)]
    a = norm_mod(x, g_mix[0], mods[0], sc_idx=1, sh_idx=0)
    for l in range(depth):
        p = _prep_layer(w_in_t, l, w_uq[l], w_ukv[l], w_o_mla[l], w_o_dil[l], w_o_sb[l], w_out[l])
        mod3 = mods[l]

        wp = p["w_pack"]
        lat = project(a, wp, *PACKED_COLS["lat"])
        kr = project(a, p["w_kr"], 0, 2 * ROPE_DIM)
        dil = [project_by_residue(a, wp, *PACKED_COLS[f"dil{g}"], DIL_RATES[g]) for g in range(DIL_GROUPS)]
        sbp = project(a, wp, *PACKED_COLS["sb"], tn=1024)
        gl = project(a, wp, *PACKED_COLS["gates"], tn=1024)

        q, k, v = mla_project(lat, kr, g_q[l], p["wq_h"], g_kv[l], p["wkv_h"], cos, sin)
        y_mla = mla_attention(q, k, v)
        dil_out = [dil_group_attention(dil[g], pos_f, g) for g in range(DIL_GROUPS)]
        y_sb = sb_attention(sbp)
        merged = merge_branches(y_mla, [o for o, _ in dil_out], [e for _, e in dil_out], y_sb, gl,
                                p["w_o_mla"], p["w_o_dil"], p["w_o_sb"])
        x, a2, logits_t = resid_project_norm_router(merged, p["w_out"], x, mod3, 2, g_moe[l], 4, 3, w_router_t)
        e0, e1, w0, w1 = router_top2(logits_t, b_router)
        moe_args = (a2, e0, e1, w0, w1, w_gate, w_up, w_down, l, x, mod3, 5)
        if l + 1 < depth:
            x, a = moe_sparse(*moe_args, g_mix[l + 1], next_mod3=mods[l + 1], sc_idx=1, sh_idx=0)
        else:
            out = moe_sparse(*moe_args, g_final)
    return out
```

```python
import functools
import math

import jax
import jax.numpy as jnp
from jax import lax
from jax.experimental import pallas as pl
from jax.experimental.pallas import tpu as pltpu

D_MODEL = 2048
EPS = 1e-6
MLA_HEADS = 8
Q_LORA = 512
KV_LORA = 512
NOPE_DIM = 128
ROPE_DIM = 64
MLA_V_DIM = 128
ROPE_THETA = 10000.0
MLA_QK = NOPE_DIM + ROPE_DIM
DIL_WINDOWS = (128, 512, 2048)
DIL_RATES = (1, 4, 16)
DIL_GROUPS = 3
DIL_HEADS = 4
DIL_HEAD_DIM = 128
DIL_SPAN = DIL_WINDOWS[0] // DIL_RATES[0]
DIL_GW = DIL_HEADS * DIL_HEAD_DIM
DIL_WIDTH = DIL_GROUPS * DIL_GW
ALIBI_MAX_BIAS = 8.0
SB_HEADS = 8
SB_HEAD_DIM = 128
SB_WIDTH = SB_HEADS * SB_HEAD_DIM
N_EXPERTS = 16
N_GROUPS = 4
EXPERTS_PER_GROUP = N_EXPERTS // N_GROUPS
D_EXPERT = 512

LOG2E = math.log2(math.e)
LN2 = math.log(2.0)
NEG_BIG = -1e30

V7X_LANES = 128
V7X_VMEM_BYTES = 64 * 1024 * 1024
VMEM_LIMIT = 56 * 1024 * 1024

F32 = jnp.float32
BF16 = jnp.bfloat16


def _cp(sem, vmem=VMEM_LIMIT):
    return pltpu.CompilerParams(dimension_semantics=sem, vmem_limit_bytes=vmem)


def _lane_tile(x, n):
    return x if n == 1 else jnp.concatenate([x] * n, axis=1)


def _nt_dot(a, b):
    return lax.dot_general(a, b, (((1,), (1,)), ((), ())), preferred_element_type=F32)


def _ada_kernel(c_ref, w_ref, b_ref, o_ref):
    c = c_ref[...]
    ca = c * (1.0 / (1.0 + jnp.exp(-c)))
    o_ref[0] = jnp.dot(ca, w_ref[0], preferred_element_type=F32,
                       precision=lax.Precision.HIGHEST) + b_ref[0]


def ada_mod(c, w_ada, b_ada, tn=1024):
    depth, d, n = w_ada.shape
    b = c.shape[0]
    return pl.pallas_call(
        _ada_kernel,
        grid=(depth, n // tn),
        in_specs=[pl.BlockSpec((b, d), lambda l, j: (0, 0)),
                  pl.BlockSpec((1, d, tn), lambda l, j: (l, 0, j)),
                  pl.BlockSpec((1, 1, tn), lambda l, j: (l, 0, j))],
        out_specs=pl.BlockSpec((1, b, tn), lambda l, j: (l, 0, j)),
        out_shape=jax.ShapeDtypeStruct((depth, b, n), F32),
        compiler_params=_cp(("arbitrary", "arbitrary")),
        name="ada_mod",
    )(c, w_ada, b_ada.reshape(depth, 1, n))


def _rope_table_kernel(pos_ref, inv_ref, cos_ref, sin_ref):
    ang = pos_ref[0] * inv_ref[...]
    cos_ref[0] = jnp.cos(ang)
    sin_ref[0] = jnp.sin(ang)


def rope_tables(pos_col, tm=512):
    b, s, _ = pos_col.shape
    tm = min(tm, s)
    half = ROPE_DIM // 2
    inv = ROPE_THETA ** (-jnp.arange(half, dtype=F32) / half)
    inv2 = jnp.concatenate([inv, inv]).reshape(1, ROPE_DIM)
    shp = jax.ShapeDtypeStruct((b, s, ROPE_DIM), F32)
    return pl.pallas_call(
        _rope_table_kernel,
        grid=(b, s // tm),
        in_specs=[pl.BlockSpec((1, tm, 1), lambda i, j: (i, j, 0)),
                  pl.BlockSpec((1, ROPE_DIM), lambda i, j: (0, 0))],
        out_specs=[pl.BlockSpec((1, tm, ROPE_DIM), lambda i, j: (i, j, 0))] * 2,
        out_shape=[shp, shp],
        compiler_params=_cp(("arbitrary", "arbitrary")),
        name="rope_tables",
    )(pos_col, inv2)


def _rmsnorm(x, g):
    return x * lax.rsqrt(jnp.mean(x * x, axis=-1, keepdims=True) + EPS) * g


def _norm_mod(x, g, scale, shift):
    return _rmsnorm(x, g) * (1.0 + scale) + shift


def _norm_kernel(x_ref, g_ref, sc_ref, sh_ref, o_ref):
    o_ref[0] = _norm_mod(x_ref[0], g_ref[...], sc_ref[0], sh_ref[0]).astype(o_ref.dtype)


def norm_mod(x, g, mod3, sc_idx, sh_idx, tm=512):
    b, s, d = x.shape
    tm = min(tm, s)
    return pl.pallas_call(
        _norm_kernel, grid=(b, s // tm),
        in_specs=[pl.BlockSpec((1, tm, d), lambda i, j: (i, j, 0)),
                  pl.BlockSpec((1, d), lambda i, j: (0, 0)),
                  pl.BlockSpec((1, 1, d), lambda i, j: (i, 0, sc_idx)),
                  pl.BlockSpec((1, 1, d), lambda i, j: (i, 0, sh_idx))],
        out_specs=pl.BlockSpec((1, tm, d), lambda i, j: (i, j, 0)),
        out_shape=jax.ShapeDtypeStruct((b, s, d), BF16),
        compiler_params=_cp(("arbitrary", "arbitrary")), name="norm_mod",
    )(x, g.reshape(1, d), mod3, mod3)


def _matmul_kernel(a_ref, w_ref, o_ref):
    o_ref[0] = jnp.dot(a_ref[0], w_ref[...], preferred_element_type=F32).astype(o_ref.dtype)


def project(a, w, col0, n, tm=1024, tn=512, out_dtype=BF16):
    b, s, k = a.shape
    tm = min(tm, s)
    tn = min(tn, n)
    assert n % tn == 0 and s % tm == 0 and col0 % tn == 0
    cb0 = col0 // tn
    return pl.pallas_call(
        _matmul_kernel,
        grid=(b, s // tm, n // tn),
        in_specs=[pl.BlockSpec((1, tm, k), lambda i, j, c: (i, j, 0)),
                  pl.BlockSpec((k, tn), lambda i, j, c: (0, cb0 + c))],
        out_specs=pl.BlockSpec((1, tm, tn), lambda i, j, c: (i, j, c)),
        out_shape=jax.ShapeDtypeStruct((b, s, n), out_dtype),
        compiler_params=_cp(("arbitrary", "arbitrary", "arbitrary")),
        name="project",
    )(a, w)


def _matmul_residue_kernel(a_ref, w_ref, o_ref, res_scr, *, rate):
    res = jnp.dot(a_ref[0], w_ref[...], preferred_element_type=F32)
    slabs, rows, lanes = res_scr.shape
    for c in range(slabs):
        res_scr[c] = res[:, c * lanes:(c + 1) * lanes]
    n = rows // rate
    for rho in range(rate):
        for c in range(slabs):
            o_ref[0, rho, :, c * lanes:(c + 1) * lanes] = (
                res_scr[c, pl.ds(rho, n, stride=rate), :].astype(o_ref.dtype))


def project_by_residue(a, w, col0, n, rate, tm=1024, tn=512):
    b, s, k = a.shape
    if rate == 1:
        return project(a, w, col0, n, tm=tm, tn=tn).reshape(b, 1, s, n)
    tm = min(tm, s)
    assert n % tn == 0 and s % tm == 0 and tm % (rate * 16) == 0 and col0 % tn == 0
    cb0 = col0 // tn
    return pl.pallas_call(
        functools.partial(_matmul_residue_kernel, rate=rate),
        grid=(b, s // tm, n // tn),
        in_specs=[pl.BlockSpec((1, tm, k), lambda i, j, c: (i, j, 0)),
                  pl.BlockSpec((k, tn), lambda i, j, c: (0, cb0 + c))],
        out_specs=pl.BlockSpec((1, rate, tm // rate, tn), lambda i, j, c: (i, 0, j, c)),
        out_shape=jax.ShapeDtypeStruct((b, rate, s // rate, n), BF16),
        scratch_shapes=[pltpu.VMEM((tn // V7X_LANES, tm, V7X_LANES), F32)],
        compiler_params=_cp(("arbitrary", "arbitrary", "arbitrary")),
        name=f"project_by_residue_{rate}",
    )(a, w)


def _latent_norm(c, g):
    c = c.astype(F32)
    return (c * lax.rsqrt(jnp.mean(c * c, axis=-1, keepdims=True) + EPS) * g).astype(BF16)


def _mla_proj_kernel(cq_ref, ckv_ref, kr_ref, gq_ref, gkv_ref, wq_ref, wkv_ref, cos_ref, sin_ref,
                     q_ref, k_ref, v_ref):
    cos, sin = cos_ref[0], sin_ref[0]
    rq = _latent_norm(cq_ref[0], gq_ref[...])
    rkv = _latent_norm(ckv_ref[0], gkv_ref[...])
    kr = kr_ref[0].astype(F32)
    k_rope = (kr[:, :ROPE_DIM] * cos + kr[:, ROPE_DIM:] * sin).astype(k_ref.dtype)
    ones = jnp.ones((rq.shape[0], MLA_V_DIM), v_ref.dtype)
    for h in range(MLA_HEADS):
        r = jnp.dot(rq, wq_ref[h], preferred_element_type=F32)
        q_ref[0, h, :, :NOPE_DIM] = r[:, :NOPE_DIM].astype(q_ref.dtype)
        roped = r[:, NOPE_DIM:NOPE_DIM + ROPE_DIM] * cos + r[:, NOPE_DIM + ROPE_DIM:] * sin
        q_ref[0, h, :, NOPE_DIM:] = roped.astype(q_ref.dtype)
        r = jnp.dot(rkv, wkv_ref[h], preferred_element_type=F32)
        k_ref[0, h, :, :NOPE_DIM] = r[:, :NOPE_DIM].astype(k_ref.dtype)
        k_ref[0, h, :, NOPE_DIM:] = k_rope
        v_ref[0, h, :, :MLA_V_DIM] = r[:, NOPE_DIM:].astype(v_ref.dtype)
        v_ref[0, h, :, MLA_V_DIM:] = ones


def mla_project(lat, kr, g_q, wq_h, g_kv, wkv_h, cos, sin, tm=512):
    b, s, _ = lat.shape
    tm = min(tm, s)
    h = MLA_HEADS
    tab = pl.BlockSpec((1, tm, ROPE_DIM), lambda i, j: (i, j, 0))

    def const(shape):
        return pl.BlockSpec(shape, lambda i, j: (0,) * len(shape))

    def head_major(width):
        return pl.BlockSpec((1, h, tm, width), lambda i, j: (i, 0, j, 0))

    return pl.pallas_call(
        _mla_proj_kernel, grid=(b, s // tm),
        in_specs=[pl.BlockSpec((1, tm, Q_LORA), lambda i, j: (i, j, 0)),
                  pl.BlockSpec((1, tm, KV_LORA), lambda i, j: (i, j, 1)),
                  pl.BlockSpec((1, tm, 2 * ROPE_DIM), lambda i, j: (i, j, 0)),
                  const((1, Q_LORA)), const((1, KV_LORA)),
                  const((h, Q_LORA, 256)), const((h, KV_LORA, 256)), tab, tab],
        out_specs=[head_major(MLA_QK), head_major(MLA_QK), head_major(2 * MLA_V_DIM)],
        out_shape=[jax.ShapeDtypeStruct((b, h, s, MLA_QK), BF16),
                   jax.ShapeDtypeStruct((b, h, s, MLA_QK), BF16),
                   jax.ShapeDtypeStruct((b, h, s, 2 * MLA_V_DIM), BF16)],
        compiler_params=_cp(("arbitrary", "arbitrary")), name="mla_project",
    )(lat, lat, kr, g_q.reshape(1, Q_LORA), g_kv.reshape(1, KV_LORA), wq_h, wkv_h, cos, sin)


def _mla_attn_kernel(q_ref, k_ref, v_ref, o_ref, s0_scr, s1_scr, m_scr, acc_scr, *, tq, tk):
    qi = pl.program_id(2)
    m_scr[...] = jnp.full(m_scr.shape, NEG_BIG, F32)
    acc_scr[...] = jnp.zeros(acc_scr.shape, F32)

    def scores(c, dst, row0=0):
        start = pl.multiple_of(c * tk, tk)
        dst[row0:, :] = _nt_dot(q_ref[0, 0, row0:, :], k_ref[0, 0, pl.ds(start, tk), :])

    def absorb(src, c, col_off=None, row0=0):
        start = pl.multiple_of(c * tk, tk)
        s = src[row0:, :]
        n = tq - row0
        if col_off is not None:
            row = row0 + lax.broadcasted_iota(jnp.int32, (n, tk), 0)
            col = col_off + lax.broadcasted_iota(jnp.int32, (n, tk), 1)
            s = jnp.where(col <= row, s, NEG_BIG)
        m_prev = m_scr[row0:, :]
        m_new = jnp.maximum(m_prev, jnp.max(s, axis=-1, keepdims=True))
        p = jnp.exp2(s - _lane_tile(m_new, tk // V7X_LANES))
        alpha = jnp.exp2(m_prev - m_new)
        pv = jnp.dot(p.astype(BF16), v_ref[0, 0, pl.ds(start, tk), :], preferred_element_type=F32)
        acc_scr[row0:, :] = _lane_tile(alpha, 2) * acc_scr[row0:, :] + pv
        m_scr[row0:, :] = m_new

    scores(0, s0_scr)

    def body(i, carry):
        scores(2 * i + 1, s1_scr)
        absorb(s0_scr, 2 * i)
        scores(2 * i + 2, s0_scr)
        absorb(s1_scr, 2 * i + 1)
        return carry

    lax.fori_loop(0, qi, body, 0)
    scores(2 * qi + 1, s1_scr, row0=tk)
    absorb(s0_scr, 2 * qi, col_off=0)
    absorb(s1_scr, 2 * qi + 1, col_off=tk, row0=tk)
    acc = acc_scr[...]
    o_ref[0] = (acc[:, :MLA_V_DIM] / acc[:, MLA_V_DIM:]).astype(o_ref.dtype)


def mla_attention(q, k, v, tq=1024):
    b, h, s, _ = q.shape
    tq = min(tq, s)
    tk = tq // 2
    return pl.pallas_call(
        functools.partial(_mla_attn_kernel, tq=tq, tk=tk),
        grid=(b, h, s // tq),
        in_specs=[pl.BlockSpec((1, 1, tq, MLA_QK), lambda i, j, t: (i, j, t, 0)),
                  pl.BlockSpec((1, 1, s, MLA_QK), lambda i, j, t: (i, j, 0, 0)),
                  pl.BlockSpec((1, 1, s, 2 * MLA_V_DIM), lambda i, j, t: (i, j, 0, 0))],
        out_specs=pl.BlockSpec((1, tq, MLA_V_DIM), lambda i, j, t: (i, t, j)),
        out_shape=jax.ShapeDtypeStruct((b, s, h * MLA_V_DIM), BF16),
        scratch_shapes=[pltpu.VMEM((tq, tk), F32), pltpu.VMEM((tq, tk), F32),
                        pltpu.VMEM((tq, V7X_LANES), F32), pltpu.VMEM((tq, 2 * MLA_V_DIM), F32)],
        compiler_params=_cp(("arbitrary", "arbitrary", "arbitrary")),
        name="mla_attention",
    )(q, k, v)


SB_SKIP_LOG2 = 150.0


def _sb_attn_kernel(q_ref, k_ref, v_ref, tri_ref, o_ref, c_scr, acc_scr, *, tq, hp):
    qi = pl.program_id(2)
    c_scr[...] = jnp.zeros(c_scr.shape, F32)
    acc_scr[...] = jnp.zeros(acc_scr.shape, F32)
    reps = tq // V7X_LANES

    def chunk(j, diagonal):
        start = pl.multiple_of(j * tq, tq)
        heads = range(hp)
        cols = [slice(hh * SB_HEAD_DIM, (hh + 1) * SB_HEAD_DIM) for hh in heads]
        if diagonal:
            before = (lax.broadcasted_iota(jnp.int32, (tq, tq), 1)
                      < lax.broadcasted_iota(jnp.int32, (tq, tq), 0))
        zn = [_nt_dot(q_ref[0, :, cols[hh]], k_ref[0, pl.ds(start, tq), cols[hh]]) for hh in heads]
        lk = [jnp.minimum(z, 0.0) - jnp.log2(1.0 + jnp.exp2(-jnp.abs(z))) for z in zn]
        if diagonal:
            lk = [jnp.where(before, x, 0.0) for x in lk]
        hi = [x.astype(BF16) for x in lk]
        lo = [(x - h.astype(F32)).astype(BF16) for x, h in zip(lk, hi)]
        suffix = [jnp.dot(jnp.concatenate([h, l], axis=1), tri_ref[...], preferred_element_type=F32)
                  for h, l in zip(hi, lo)]
        c = [c_scr[hh] for hh in heads]
        a = [jnp.exp2(sf - z + _lane_tile(cc, reps)) for sf, z, cc in zip(suffix, zn, c)]
        if diagonal:
            a = [jnp.where(before, x, 0.0) for x in a]
        for hh in heads:
            acc_scr[:, cols[hh]] += jnp.dot(a[hh].astype(BF16), v_ref[0, pl.ds(start, tq), cols[hh]],
                                            preferred_element_type=F32)
            c_scr[hh] = c[hh] + jnp.sum(lk[hh], axis=-1, keepdims=True)

    chunk(qi, True)

    def cond(state):
        j, go = state
        return jnp.logical_and(j >= 0, go > 0)

    def body(state):
        j, _ = state
        chunk(j, False)
        go = (jnp.max(c_scr[...]) > -SB_SKIP_LOG2).astype(jnp.int32)
        return j - 1, go

    lax.while_loop(cond, body, (qi - 1, jnp.int32(1)))
    o_ref[0] = acc_scr[...].astype(o_ref.dtype)


def sb_attention(qkv, tq=256, hp=4):
    b, s, _ = qkv.shape
    tq = min(tq, s)
    hw = hp * SB_HEAD_DIM
    nhb = SB_HEADS // hp
    tri = (lax.broadcasted_iota(jnp.int32, (tq, tq), 0)
           >= lax.broadcasted_iota(jnp.int32, (tq, tq), 1)).astype(BF16)
    tri2 = jnp.concatenate([tri, tri], axis=0)
    return pl.pallas_call(
        functools.partial(_sb_attn_kernel, tq=tq, hp=hp),
        grid=(b, nhb, s // tq),
        in_specs=[pl.BlockSpec((1, tq, hw), lambda i, j, t: (i, t, j)),
                  pl.BlockSpec((1, s, hw), lambda i, j, t: (i, 0, nhb + j)),
                  pl.BlockSpec((1, s, hw), lambda i, j, t: (i, 0, 2 * nhb + j)),
                  pl.BlockSpec((2 * tq, tq), lambda i, j, t: (0, 0))],
        out_specs=pl.BlockSpec((1, tq, hw), lambda i, j, t: (i, t, j)),
        out_shape=jax.ShapeDtypeStruct((b, s, SB_WIDTH), BF16),
        scratch_shapes=[pltpu.VMEM((hp, tq, V7X_LANES), F32), pltpu.VMEM((tq, hw), F32)],
        compiler_params=_cp(("arbitrary", "arbitrary", "arbitrary")),
        name="sb_attention",
    )(qkv, qkv, qkv, tri2)


def _dil_kernel(q_ref, kc_ref, vc_ref, kp_ref, vp_ref, pq_ref, pkc_ref, pkp_ref, sl_ref,
                o_ref, lse_ref, *, tq):
    ti = pl.program_id(2)
    w = DIL_SPAN
    a_idx = lax.broadcasted_iota(jnp.int32, (w, 2 * w), 0)
    c_idx = lax.broadcasted_iota(jnp.int32, (w, 2 * w), 1)
    window = jnp.logical_and(c_idx >= a_idx, c_idx <= a_idx + w)
    first_window = jnp.logical_and(window, jnp.logical_or(c_idx >= w, ti > 0))
    for sb in range(tq // w):
        rows = slice(sb * w, (sb + 1) * w)
        both = slice((sb - 1) * w, (sb + 1) * w)
        ok = first_window if sb == 0 else window
        pq = pq_ref[0, 0, rows, :]
        if sb == 0:
            pk = jnp.concatenate([pkp_ref[0, 0], pkc_ref[0, 0, :, rows]], axis=1)
        else:
            pk = pkc_ref[0, 0, :, both]
        dist = jnp.abs(pq - pk)
        heads = range(DIL_HEADS)
        cols = [slice(h * DIL_HEAD_DIM, (h + 1) * DIL_HEAD_DIM) for h in heads]

        def keys(cur_ref, prev_ref, h):
            if sb == 0:
                return jnp.concatenate([prev_ref[0, :, cols[h]], cur_ref[0, rows, cols[h]]], axis=0)
            return cur_ref[0, both, cols[h]]

        z = [_nt_dot(q_ref[0, rows, cols[h]], keys(kc_ref, kp_ref, h)) for h in heads]
        s = [jnp.where(ok, z[h] - sl_ref[h] * dist, NEG_BIG) for h in heads]
        m = [jnp.max(s[h], axis=-1, keepdims=True) for h in heads]
        p = [jnp.exp2(s[h] - m[h]) for h in heads]
        den = [jnp.sum(p[h], axis=-1, keepdims=True) for h in heads]
        for h in heads:
            o = jnp.dot(p[h].astype(BF16), keys(vc_ref, vp_ref, h), preferred_element_type=F32) / den[h]
            o_ref[0, rows, cols[h]] = o
            lse = (m[h] + jnp.log2(den[h])) * LN2
            lse_ref[0, rows, cols[h]] = jnp.broadcast_to(lse, (w, DIL_HEAD_DIM))


def dil_group_attention(dil_g, pos_f, g, tq=512):
    b, r, sr, c = dil_g.shape
    assert r == DIL_RATES[g]
    tq = min(tq, sr)
    w = DIL_SPAN
    assert sr % tq == 0 and tq % w == 0
    nsub = tq // w
    view = dil_g.reshape(b * r, sr, c)
    pos_r = pos_f.reshape(b, sr, r).transpose(0, 2, 1)
    pos_q = pos_r[..., None]
    pos_k = pos_r[:, :, None, :]
    n_h = DIL_GROUPS * DIL_HEADS
    slopes = 2.0 ** (-ALIBI_MAX_BIAS * jnp.arange(1, n_h + 1, dtype=F32) / n_h)
    slopes2 = (slopes * LOG2E)[g * DIL_HEADS:(g + 1) * DIL_HEADS]

    def cur(part):
        return pl.BlockSpec((1, tq, DIL_GW), lambda i, p, t: (i * r + p, t, part))

    def prev(part):
        return pl.BlockSpec((1, w, DIL_GW), lambda i, p, t: (i * r + p, jnp.maximum(t * nsub - 1, 0), part))

    out_spec = pl.BlockSpec((1, tq, DIL_GW), lambda i, p, t: (i * r + p, t, 0))
    out_shape = jax.ShapeDtypeStruct((b * r, sr, DIL_GW), F32)
    o, lse = pl.pallas_call(
        functools.partial(_dil_kernel, tq=tq),
        grid=(b, r, sr // tq),
        in_specs=[cur(0), cur(1), cur(2), prev(1), prev(2),
                  pl.BlockSpec((1, 1, tq, 1), lambda i, p, t: (i, p, t, 0)),
                  pl.BlockSpec((1, 1, 1, tq), lambda i, p, t: (i, p, 0, t)),
                  pl.BlockSpec((1, 1, 1, w), lambda i, p, t: (i, p, 0, jnp.maximum(t * nsub - 1, 0))),
                  pl.BlockSpec(memory_space=pltpu.SMEM)],
        out_specs=[out_spec, out_spec],
        out_shape=[out_shape, out_shape],
        compiler_params=_cp(("arbitrary", "arbitrary", "arbitrary")),
        name=f"dil_attention_g{g}",
    )(view, view, view, view, view, pos_q, pos_k, pos_k, slopes2)
    return o.reshape(b, r, sr, DIL_GW), lse.reshape(b, r, sr, DIL_GW)


def _token_order(ref, scr):
    _, r, n, width = ref.shape
    if r == 1:
        return ref[0, 0]
    slabs = width // V7X_LANES
    for rho in range(r):
        for c in range(slabs):
            scr[c, pl.ds(rho, n, stride=r), :] = ref[0, rho, :, c * V7X_LANES:(c + 1) * V7X_LANES]
    return jnp.concatenate([scr[c] for c in range(slabs)], axis=1)


def _merge_kernel(ym_ref, o0_ref, o1_ref, o2_ref, l0_ref, l1_ref, l2_ref, ys_ref,
                  g0_ref, g1_ref, g2_ref, wm_ref, wd_ref, ws_ref, out_ref, *scr):
    o0, o1, o2 = _token_order(o0_ref, scr[0]), _token_order(o1_ref, scr[1]), _token_order(o2_ref, scr[2])
    l0, l1, l2 = _token_order(l0_ref, scr[3]), _token_order(l1_ref, scr[4]), _token_order(l2_ref, scr[5])
    mx = jnp.maximum(jnp.maximum(l0, l1), l2)
    e0, e1, e2 = jnp.exp(l0 - mx), jnp.exp(l1 - mx), jnp.exp(l2 - mx)
    y_dil = (e0 * o0 + e1 * o1 + e2 * o2) / (e0 + e1 + e2)

    def sig(ref):
        return 1.0 / (1.0 + jnp.exp(-ref[0].astype(F32)))

    merged = (sig(g0_ref) * jnp.dot(ym_ref[0], wm_ref[...], preferred_element_type=F32)
              + sig(g1_ref) * jnp.dot(y_dil.astype(BF16), wd_ref[...], preferred_element_type=F32)
              + sig(g2_ref) * jnp.dot(ys_ref[0], ws_ref[...], preferred_element_type=F32))
    out_ref[0] = merged.astype(out_ref.dtype)


def merge_branches(y_mla, dil_o, dil_lse, y_sb, gates, w_o_mla, w_o_dil, w_o_sb, tm=256):
    b, s, _ = y_mla.shape
    d = D_MODEL
    tm = min(tm, s)

    def row(width, cb=0):
        return pl.BlockSpec((1, tm, width), lambda i, j: (i, j, cb))

    def by_residue(r):
        return pl.BlockSpec((1, r, tm // r, DIL_GW), lambda i, j: (i, 0, j, 0))

    def full(w):
        return pl.BlockSpec(w.shape, lambda i, j: (0, 0))

    dil_specs = [by_residue(r) for r in DIL_RATES]
    return pl.pallas_call(
        _merge_kernel, grid=(b, s // tm),
        in_specs=[row(y_mla.shape[2])] + dil_specs + dil_specs + [row(SB_WIDTH),
                  row(d, 0), row(d, 1), row(d, 2), full(w_o_mla), full(w_o_dil), full(w_o_sb)],
        out_specs=row(d),
        out_shape=jax.ShapeDtypeStruct((b, s, d), BF16),
        scratch_shapes=[pltpu.VMEM((DIL_GW // V7X_LANES, tm, V7X_LANES), F32)] * (2 * DIL_GROUPS),
        compiler_params=_cp(("arbitrary", "arbitrary")),
        name="merge_branches",
    )(y_mla, *dil_o, *dil_lse, y_sb, gates, gates, gates, w_o_mla, w_o_dil, w_o_sb)


def _resid_norm_router_kernel(a_ref, w_ref, x_ref, ga_ref, g_ref, sc_ref, sh_ref, wr_ref,
                              xo_ref, a2_ref, lg_ref):
    x_new = x_ref[0] + ga_ref[0] * jnp.dot(a_ref[0], w_ref[...], preferred_element_type=F32)
    xo_ref[0] = x_new
    a2 = _norm_mod(x_new, g_ref[...], sc_ref[0], sh_ref[0])
    a2_ref[0] = a2
    lg_ref[0] = lax.dot_general(wr_ref[...], a2, (((1,), (1,)), ((), ())),
                                preferred_element_type=F32, precision=lax.Precision.HIGHEST)


def resid_project_norm_router(a, w, x, mod3, gate_idx, g, sc_idx, sh_idx, w_router_t, tm=512):
    b, s, d = x.shape
    k = a.shape[2]
    e = w_router_t.shape[0]
    tm = min(tm, s)
    row = pl.BlockSpec((1, tm, d), lambda i, j: (i, j, 0))

    def chunk(idx):
        return pl.BlockSpec((1, 1, d), lambda i, j: (i, 0, idx))

    big = jax.ShapeDtypeStruct((b, s, d), F32)
    return pl.pallas_call(
        _resid_norm_router_kernel, grid=(b, s // tm),
        in_specs=[pl.BlockSpec((1, tm, k), lambda i, j: (i, j, 0)),
                  pl.BlockSpec((k, d), lambda i, j: (0, 0)),
                  row, chunk(gate_idx),
                  pl.BlockSpec((1, d), lambda i, j: (0, 0)), chunk(sc_idx), chunk(sh_idx),
                  pl.BlockSpec((e, d), lambda i, j: (0, 0))],
        out_specs=[row, row, pl.BlockSpec((1, e, tm), lambda i, j: (i, 0, j))],
        out_shape=[big, big, jax.ShapeDtypeStruct((b, e, s), F32)],
        compiler_params=_cp(("arbitrary", "arbitrary")),
        name="resid_project_norm_router",
    )(a, w, x, mod3, g.reshape(1, d), mod3, mod3, w_router_t)


def _router_kernel(lg_ref, b_ref, e0_ref, e1_ref, w0_ref, w1_ref):
    lg = lg_ref[0]
    tm = lg.shape[1]
    sc = [1.0 / (1.0 + jnp.exp(-lg[e:e + 1, :])) for e in range(N_EXPERTS)]
    bi = [sc[e] + b_ref[e] for e in range(N_EXPERTS)]
    n = EXPERTS_PER_GROUP
    gscore = []
    for g in range(N_GROUPS):
        v = bi[g * n:(g + 1) * n]
        pair_max = None
        for a in range(n):
            for c in range(a + 1, n):
                pm = v[a] + v[c]
                pair_max = pm if pair_max is None else jnp.maximum(pair_max, pm)
        gscore.append(pair_max)
    best = gscore[0]
    gsel = jnp.zeros((1, tm), jnp.int32)
    for g in range(1, N_GROUPS):
        better = gscore[g] > best
        best = jnp.where(better, gscore[g], best)
        gsel = jnp.where(better, g, gsel)
    gb = [sum(jnp.where(gsel == g, bi[g * n + i], 0.0) for g in range(N_GROUPS)) for i in range(n)]
    gs = [sum(jnp.where(gsel == g, sc[g * n + i], 0.0) for g in range(N_GROUPS)) for i in range(n)]
    sel = []
    for i in range(n):
        beaten = jnp.zeros((1, tm), jnp.int32)
        for j in range(n):
            if j == i:
                continue
            wins = (gb[j] > gb[i]) if j > i else (gb[j] >= gb[i])
            beaten = beaten + wins.astype(jnp.int32)
        sel.append(beaten < 2)
    den = sum(jnp.where(sel[i], gs[i], 0.0) for i in range(n))
    i0 = jnp.where(sel[0], 0, jnp.where(sel[1], 1, 2))
    i1 = jnp.where(sel[3], 3, jnp.where(sel[2], 2, 1))
    e0_ref[0] = gsel * n + i0
    e1_ref[0] = gsel * n + i1
    w0_ref[0] = sum(jnp.where(i0 == i, gs[i], 0.0) for i in range(n)) / den
    w1_ref[0] = sum(jnp.where(i1 == i, gs[i], 0.0) for i in range(n)) / den


def router_top2(logits_t, b_router, tm=1024):
    b, e, s = logits_t.shape
    tm = min(tm, s)
    row = pl.BlockSpec((1, 1, tm), lambda i, j: (i, 0, j))
    ishape = jax.ShapeDtypeStruct((b, 1, s), jnp.int32)
    fshape = jax.ShapeDtypeStruct((b, 1, s), F32)
    return pl.pallas_call(
        _router_kernel, grid=(b, s // tm),
        in_specs=[pl.BlockSpec((1, e, tm), lambda i, j: (i, 0, j)),
                  pl.BlockSpec(memory_space=pltpu.SMEM)],
        out_specs=[row, row, row, row],
        out_shape=[ishape, ishape, fshape, fshape],
        compiler_params=_cp(("arbitrary", "arbitrary")),
        name="router_top2",
    )(logits_t, b_router)


def _rank_kernel(e0_ref, e1_ref, tri_ref, r0_ref, r1_ref, cnt_ref, carry_scr):
    @pl.when(jnp.logical_and(pl.program_id(0) == 0, pl.program_id(1) == 0))
    def _():
        carry_scr[...] = jnp.zeros(carry_scr.shape, F32)

    e0, e1 = e0_ref[0], e1_ref[0]
    tb = e0.shape[1]
    eid = lax.broadcasted_iota(jnp.int32, (N_EXPERTS, tb), 0)
    hit0 = eid == e0
    hit1 = eid == e1
    used = jnp.where(jnp.logical_or(hit0, hit1), 1.0, 0.0)
    before = jnp.dot(used.astype(BF16), tri_ref[...], preferred_element_type=F32)
    rank = before + _lane_tile(carry_scr[...], tb // V7X_LANES)
    r0_ref[0] = jnp.sum(jnp.where(hit0, rank, 0.0), axis=0, keepdims=True).astype(jnp.int32)
    r1_ref[0] = jnp.sum(jnp.where(hit1, rank, 0.0), axis=0, keepdims=True).astype(jnp.int32)
    carry_scr[...] = carry_scr[...] + jnp.sum(used, axis=1, keepdims=True)
    cnt_ref[...] = carry_scr[...]


def expert_ranks(e0, e1, tb=512):
    b, _, s = e0.shape
    tb = min(tb, s)
    tri = (lax.broadcasted_iota(jnp.int32, (tb, tb), 0)
           < lax.broadcasted_iota(jnp.int32, (tb, tb), 1)).astype(BF16)
    row = pl.BlockSpec((1, 1, tb), lambda i, j: (i, 0, j))
    ishape = jax.ShapeDtypeStruct((b, 1, s), jnp.int32)
    return pl.pallas_call(
        _rank_kernel, grid=(b, s // tb),
        in_specs=[row, row, pl.BlockSpec((tb, tb), lambda i, j: (0, 0))],
        out_specs=[row, row, pl.BlockSpec((N_EXPERTS, V7X_LANES), lambda i, j: (0, 0))],
        out_shape=[ishape, ishape, jax.ShapeDtypeStruct((N_EXPERTS, V7X_LANES), F32)],
        scratch_shapes=[pltpu.VMEM((N_EXPERTS, V7X_LANES), F32)],
        compiler_params=_cp(("arbitrary", "arbitrary")),
        name="expert_ranks",
    )(e0, e1, tri)


MOE_TM = 256


def _row_copy(src_hbm, row, dst, r, sem):
    return pltpu.make_async_copy(src_hbm.at[pl.ds(row, 1), :], dst.at[pl.ds(r, 1), :], sem)


def _row_gather(idx_ref, base, src_hbm, dst, sem, n):
    def issue(r, carry):
        _row_copy(src_hbm, idx_ref[base + r], dst, r, sem).start()
        return carry

    lax.fori_loop(0, n, issue, 0, unroll=8)


def _wait_rows(buf, sem):
    pltpu.make_async_copy(buf, buf, sem).wait()


def _moe_group_kernel(te_ref, nu_ref, src_ref, x_hbm, wg_ref, wu_ref, wd_ref, y_ref,
                      xbuf, wg_bf, wu_bf, wd_bf, sem):
    i = pl.program_id(0)
    n_used = nu_ref[0]
    tm = xbuf.shape[1]

    @pl.when(i == 0)
    def _():
        _row_gather(src_ref, 0, x_hbm, xbuf.at[0], sem.at[0], tm)

    @pl.when(jnp.logical_or(i == 0, te_ref[i] != te_ref[jnp.maximum(i - 1, 0)]))
    def _():
        wg_bf[...] = wg_ref[0, 0].astype(BF16)
        wu_bf[...] = wu_ref[0, 0].astype(BF16)
        wd_bf[...] = wd_ref[0, 0].astype(BF16)

    @pl.when(i < n_used)
    def _():
        slot = i % 2
        nxt = 1 - slot
        _wait_rows(xbuf.at[slot], sem.at[slot])
        base = jnp.minimum(i + 1, n_used - 1) * tm
        for r in range(tm):
            _row_copy(x_hbm, src_ref[base + r], xbuf.at[nxt], r, sem.at[nxt]).start(priority=r % 2)
        xb = xbuf[slot].astype(BF16)
        hg = jnp.dot(xb, wg_bf[...], preferred_element_type=F32)
        hu = jnp.dot(xb, wu_bf[...], preferred_element_type=F32)
        h = (hg / (1.0 + jnp.exp(-hg))) * hu
        y_ref[...] = jnp.dot(h.astype(BF16), wd_bf[...], preferred_element_type=F32)

    @pl.when(i == n_used - 1)
    def _():
        _wait_rows(xbuf.at[(i + 1) % 2], sem.at[(i + 1) % 2])

    @pl.when(i >= n_used)
    def _():
        y_ref[...] = jnp.zeros(y_ref.shape, y_ref.dtype)


def moe_grouped_ffn(x_rows, src, tile_expert, n_used, w_gate, w_up, w_down, layer):
    d = D_MODEL
    p = src.shape[0]
    tm = MOE_TM
    grid_spec = pltpu.PrefetchScalarGridSpec(
        num_scalar_prefetch=3,
        grid=(p // tm,),
        in_specs=[pl.BlockSpec(memory_space=pl.ANY),
                  pl.BlockSpec((1, 1, d, D_EXPERT), lambda i, te, nu, sr: (layer, te[i], 0, 0)),
                  pl.BlockSpec((1, 1, d, D_EXPERT), lambda i, te, nu, sr: (layer, te[i], 0, 0)),
                  pl.BlockSpec((1, 1, D_EXPERT, d), lambda i, te, nu, sr: (layer, te[i], 0, 0))],
        out_specs=pl.BlockSpec((tm, d), lambda i, te, nu, sr: (i, 0)),
        scratch_shapes=[pltpu.VMEM((2, tm, d), F32),
                        pltpu.VMEM((d, D_EXPERT), BF16), pltpu.VMEM((d, D_EXPERT), BF16),
                        pltpu.VMEM((D_EXPERT, d), BF16), pltpu.SemaphoreType.DMA((2,))],
    )
    return pl.pallas_call(
        _moe_group_kernel, grid_spec=grid_spec,
        out_shape=jax.ShapeDtypeStruct((p, d), F32),
        compiler_params=_cp(("arbitrary",)),
        name="moe_grouped_ffn",
    )(tile_expert, n_used, src, x_rows, w_gate, w_up, w_down)


def _moe_combine_kernel(p0_ref, p1_ref, y_hbm, w0_ref, w1_ref, x_ref, gm_ref, g_ref, *rest, final):
    if final:
        o_ref, buf0, buf1, sem = rest
    else:
        sc_ref, sh_ref, xo_ref, a_ref, buf0, buf1, sem = rest
    i = pl.program_id(0)
    tm = buf0.shape[1]

    def gather(tile, slot):
        base = tile * tm

        def issue(r, carry):
            _row_copy(y_hbm, p0_ref[base + r], buf0.at[slot], r, sem.at[0, slot]).start(priority=0)
            _row_copy(y_hbm, p1_ref[base + r], buf1.at[slot], r, sem.at[1, slot]).start(priority=1)
            return carry

        lax.fori_loop(0, tm, issue, 0, unroll=8)

    @pl.when(i == 0)
    def _():
        gather(0, 0)

    @pl.when(i + 1 < pl.num_programs(0))
    def _():
        gather(i + 1, (i + 1) % 2)

    slot = i % 2
    _wait_rows(buf0.at[slot], sem.at[0, slot])
    _wait_rows(buf1.at[slot], sem.at[1, slot])
    moe = w0_ref[...] * buf0[slot] + w1_ref[...] * buf1[slot]
    x_new = x_ref[...] + gm_ref[0] * moe
    if final:
        o_ref[...] = _rmsnorm(x_new, g_ref[...])
    else:
        xo_ref[...] = x_new
        a_ref[...] = _norm_mod(x_new, g_ref[...], sc_ref[0], sh_ref[0]).astype(a_ref.dtype)


def moe_combine(y, pos0, pos1, w0, w1, x, mod3, gate_idx, g, next_mod3=None, sc_idx=None, sh_idx=None, tm=256):
    b, s, d = x.shape
    t = b * s
    tm = min(tm, s)
    per_batch = s // tm
    final = next_mod3 is None

    def chunk(idx):
        return pl.BlockSpec((1, 1, d), lambda i, p0, p1: (i // per_batch, 0, idx))

    row = pl.BlockSpec((tm, d), lambda i, p0, p1: (i, 0))
    in_specs = [pl.BlockSpec(memory_space=pl.ANY),
                pl.BlockSpec((tm, 1), lambda i, p0, p1: (i, 0)),
                pl.BlockSpec((tm, 1), lambda i, p0, p1: (i, 0)),
                row, chunk(gate_idx), pl.BlockSpec((1, d), lambda i, p0, p1: (0, 0))]
    operands = [y, w0.reshape(t, 1), w1.reshape(t, 1), x.reshape(t, d), mod3, g.reshape(1, d)]
    if final:
        out_specs, out_shape = row, jax.ShapeDtypeStruct((t, d), F32)
    else:
        in_specs += [chunk(sc_idx), chunk(sh_idx)]
        operands += [next_mod3, next_mod3]
        out_specs = [row, row]
        out_shape = [jax.ShapeDtypeStruct((t, d), F32), jax.ShapeDtypeStruct((t, d), BF16)]
    grid_spec = pltpu.PrefetchScalarGridSpec(
        num_scalar_prefetch=2, grid=(t // tm,), in_specs=in_specs, out_specs=out_specs,
        scratch_shapes=[pltpu.VMEM((2, tm, d), F32), pltpu.VMEM((2, tm, d), F32),
                        pltpu.SemaphoreType.DMA((2, 2))],
    )
    out = pl.pallas_call(
        functools.partial(_moe_combine_kernel, final=final), grid_spec=grid_spec, out_shape=out_shape,
        compiler_params=_cp(("arbitrary",)),
        name="moe_combine_final" if final else "moe_combine",
    )(pos0, pos1, *operands)
    if final:
        return out.reshape(b, s, d)
    return out[0].reshape(b, s, d), out[1].reshape(b, s, d)


def moe_sparse(a2, e0, e1, w0, w1, w_gate, w_up, w_down, layer, x, mod3, gate_idx, g, **next_norm):
    b, s, d = x.shape
    t = b * s
    tm = MOE_TM
    n_tiles = (2 * t + N_EXPERTS * (tm - 1)) // tm
    r0, r1, cnt = expert_ranks(e0, e1)
    counts = cnt[:, 0].astype(jnp.int32)
    padded = (counts + tm - 1) // tm * tm
    ends = jnp.cumsum(padded)
    offs = ends - padded
    e0f, e1f = e0.reshape(t), e1.reshape(t)
    pos0 = offs[e0f] + r0.reshape(t)
    pos1 = offs[e1f] + r1.reshape(t)
    tok = jnp.arange(t, dtype=jnp.int32)
    src = jnp.zeros((n_tiles * tm,), jnp.int32).at[jnp.concatenate([pos0, pos1])].set(
        jnp.concatenate([tok, tok]), unique_indices=True)
    tile_start = jnp.arange(n_tiles, dtype=jnp.int32) * tm
    tile_expert = jnp.minimum(jnp.sum((tile_start[:, None] >= ends[None, :]).astype(jnp.int32), axis=1),
                              N_EXPERTS - 1)
    n_used = (ends[-1:] // tm).astype(jnp.int32)
    y = moe_grouped_ffn(a2.reshape(t, d), src, tile_expert, n_used, w_gate, w_up, w_down, layer)
    return moe_combine(y, pos0, pos1, w0, w1, x, mod3, gate_idx, g, **next_norm)


def _rot_cols(w):
    half = w.shape[-1] // 2
    return jnp.concatenate([-w[..., half:], w[..., :half]], axis=-1)


_O_KR = Q_LORA + KV_LORA
_O_DIL = _O_KR + ROPE_DIM
_O_SB = _O_DIL + 3 * DIL_WIDTH
_O_GATES = _O_SB + 3 * SB_WIDTH
IN_COLS = _O_GATES + 3 * D_MODEL
DIL_Q_SCALE = DIL_HEAD_DIM ** -0.5 * LOG2E
SB_Q_SCALE = -(SB_HEAD_DIM ** -0.5) * LOG2E

_PACK_PIECES = [(_O_GATES, 3 * D_MODEL, None), (_O_SB, SB_WIDTH, SB_Q_SCALE), (_O_SB + SB_WIDTH, 2 * SB_WIDTH, None)]
for _g in range(DIL_GROUPS):
    _PACK_PIECES += [(_O_DIL + _g * DIL_GW, DIL_GW, DIL_Q_SCALE),
                     (_O_DIL + DIL_WIDTH + _g * DIL_GW, DIL_GW, None),
                     (_O_DIL + 2 * DIL_WIDTH + _g * DIL_GW, DIL_GW, None)]
_PACK_PIECES.append((0, Q_LORA + KV_LORA, None))
PACK_WIDTH = sum(width for _, width, _ in _PACK_PIECES)
PACKED_COLS = dict(gates=(0, 3 * D_MODEL), sb=(3 * D_MODEL, 3 * SB_WIDTH))
for _g in range(DIL_GROUPS):
    PACKED_COLS[f"dil{_g}"] = (3 * D_MODEL + 3 * SB_WIDTH + _g * 3 * DIL_GW, 3 * DIL_GW)
PACKED_COLS["lat"] = (3 * D_MODEL + 3 * SB_WIDTH + 3 * DIL_WIDTH, Q_LORA + KV_LORA)


PACK_BLOCK = 512


def _pack_kernel(src_ref, scale_ref, wt_ref, o_ref):
    del src_ref
    o_ref[...] = (wt_ref[0].T * scale_ref[pl.program_id(0)]).astype(o_ref.dtype)


def pack_projection_weights(w_in_t, layer):
    _, n, d = w_in_t.shape
    assert n == IN_COLS
    src_rows, scales = [], []
    for src, width, scale in _PACK_PIECES:
        assert width % PACK_BLOCK == 0
        for k in range(width // PACK_BLOCK):
            row = src + k * PACK_BLOCK
            assert row % ROPE_DIM == 0
            src_rows.append(row // ROPE_DIM)
            scales.append(1.0 if scale is None else scale)
    grid_spec = pltpu.PrefetchScalarGridSpec(
        num_scalar_prefetch=1,
        grid=(len(src_rows),),
        in_specs=[pl.BlockSpec(memory_space=pltpu.SMEM),
                  pl.BlockSpec((pl.Element(1), pl.Element(PACK_BLOCK), pl.Element(d)),
                               lambda j, src: (layer, src[j] * ROPE_DIM, 0))],
        out_specs=pl.BlockSpec((d, PACK_BLOCK), lambda j, src: (0, j)),
    )
    return pl.pallas_call(
        _pack_kernel, grid_spec=grid_spec,
        out_shape=jax.ShapeDtypeStruct((d, PACK_WIDTH), BF16),
        compiler_params=_cp(("arbitrary",)),
        name="pack_projection_weights",
    )(jnp.asarray(src_rows, jnp.int32), jnp.asarray(scales, F32), w_in_t)


def _pack_rope_kernel(wt_ref, o_ref):
    t = wt_ref[0]
    half = ROPE_DIM // 2
    both = jnp.concatenate([t, -t[half:], t[:half]], axis=0)
    o_ref[...] = both.T.astype(o_ref.dtype)


def pack_rope_weights(w_in_t, layer):
    d = w_in_t.shape[2]
    return pl.pallas_call(
        _pack_rope_kernel, grid=(1,),
        in_specs=[pl.BlockSpec((1, ROPE_DIM, d), lambda i: (layer, _O_KR // ROPE_DIM, 0))],
        out_specs=pl.BlockSpec((d, 2 * ROPE_DIM), lambda i: (0, 0)),
        out_shape=jax.ShapeDtypeStruct((d, 2 * ROPE_DIM), BF16),
        compiler_params=_cp(("arbitrary",)),
        name="pack_rope_weights",
    )(w_in_t)


def _prep_layer(w_in_t, layer, w_uq, w_ukv, w_o_mla, w_o_dil, w_o_sb, w_out):
    w_pack = pack_projection_weights(w_in_t, layer)
    w_kr = pack_rope_weights(w_in_t, layer)
    q_scale = MLA_QK ** -0.5 * LOG2E
    wq = (w_uq * q_scale).reshape(Q_LORA, MLA_HEADS, MLA_QK).transpose(1, 0, 2)
    wq_h = jnp.concatenate([wq, _rot_cols(wq[..., NOPE_DIM:])], axis=-1).astype(BF16)
    wkv_h = w_ukv.reshape(KV_LORA, MLA_HEADS, NOPE_DIM + MLA_V_DIM).transpose(1, 0, 2).astype(BF16)
    return dict(w_pack=w_pack, w_kr=w_kr, wq_h=wq_h, wkv_h=wkv_h,
                w_o_mla=w_o_mla.astype(BF16), w_o_dil=w_o_dil.astype(BF16), w_o_sb=w_o_sb.astype(BF16),
                w_out=w_out.astype(BF16))


def kernel(x, c, positions, w_ada, b_ada, g_mix, g_moe, w_in, g_q, w_uq, g_kv, w_ukv, w_o_mla, w_o_dil,
           w_o_sb, w_out, w_router, b_router, w_gate, w_up, w_down, g_final):
    b, s, d = x.shape
    depth = w_ada.shape[0]
    pos_f = positions.astype(F32)
    cos, sin = rope_tables(pos_f.reshape(b, s, 1))
    mod = ada_mod(c, w_ada, b_ada)
    w_router_t = w_router.T
    w_in_t = jnp.swapaxes(w_in, 1, 2)
    mods = [mod[l].reshape(b, 1, 6 * d) for l in range(depth)]
    a = norm_mod(x, g_mix[0], mods[0], sc_idx=1, sh_idx=0)
    for l in range(depth):
        p = _prep_layer(w_in_t, l, w_uq[l], w_ukv[l], w_o_mla[l], w_o_dil[l], w_o_sb[l], w_out[l])
        mod3 = mods[l]

        wp = p["w_pack"]
        lat = project(a, wp, *PACKED_COLS["lat"])
        kr = project(a, p["w_kr"], 0, 2 * ROPE_DIM)
        dil = [project_by_residue(a, wp, *PACKED_COLS[f"dil{g}"], DIL_RATES[g]) for g in range(DIL_GROUPS)]
        sbp = project(a, wp, *PACKED_COLS["sb"], tn=1024)
        gl = project(a, wp, *PACKED_COLS["gates"], tn=1024)

        q, k, v = mla_project(lat, kr, g_q[l], p["wq_h"], g_kv[l], p["wkv_h"], cos, sin)
        y_mla = mla_attention(q, k, v)
        dil_out = [dil_group_attention(dil[g], pos_f, g) for g in range(DIL_GROUPS)]
        y_sb = sb_attention(sbp)
        merged = merge_branches(y_mla, [o for o, _ in dil_out], [e for _, e in dil_out], y_sb, gl,
                                p["w_o_mla"], p["w_o_dil"], p["w_o_sb"])
        x, a2, logits_t = resid_project_norm_router(merged, p["w_out"], x, mod3, 2, g_moe[l], 4, 3, w_router_t)
        e0, e1, w0, w1 = router_top2(logits_t, b_router)
        moe_args = (a2, e0, e1, w0, w1, w_gate, w_up, w_down, l, x, mod3, 5)
        if l + 1 < depth:
            x, a = moe_sparse(*moe_args, g_mix[l + 1], next_mod3=mods[l + 1], sc_idx=1, sh_idx=0)
        else:
            out = moe_sparse(*moe_args, g_final)
    return out
```
